```python
import functools
import jax, jax.numpy as jnp
from jax import lax
import numpy as np

D_MODEL = 4096
BATCH = 4
SEQ = 2048
DEPTH = 2
DEC_BATCH = 8
DEC_SEQ = 4
PAST_LEN = 16384
PAGE_SIZE = 128

HG_HEADS = 8
HG_DK = 128
HG_DV = 128
HG_W = HG_HEADS * HG_DV
GLA_CHUNK = 64
EXP_CLIP = 60.0
RW_HEADS = 16
RW_HD = 64
RW_W = RW_HEADS * RW_HD
RW_DECAY_LORA = 64
RW_AAA_LORA = 64
RW_GATE_LORA = 160
RW_GN_EPS = 64e-5
RW_COLS = 3 * RW_W + RW_DECAY_LORA + RW_AAA_LORA + RW_GATE_LORA
AT_HEADS = 16
AT_KV_HEADS = 4
HEAD_DIM = 128
AT_W = AT_HEADS * HEAD_DIM
KV_W = AT_KV_HEADS * HEAD_DIM
IDX_HEADS = 8
IDX_DIM = 128
TOPK_MAX = 256
Q_BLOCK = 128
NEG_BIG = -1e30
HG_COLS = 2 * HG_HEADS * HG_DK + 2 * HG_W
AT_COLS = AT_W + 2 * KV_W + IDX_HEADS * IDX_DIM + IDX_DIM + IDX_HEADS
IN_COLS = HG_COLS + RW_COLS + AT_COLS
D_FF = -(-8 * D_MODEL // (3 * 256)) * 256
NORM_EPS = 1e-6

kernel_name = 'hybrid_hgrn2_rwkv7_dsa_decoder_step'


def _split(a, sizes):
    offs = np.cumsum(sizes)[:-1].tolist()
    return jnp.split(a, offs, axis=-1)


def rms_norm(x, g, eps=NORM_EPS):
    xf = x.astype(jnp.float32)
    y = xf * lax.rsqrt(jnp.mean(xf * xf, axis=-1, keepdims=True) + eps)
    return (y * g.astype(jnp.float32)).astype(x.dtype)


def chunked_gla(q, k, v, log_f, S0):
    B, T, H, DK = q.shape
    DV = v.shape[-1]
    C = GLA_CHUNK if T % GLA_CHUNK == 0 else T
    n = T // C
    to_chunks = lambda a: a.reshape(B, n, C, H, a.shape[-1]).swapaxes(0, 1)
    tril = jnp.tril(jnp.ones((C, C), dtype=bool))[None, :, :, None, None]

    def step(S, inp):
        qc, kc, vc, gc = inp
        b = jnp.cumsum(gc, axis=1)
        o_inter = jnp.einsum('bthk,bhkv->bthv', qc * jnp.exp(b), S)
        diff = b[:, :, None] - b[:, None, :]
        decay = jnp.where(tril, jnp.exp(jnp.where(tril, diff, 0.0)), 0.0)
        A = jnp.einsum('bthk,bshk,btshk->bths', qc, kc, decay)
        o_intra = jnp.einsum('bths,bshv->bthv', A, vc)
        b_last = b[:, -1]
        kd = kc * jnp.exp(b_last[:, None] - b)
        S_new = jnp.exp(b_last)[..., None] * S + jnp.einsum('bshk,bshv->bhkv', kd, vc)
        return S_new, o_inter + o_intra

    S, o = lax.scan(step, S0.astype(jnp.float32),
                    (to_chunks(q), to_chunks(k), to_chunks(v), to_chunks(log_f)))
    return o.swapaxes(0, 1).reshape(B, T, H, DV), S


def hgrn2_mixer(pq, pf, pi, pg, lb, norm_g, S0):
    f32 = jnp.float32
    B, T, _ = pq.shape
    q = jax.nn.silu(pq.astype(f32)).reshape(B, T, HG_HEADS, HG_DK) * (HG_DK ** -0.5)
    fr = pf.astype(f32).reshape(B, T, HG_HEADS, HG_DK)
    lbh = lb.astype(f32).reshape(HG_HEADS, HG_DK)
    log_f = jax.nn.log_sigmoid(fr) + jnp.log1p(lbh * jnp.exp(jnp.minimum(-fr, EXP_CLIP)))
    k = (1.0 - lbh) * jax.nn.sigmoid(-fr)
    v = pi.astype(f32).reshape(B, T, HG_HEADS, HG_DV)
    o, S = chunked_gla(q, k, v, log_f, S0)
    o = rms_norm(o, norm_g.reshape(HG_HEADS, HG_DV)) * jax.nn.sigmoid(pg.astype(f32)).reshape(B, T, HG_HEADS, HG_DV)
    return o.reshape(B, T, HG_W), S


def rwkv7_mixer(p, shift_prev, mu, w0, w2, a0, a2, g2, k_k, k_a, r_k, lnx_w, lnx_b, S0):
    f32 = jnp.float32
    B, T, _ = p.shape
    pf = p.astype(f32)
    prev = jnp.concatenate([shift_prev.astype(f32)[:, None], pf[:, :-1]], axis=1)
    xs = pf + (prev - pf) * mu.astype(f32)
    r, k, v, xw, xa, xg = _split(xs, [RW_W, RW_W, RW_W, RW_DECAY_LORA, RW_AAA_LORA, RW_GATE_LORA])
    w = -jax.nn.softplus(-(w0.astype(f32) + jnp.tanh(xw) @ w2.astype(f32))) - 0.5
    decay = jnp.exp(-jnp.exp(w))
    a = jax.nn.sigmoid(a0.astype(f32) + xa @ a2.astype(f32))
    g = jax.nn.sigmoid(xg) @ g2.astype(f32)
    heads = lambda t: t.reshape(B, T, RW_HEADS, RW_HD)
    kk = heads(k * k_k.astype(f32))
    kk = kk * lax.rsqrt(jnp.maximum(jnp.sum(kk * kk, axis=-1, keepdims=True), 1e-24))
    k = k * (1.0 + (a - 1.0) * k_a.astype(f32))
    r_h, k_h, v_h, w_h, a_h = heads(r), heads(k), heads(v), heads(decay), heads(a)

    def step(S, inp):
        rt, wt, kt, vt, kkt, at = inp
        sa = jnp.einsum('bhvk,bhk->bhv', S, -kkt)
        S = S * wt[:, :, None, :] + sa[..., None] * (kkt * at)[:, :, None, :] + vt[..., None] * kt[:, :, None, :]
        return S, jnp.einsum('bhvk,bhk->bhv', S, rt)

    tm = lambda t: jnp.swapaxes(t, 0, 1)
    S, y = lax.scan(step, S0.astype(f32), (tm(r_h), tm(w_h), tm(k_h), tm(v_h), tm(kk), tm(a_h)))
    y = tm(y)
    mean = jnp.mean(y, axis=-1, keepdims=True)
    var = jnp.mean(jnp.square(y - mean), axis=-1, keepdims=True)
    y = ((y - mean) * lax.rsqrt(var + RW_GN_EPS)).reshape(B, T, RW_W) * lnx_w.astype(f32) + lnx_b.astype(f32)
    bonus = jnp.sum(r_h * k_h * r_k.astype(f32), axis=-1, keepdims=True) * v_h
    out = (y + bonus.reshape(B, T, RW_W)) * g
    return out, S, pf[:, -1]


def indexer_topk(qi, wi, ki, q_pos, n_sel):
    s = jnp.einsum('bthd,bsd->bths', qi.astype(jnp.float32), ki.astype(jnp.float32))
    score = jnp.einsum('bths,bth->bts', jax.nn.relu(s), wi.astype(jnp.float32))
    causal = jnp.arange(ki.shape[1])[None, :] <= q_pos[:, None]
    score = jnp.where(causal[None], score, NEG_BIG)
    _, idx = lax.top_k(score, n_sel)
    valid = idx <= q_pos[None, :, None]
    return idx, valid


def attend_selected(q, k_sel, v_sel, valid):
    B, T = q.shape[:2]
    qg = q.astype(jnp.float32).reshape(B, T, AT_KV_HEADS, AT_HEADS // AT_KV_HEADS, HEAD_DIM)
    s = jnp.einsum('btngd,btknd->btngk', qg, k_sel.astype(jnp.float32)) * (HEAD_DIM ** -0.5)
    s = jnp.where(valid[:, :, None, None, :], s, NEG_BIG)
    p = jax.nn.softmax(s, axis=-1)
    o = jnp.einsum('btngk,btknd->btngd', p, v_sel.astype(jnp.float32))
    return o.reshape(B, T, AT_W)


_gather_rows = jax.vmap(lambda src, ii: src[ii])


def dsa_prompt(q, k, v, qi, wi, ki):
    B, T = q.shape[:2]
    n_sel = min(TOPK_MAX, T // 4)
    qb = min(Q_BLOCK, T)
    nb = T // qb
    blocks = lambda a: a.reshape(B, nb, qb, *a.shape[2:]).swapaxes(0, 1)
    pos = jnp.arange(T, dtype=jnp.int32).reshape(nb, qb)

    def one(args):
        q_b, qi_b, wi_b, p_b = args
        idx, valid = indexer_topk(qi_b, wi_b, ki, p_b, n_sel)
        return attend_selected(q_b, _gather_rows(k, idx), _gather_rows(v, idx), valid)

    o = lax.map(one, (blocks(q), blocks(qi), blocks(wi), pos))
    return o.swapaxes(0, 1).reshape(B, T, AT_W)


def dsa_sample(q, k, v, qi, wi, ki, cache_k_l, cache_v_l, cache_kidx_l, page_table):
    DB, DS = q.shape[:2]
    past = page_table.shape[1] * PAGE_SIZE
    n_sel = min(TOPK_MAX, (past + DS) // 4)
    ki_past = cache_kidx_l[page_table].reshape(DB, past, IDX_DIM)
    ki_all = jnp.concatenate([ki_past.astype(jnp.float32), ki.astype(jnp.float32)], axis=1)
    q_pos = past + jnp.arange(DS, dtype=jnp.int32)
    idx, valid = indexer_topk(qi, wi, ki_all, q_pos, n_sel)
    in_past = (idx < past)[..., None, None]
    ip = jnp.minimum(idx, past - 1)
    phys = page_table[jnp.arange(DB)[:, None, None], ip // PAGE_SIZE]
    off = ip % PAGE_SIZE
    inew = jnp.clip(idx - past, 0, DS - 1)
    k_sel = jnp.where(in_past, cache_k_l[phys, off].astype(jnp.float32), _gather_rows(k, inew).astype(jnp.float32))
    v_sel = jnp.where(in_past, cache_v_l[phys, off].astype(jnp.float32), _gather_rows(v, inew).astype(jnp.float32))
    return attend_selected(q, k_sel, v_sel, valid)


def trunk_layer(x, lp, lb, hg_S0, rw_S0, shift0, attend):
    B, T, _ = x.shape
    h = rms_norm(x, lp['ln1'])
    p = h @ lp['w_in']
    (hq, hf, hi, hg, prw, aq, ak, av, aqi, aki, awi) = _split(
        p, [HG_HEADS * HG_DK, HG_HEADS * HG_DK, HG_W, HG_W, RW_COLS,
            AT_W, KV_W, KV_W, IDX_HEADS * IDX_DIM, IDX_DIM, IDX_HEADS])
    o_hg, hg_S = hgrn2_mixer(hq, hf, hi, hg, lb, lp['hgrn_norm'], hg_S0)
    o_rw, rw_S, shift = rwkv7_mixer(prw, shift0, lp['rwkv_mu'], lp['rwkv_w0'], lp['rwkv_w2'],
                                    lp['rwkv_a0'], lp['rwkv_a2'], lp['rwkv_g2'], lp['rwkv_kk'],
                                    lp['rwkv_ka'], lp['rwkv_rk'], lp['rwkv_lnx_w'], lp['rwkv_lnx_b'], rw_S0)
    q = rms_norm(aq.reshape(B, T, AT_HEADS, HEAD_DIM), lp['q_norm'])
    k = rms_norm(ak.reshape(B, T, AT_KV_HEADS, HEAD_DIM), lp['k_norm'])
    v = av.reshape(B, T, AT_KV_HEADS, HEAD_DIM)
    qi = aqi.reshape(B, T, IDX_HEADS, IDX_DIM)
    wi = awi * (IDX_HEADS ** -0.5 * IDX_DIM ** -0.5)
    o_at = attend(q, k, v, qi, wi, aki)
    mix = jnp.concatenate([o_hg.astype(x.dtype), o_rw.astype(x.dtype), o_at.astype(x.dtype)], axis=-1)
    x = x + mix @ lp['w_out']
    h2 = rms_norm(x, lp['ln2'])
    x = x + (jax.nn.silu(h2 @ lp['w_gate']) * (h2 @ lp['w_up'])) @ lp['w_down']
    return x, (k, v, aki, hg_S, rw_S, shift)


def setup_inputs(seed: int = 0) -> dict:
    key = jax.random.key(seed)
    ks = jax.random.split(key, 32)
    f32 = jnp.float32
    nrm = lambda i, shape, scale=1.0: jax.random.normal(ks[i], shape, f32) * scale
    gain = lambda i, shape: 1.0 + 0.02 * jax.random.normal(ks[i], shape, f32)
    n_pages = PAST_LEN // PAGE_SIZE
    n_pool = (DEC_BATCH * n_pages * 5) // 4
    perm = jax.random.permutation(ks[8], n_pool)
    page_table = perm[:DEC_BATCH * n_pages].reshape(DEC_BATCH, n_pages).astype(jnp.int32)
    return {
        'x_prompt': nrm(0, (BATCH, SEQ, D_MODEL)),
        'x_sample': nrm(1, (DEC_BATCH, DEC_SEQ, D_MODEL)),
        'cache_k': nrm(2, (DEPTH, n_pool, PAGE_SIZE, AT_KV_HEADS, HEAD_DIM)),
        'cache_v': nrm(3, (DEPTH, n_pool, PAGE_SIZE, AT_KV_HEADS, HEAD_DIM)),
        'cache_kidx': nrm(4, (DEPTH, n_pool, PAGE_SIZE, IDX_DIM)),
        'state_hgrn': nrm(5, (DEPTH, DEC_BATCH, HG_HEADS, HG_DK, HG_DV), 0.5),
        'state_rwkv': nrm(6, (DEPTH, DEC_BATCH, RW_HEADS, RW_HD, RW_HD), 0.5),
        'state_shift': nrm(7, (DEPTH, DEC_BATCH, RW_COLS)),
        'page_table': page_table,
        'ln1': gain(9, (DEPTH, D_MODEL)),
        'w_in': nrm(10, (DEPTH, D_MODEL, IN_COLS), D_MODEL ** -0.5),
        'hgrn_lb': nrm(11, (DEPTH, HG_HEADS * HG_DK)),
        'hgrn_norm': gain(12, (DEPTH, HG_W)),
        'rwkv_mu': jax.random.uniform(ks[13], (DEPTH, RW_COLS), f32),
        'rwkv_w0': nrm(14, (DEPTH, RW_W), 0.5) - 1.0,
        'rwkv_w2': nrm(15, (DEPTH, RW_DECAY_LORA, RW_W), 0.5 * RW_DECAY_LORA ** -0.5),
        'rwkv_a0': nrm(16, (DEPTH, RW_W), 0.1),
        'rwkv_a2': nrm(17, (DEPTH, RW_AAA_LORA, RW_W), 0.5 * RW_AAA_LORA ** -0.5),
        'rwkv_g2': nrm(18, (DEPTH, RW_GATE_LORA, RW_W), RW_GATE_LORA ** -0.5),
        'rwkv_kk': 0.85 + nrm(19, (DEPTH, RW_W), 0.05),
        'rwkv_ka': 1.0 + nrm(20, (DEPTH, RW_W), 0.05),
        'rwkv_rk': nrm(21, (DEPTH, RW_HEADS, RW_HD), 0.1),
        'rwkv_lnx_w': gain(22, (DEPTH, RW_W)),
        'rwkv_lnx_b': nrm(23, (DEPTH, RW_W), 0.02),
        'q_norm': gain(24, (DEPTH, HEAD_DIM)),
        'k_norm': gain(25, (DEPTH, HEAD_DIM)),
        'w_out': nrm(26, (DEPTH, D_MODEL, D_MODEL), D_MODEL ** -0.5),
        'ln2': gain(27, (DEPTH, D_MODEL)),
        'w_gate': nrm(28, (DEPTH, D_MODEL, D_FF), D_MODEL ** -0.5),
        'w_up': nrm(29, (DEPTH, D_MODEL, D_FF), D_MODEL ** -0.5),
        'w_down': nrm(30, (DEPTH, D_FF, D_MODEL), D_FF ** -0.5),
    }


def reference(x_prompt, x_sample, cache_k, cache_v, cache_kidx, state_hgrn, state_rwkv, state_shift,
              page_table, ln1, w_in, hgrn_lb, hgrn_norm, rwkv_mu, rwkv_w0, rwkv_w2, rwkv_a0, rwkv_a2,
              rwkv_g2, rwkv_kk, rwkv_ka, rwkv_rk, rwkv_lnx_w, rwkv_lnx_b, q_norm, k_norm, w_out, ln2,
              w_gate, w_up, w_down):
    f32 = jnp.float32
    lb_p = jax.nn.softmax(hgrn_lb.astype(f32), axis=0)
    lbs = jnp.cumsum(lb_p, axis=0) - lb_p[0:1]
    layers = [dict(ln1=ln1[l], w_in=w_in[l], hgrn_norm=hgrn_norm[l], rwkv_mu=rwkv_mu[l],
                   rwkv_w0=rwkv_w0[l], rwkv_w2=rwkv_w2[l], rwkv_a0=rwkv_a0[l], rwkv_a2=rwkv_a2[l],
                   rwkv_g2=rwkv_g2[l], rwkv_kk=rwkv_kk[l], rwkv_ka=rwkv_ka[l], rwkv_rk=rwkv_rk[l],
                   rwkv_lnx_w=rwkv_lnx_w[l], rwkv_lnx_b=rwkv_lnx_b[l], q_norm=q_norm[l], k_norm=k_norm[l],
                   w_out=w_out[l], ln2=ln2[l], w_gate=w_gate[l], w_up=w_up[l], w_down=w_down[l])
              for l in range(DEPTH)]

    def run(x, hg0, rw0, sh0, attend_for_layer):
        outs = []
        for l in range(DEPTH):
            x, st = trunk_layer(x, layers[l], lbs[l], hg0[l], rw0[l], sh0[l], attend_for_layer(l))
            outs.append(st)
        return x, [jnp.stack([o[i] for o in outs]) for i in range(6)]

    B = x_prompt.shape[0]
    hg0_p = jnp.zeros((DEPTH, B, HG_HEADS, HG_DK, HG_DV), f32)
    rw0_p = jnp.zeros((DEPTH, B, RW_HEADS, RW_HD, RW_HD), f32)
    sh0_p = jnp.zeros((DEPTH, B, RW_COLS), f32)
    y_prompt, st_p = run(x_prompt, hg0_p, rw0_p, sh0_p, lambda l: dsa_prompt)
    k_p, v_p, kidx_p, hg_p, rw_p, sh_p = st_p

    sample_attend = lambda l: functools.partial(dsa_sample, cache_k_l=cache_k[l], cache_v_l=cache_v[l],
                                                cache_kidx_l=cache_kidx[l], page_table=page_table)
    y_sample, st_s = run(x_sample, state_hgrn, state_rwkv, state_shift, sample_attend)
    k_s, v_s, kidx_s, hg_s, rw_s, sh_s = st_s

    return (y_prompt, y_sample, k_p, v_p, kidx_p, hg_p, rw_p, sh_p, k_s, v_s, kidx_s, hg_s, rw_s, sh_s)
```

```python
import functools

import jax
import jax.numpy as jnp
import numpy as np
from jax import lax
from jax.experimental import pallas as pl
from jax.experimental.pallas import tpu as pltpu

f32 = jnp.float32
bf16 = jnp.bfloat16

HG_HEADS, HG_DK, HG_DV = 8, 128, 128
HG_W = HG_HEADS * HG_DV
EXP_CLIP = 60.0
RW_HEADS, RW_HD = 16, 64
RW_W = RW_HEADS * RW_HD
RW_DECAY_LORA, RW_AAA_LORA, RW_GATE_LORA = 64, 64, 160
RW_LORA = RW_DECAY_LORA + RW_AAA_LORA + RW_GATE_LORA
RW_GN_EPS = 64e-5
RW_COLS = 3 * RW_W + RW_LORA
AT_HEADS, AT_KV_HEADS, HEAD_DIM = 16, 4, 128
AT_GROUP = AT_HEADS // AT_KV_HEADS
AT_W = AT_HEADS * HEAD_DIM
KV_W = AT_KV_HEADS * HEAD_DIM
IDX_HEADS, IDX_DIM = 8, 128
TOPK_MAX = 256
Q_BLOCK = 128
NEG_BIG = -1e30
NORM_EPS = 1e-6
HG_COLS = 2 * HG_HEADS * HG_DK + 2 * HG_W
AT_COLS = AT_W + 2 * KV_W + IDX_HEADS * IDX_DIM + IDX_DIM + IDX_HEADS

LANES = 128
VMEM_LIMIT = 56 * 1024 * 1024
SAMPLE_ROWS = 16
GLA_CHUNK = 128
GLA_SUB = 16

LORA_PAD = 512
OFF_HG = 0
OFF_AQ = HG_COLS
OFF_AQI = OFF_AQ + AT_W
OFF_AK = OFF_AQI + IDX_HEADS * IDX_DIM
OFF_AV = OFF_AK + KV_W
OFF_AKI = OFF_AV + KV_W
OFF_AWI = OFF_AKI + IDX_DIM
OFF_LORA = 8704
OFF_R = OFF_LORA + LORA_PAD
OFF_K = OFF_R + RW_W
OFF_V = OFF_K + RW_W
IN_PAD = OFF_V + RW_W


def _cparams(sem):
    return pltpu.CompilerParams(dimension_semantics=sem, vmem_limit_bytes=VMEM_LIMIT)


def _sigmoid(x):
    return jax.nn.sigmoid(x)


def _dot(a, b, precision=None):
    return jnp.dot(a, b, preferred_element_type=f32, precision=precision)


def _dot_nt(a, b, precision=None):
    return lax.dot_general(a, b, (((1,), (1,)), ((), ())), preferred_element_type=f32, precision=precision)


def _dot_tn(a, b, precision=None):
    return lax.dot_general(a, b, (((0,), (0,)), ((), ())), preferred_element_type=f32, precision=precision)


HI = lax.Precision.HIGHEST


def _rmsnorm_kernel(x_ref, g_ref, o_ref):
    x = x_ref[...]
    ms = jnp.mean(x * x, axis=-1, keepdims=True)
    o_ref[...] = (x * lax.rsqrt(ms + NORM_EPS) * g_ref[...]).astype(o_ref.dtype)


def _rmsnorm(x, g, layer):
    m, d = x.shape
    tr = min(256, m)
    return pl.pallas_call(
        _rmsnorm_kernel,
        grid=(m // tr,),
        in_specs=[pl.BlockSpec((tr, d), lambda i: (i, 0)),
                  pl.BlockSpec((None, 1, d), lambda i: (layer, 0, 0))],
        out_specs=pl.BlockSpec((tr, d), lambda i: (i, 0)),
        out_shape=jax.ShapeDtypeStruct((m, d), bf16),
        compiler_params=_cparams(("parallel",)),
    )(x, g)


def _mm_kernel(*refs, nk, has_res):
    if has_res:
        a_ref, b_ref, r_ref, o_ref = refs[:4]
    else:
        a_ref, b_ref, o_ref = refs[:3]
        r_ref = None
    part = _dot(a_ref[...], b_ref[...])
    if nk == 1:
        o_ref[...] = (part + r_ref[...]) if has_res else part
        return
    acc_ref = refs[-1]
    k = pl.program_id(2)

    @pl.when(k == 0)
    def _():
        acc_ref[...] = part

    @pl.when(k > 0)
    def _():
        acc_ref[...] += part

    @pl.when(k == nk - 1)
    def _():
        o_ref[...] = (acc_ref[...] + r_ref[...]) if has_res else acc_ref[...]


def _matmul(a, w, layer, *, tm, tn, tk, res=None):
    m, kdim = a.shape
    n = w.shape[-1]
    tm = min(tm, m)
    tn = min(tn, n)
    nk = kdim // tk
    assert m % tm == 0 and n % tn == 0 and kdim % tk == 0
    in_specs = [pl.BlockSpec((tm, tk), lambda i, j, k: (i, k)),
                pl.BlockSpec((None, tk, tn), lambda i, j, k: (layer, k, j))]
    args = [a, w]
    if res is not None:
        in_specs.append(pl.BlockSpec((tm, tn), lambda i, j, k: (i, j)))
        args.append(res)
    scratch = [pltpu.VMEM((tm, tn), f32)] if nk > 1 else []
    return pl.pallas_call(
        functools.partial(_mm_kernel, nk=nk, has_res=res is not None),
        grid=(m // tm, n // tn, nk),
        in_specs=in_specs,
        out_specs=pl.BlockSpec((tm, tn), lambda i, j, k: (i, j)),
        out_shape=jax.ShapeDtypeStruct((m, n), f32),
        scratch_shapes=scratch,
        compiler_params=_cparams(("parallel", "parallel", "arbitrary")),
    )(*args)


def _swiglu_kernel(a_ref, wg_ref, wu_ref, o_ref):
    a = a_ref[...]
    g = _dot(a, wg_ref[...])
    u = _dot(a, wu_ref[...])
    o_ref[...] = (g * _sigmoid(g) * u).astype(o_ref.dtype)


def _swiglu(a, wg, wu, layer, *, tm, tn):
    m, d = a.shape
    n = wg.shape[-1]
    tm = min(tm, m)
    assert m % tm == 0 and n % tn == 0
    wspec = pl.BlockSpec((None, d, tn), lambda i, j: (layer, 0, j))
    return pl.pallas_call(
        _swiglu_kernel,
        grid=(m // tm, n // tn),
        in_specs=[pl.BlockSpec((tm, d), lambda i, j: (i, 0)), wspec, wspec],
        out_specs=pl.BlockSpec((tm, tn), lambda i, j: (i, j)),
        out_shape=jax.ShapeDtypeStruct((m, n), bf16),
        compiler_params=_cparams(("parallel", "parallel")),
    )(a, wg, wu)


def _gla_kernel(pq_ref, pf_ref, pi_ref, pg_ref, lb_ref, g_ref, s0_ref, o_ref, sout_ref, st_ref,
                *, rows, valid):
    i = pl.program_id(2)
    chunk, sub = GLA_CHUNK, GLA_SUB
    live_rows = min(rows, chunk)

    @pl.when(i == 0)
    def _():
        st_ref[...] = s0_ref[...].T

    lb = lb_ref[...]
    gain = g_ref[...]
    n_live = live_rows if valid is None else min(valid, live_rows)
    nsb = -(-n_live // sub)
    crow = lax.broadcasted_iota(jnp.int32, (chunk, 1), 0)
    tri = (lax.broadcasted_iota(jnp.int32, (chunk, chunk), 0)
           >= lax.broadcasted_iota(jnp.int32, (chunk, chunk), 1)).astype(f32)
    sub_row = lax.broadcasted_iota(jnp.int32, (sub, 1), 0)

    def padded(x):
        if live_rows == chunk:
            return x
        return jnp.concatenate([x, jnp.zeros((chunk - live_rows, x.shape[1]), x.dtype)], axis=0)

    def one_chunk(c, carry):
        rs = pl.ds(pl.multiple_of(c * live_rows, live_rows), live_rows)
        pq = padded(pq_ref[rs, :])
        fr = padded(pf_ref[rs, :])
        v = padded(pi_ref[rs, :])
        q = pq * _sigmoid(pq) * (HG_DK ** -0.5)
        log_sig = jnp.minimum(fr, 0.0) - jnp.log1p(jnp.exp(-jnp.abs(fr)))
        log_f = log_sig + jnp.log1p(lb * jnp.exp(jnp.minimum(-fr, EXP_CLIP)))
        k = (1.0 - lb) * _sigmoid(-fr)
        if valid is not None or live_rows != chunk:
            live = crow < live_rows
            if valid is not None:
                live = live & (i * rows + c * live_rows + crow < valid)
            log_f = jnp.where(live, log_f, 0.0)
            k = jnp.where(live, k, 0.0)
        b = _dot(tri, log_f, precision=HI)
        st = st_ref[...]
        o = _dot_nt((q * jnp.exp(b)).astype(bf16), st.astype(bf16))
        v16 = v.astype(bf16)
        a_rows, d_rows = [], []
        for ib in range(nsb):
            lo = ib * sub
            q_i = q[lo:lo + sub]
            b_i = b[lo:lo + sub]
            if ib == 0:
                a_rows.append(jnp.zeros((sub, chunk), f32))
            else:
                ref_row = b[lo - 1:lo]
                qt = (q_i * jnp.exp(b_i - ref_row)).astype(bf16)
                kt = jnp.where(crow < lo, k * jnp.exp(jnp.minimum(ref_row - b, 0.0)), 0.0).astype(bf16)
                a_rows.append(_dot_nt(qt, kt))
            d_i = jnp.zeros((sub, HG_DV), f32)
            for s in range(sub):
                gs = lo + s
                term = q_i * jnp.exp(jnp.minimum(b_i - b[gs:gs + 1], 0.0)) * k[gs:gs + 1]
                a_col = jnp.sum(term, axis=1, keepdims=True)
                d_i = d_i + jnp.where(sub_row >= s, a_col, 0.0) * v[gs:gs + 1]
            d_rows.append(d_i)
        if nsb * sub < chunk:
            a_rows.append(jnp.zeros((chunk - nsb * sub, chunk), f32))
            d_rows.append(jnp.zeros((chunk - nsb * sub, HG_DV), f32))
        a_off = jnp.concatenate(a_rows, axis=0)
        o = o + _dot(a_off.astype(bf16), v16) + jnp.concatenate(d_rows, axis=0)
        b_last = b[chunk - 1:chunk]
        kd = (k * jnp.exp(b_last - b)).astype(bf16)
        st_ref[...] = st * jnp.exp(b_last) + _dot_tn(v16, kd)
        o = o[:live_rows]
        ms = jnp.mean(o * o, axis=-1, keepdims=True)
        y = o * lax.rsqrt(ms + NORM_EPS) * gain * _sigmoid(pg_ref[rs, :])
        o_ref[rs, :] = y.astype(o_ref.dtype)
        return carry

    lax.fori_loop(0, rows // live_rows, one_chunk, 0)

    @pl.when(i == pl.num_programs(2) - 1)
    def _():
        sout_ref[...] = st_ref[...].T


def _gla(p, lb, gain, s0, layer, *, nb, tp, rows, valid):
    nblk = tp // rows
    H = HG_HEADS

    def pspec(part):
        return pl.BlockSpec((rows, LANES), lambda b, h, i: (b * nblk + i, part * H + h))

    return pl.pallas_call(
        functools.partial(_gla_kernel, rows=rows, valid=valid),
        grid=(nb, H, nblk),
        in_specs=[pspec(0), pspec(1), pspec(2), pspec(3),
                  pl.BlockSpec((None, 1, LANES), lambda b, h, i: (layer, 0, h)),
                  pl.BlockSpec((None, 1, LANES), lambda b, h, i: (layer, 0, h)),
                  pl.BlockSpec((None, None, HG_DK, HG_DV), lambda b, h, i: (b, h, 0, 0))],
        out_specs=[pl.BlockSpec((rows, LANES), lambda b, h, i: (b * nblk + i, h)),
                   pl.BlockSpec((None, None, HG_DK, HG_DV), lambda b, h, i: (b, h, 0, 0))],
        out_shape=[jax.ShapeDtypeStruct((nb * tp, HG_W), bf16),
                   jax.ShapeDtypeStruct((nb, H, HG_DK, HG_DV), f32)],
        scratch_shapes=[pltpu.VMEM((HG_DV, HG_DK), f32)],
        compiler_params=_cparams(("parallel", "parallel", "arbitrary")),
    )(p, p, p, p, lb, gain, s0)


def _head_sum(x, hm):
    cols = [_dot(x[:, c * LANES:(c + 1) * LANES], hm, precision=HI) for c in range(RW_W // LANES)]
    return jnp.concatenate(cols, axis=1)


def _head_matrix():
    r = lax.broadcasted_iota(jnp.int32, (LANES, LANES), 0) // RW_HD
    c = lax.broadcasted_iota(jnp.int32, (LANES, LANES), 1) // RW_HD
    return (r == c).astype(f32)


def _rw_pre_kernel(pr_ref, pk_ref, pv_ref, pl_ref, sr_ref, sk_ref, sv_ref, sl_ref,
                   mr_ref, mk_ref, mv_ref, ml_ref, w0_ref, w2_ref, a0_ref, a2_ref, g2_ref,
                   kkw_ref, kaw_ref, rkw_ref,
                   r_o, w_o, k_o, v_o, nkk_o, b_o, bonus_o, g_o,
                   cr_ref, ck_ref, cv_ref, cl_ref, *, rows, valid):
    i = pl.program_id(1)

    @pl.when(i == 0)
    def _():
        cr_ref[...] = sr_ref[...]
        ck_ref[...] = sk_ref[...]
        cv_ref[...] = sv_ref[...]
        cl_ref[...] = sl_ref[...]

    def mixed(p_ref, c_ref, m_ref):
        cur = p_ref[...]
        rolled = pltpu.roll(cur, 1, axis=0)
        rowid = lax.broadcasted_iota(jnp.int32, cur.shape, 0)
        prev = jnp.where(rowid == 0, c_ref[...], rolled)
        c_ref[...] = cur[rows - 1:rows, :]
        return cur + (prev - cur) * m_ref[...]

    r = mixed(pr_ref, cr_ref, mr_ref)
    k = mixed(pk_ref, ck_ref, mk_ref)
    v = mixed(pv_ref, cv_ref, mv_ref)
    xl = mixed(pl_ref, cl_ref, ml_ref)

    zw = w0_ref[...] + _dot(jnp.tanh(xl).astype(bf16), w2_ref[...])
    w = jnp.minimum(zw, 0.0) - jnp.log1p(jnp.exp(-jnp.abs(zw))) - 0.5
    decay = jnp.exp(-jnp.exp(w))
    a = _sigmoid(a0_ref[...] + _dot(xl.astype(bf16), a2_ref[...]))
    g = _dot(_sigmoid(xl).astype(bf16), g2_ref[...])

    hm = _head_matrix()
    kk = k * kkw_ref[...]
    kk = kk * lax.rsqrt(jnp.maximum(_head_sum(kk * kk, hm), 1e-24))
    k2 = k * (1.0 + (a - 1.0) * kaw_ref[...])
    bonus = _head_sum(r * k2 * rkw_ref[...], hm) * v
    nkk = -kk
    bb = kk * a
    if valid is not None:
        tok = i * rows + lax.broadcasted_iota(jnp.int32, (rows, 1), 0)
        live = tok < valid
        decay = jnp.where(live, decay, 1.0)
        k2 = jnp.where(live, k2, 0.0)
        v = jnp.where(live, v, 0.0)
        nkk = jnp.where(live, nkk, 0.0)
        bb = jnp.where(live, bb, 0.0)
    r_o[...] = r
    w_o[...] = decay
    k_o[...] = k2
    v_o[...] = v
    nkk_o[...] = nkk
    b_o[...] = bb
    bonus_o[...] = bonus
    g_o[...] = g


def _rw_pre(p, shift_pad, mu_pad, w0, w2p, a0, a2p, g2p, kkw, kaw, rkw, layer, *, nb, tp, rows, valid):
    nblk = tp // rows
    W = RW_W

    def pspec(off, width):
        return pl.BlockSpec((rows, width), lambda b, i: (b * nblk + i, off // width))

    def sspec(off, width):
        return pl.BlockSpec((None, 1, width), lambda b, i: (b, 0, off // width))

    def mspec(off, width):
        return pl.BlockSpec((None, 1, width), lambda b, i: (layer, 0, off // width))

    def vec():
        return pl.BlockSpec((None, 1, W), lambda b, i: (layer, 0, 0))

    def lora():
        return pl.BlockSpec((None, LORA_PAD, W), lambda b, i: (layer, 0, 0))

    out_spec = pl.BlockSpec((rows, W), lambda b, i: (b * nblk + i, 0))
    out_shape = jax.ShapeDtypeStruct((nb * tp, W), f32)
    return pl.pallas_call(
        functools.partial(_rw_pre_kernel, rows=rows, valid=valid),
        grid=(nb, nblk),
        in_specs=[pspec(OFF_R, W), pspec(OFF_K, W), pspec(OFF_V, W), pspec(OFF_LORA, LORA_PAD),
                  sspec(0, W), sspec(W, W), sspec(2 * W, W), sspec(3 * W, LORA_PAD),
                  mspec(0, W), mspec(W, W), mspec(2 * W, W), mspec(3 * W, LORA_PAD),
                  vec(), lora(), vec(), lora(), lora(), vec(), vec(), vec()],
        out_specs=[out_spec] * 8,
        out_shape=[out_shape] * 8,
        scratch_shapes=[pltpu.VMEM((1, W), f32), pltpu.VMEM((1, W), f32), pltpu.VMEM((1, W), f32),
                        pltpu.VMEM((1, LORA_PAD), f32)],
        compiler_params=_cparams(("parallel", "arbitrary")),
    )(p, p, p, p, shift_pad, shift_pad, shift_pad, shift_pad, mu_pad, mu_pad, mu_pad, mu_pad,
      w0, w2p, a0, a2p, g2p, kkw, kaw, rkw)


def _rw_scan_kernel(r_ref, w_ref, k_ref, b_ref, nkk_ref, v_ref, s0_ref, y_ref, sout_ref, s_ref,
                    *, tt, ki_n, fold):
    i = pl.program_id(0)

    @pl.when(i == 0)
    def _():
        s_ref[...] = s0_ref[...]

    def lane_total(x):
        return x + pltpu.roll(x, LANES // 2, axis=1) if fold else x

    def token(t, carry):
        acc = jnp.zeros((RW_HD, LANES), f32)
        for ki in range(ki_n):
            acc = acc + s_ref[ki] * nkk_ref[t, pl.ds(ki, 1), :]
        sa = lane_total(acc)
        vt = v_ref[t]
        y = jnp.zeros((RW_HD, LANES), f32)
        for ki in range(ki_n):
            s_new = (s_ref[ki] * w_ref[t, pl.ds(ki, 1), :] + sa * b_ref[t, pl.ds(ki, 1), :]
                     + vt * k_ref[t, pl.ds(ki, 1), :])
            s_ref[ki] = s_new
            y = y + s_new * r_ref[t, pl.ds(ki, 1), :]
        y_ref[t] = lane_total(y)
        return carry

    lax.fori_loop(0, tt, token, 0)

    @pl.when(i == pl.num_programs(0) - 1)
    def _():
        sout_ref[...] = s_ref[...]


def _rw_scan(rT, wT, kT, bT, nkkT, vT, s0T, *, tt):
    t_len, ki_n, _ = rT.shape
    fold = ki_n * 2 == RW_HD
    assert fold or ki_n == RW_HD
    tt = min(tt, t_len)
    op = pl.BlockSpec((tt, ki_n, LANES), lambda i: (i, 0, 0))
    vs = pl.BlockSpec((tt, RW_HD, LANES), lambda i: (i, 0, 0))
    ss = pl.BlockSpec((ki_n, RW_HD, LANES), lambda i: (0, 0, 0))
    return pl.pallas_call(
        functools.partial(_rw_scan_kernel, tt=tt, ki_n=ki_n, fold=fold),
        grid=(t_len // tt,),
        in_specs=[op, op, op, op, op, vs, ss],
        out_specs=[vs, ss],
        out_shape=[jax.ShapeDtypeStruct((t_len, RW_HD, LANES), f32),
                   jax.ShapeDtypeStruct((ki_n, RW_HD, LANES), f32)],
        scratch_shapes=[pltpu.VMEM((ki_n, RW_HD, LANES), f32)],
        compiler_params=_cparams(("arbitrary",)),
    )(rT, wT, kT, bT, nkkT, vT, s0T)


def _rw_post_kernel(y_ref, bonus_ref, g_ref, lw_ref, lb_ref, o_ref):
    hm = _head_matrix()
    y = y_ref[...]
    mean = _head_sum(y, hm) * (1.0 / RW_HD)
    d = y - mean
    var = _head_sum(d * d, hm) * (1.0 / RW_HD)
    yn = d * lax.rsqrt(var + RW_GN_EPS) * lw_ref[...] + lb_ref[...]
    o_ref[...] = ((yn + bonus_ref[...]) * g_ref[...]).astype(o_ref.dtype)


def _rw_post(y, bonus, g, lnw, lnb, layer):
    m = y.shape[0]
    tr = min(512, m)
    spec = pl.BlockSpec((tr, RW_W), lambda i: (i, 0))
    vec = pl.BlockSpec((None, 1, RW_W), lambda i: (layer, 0, 0))
    return pl.pallas_call(
        _rw_post_kernel,
        grid=(m // tr,),
        in_specs=[spec, spec, spec, vec, vec],
        out_specs=spec,
        out_shape=jax.ShapeDtypeStruct((m, RW_W), bf16),
        compiler_params=_cparams(("parallel",)),
    )(y, bonus, g, lnw, lnb)


def _rwkv(p, shift0, s0, wts, layer, *, nb, tp, t_valid, rows):
    H, N = RW_HEADS, RW_HD
    kh = LANES // (nb * H)
    assert kh in (1, 2) and kh * nb * H == LANES
    ki_n = N // kh
    shift_pad = jnp.pad(shift0, ((0, 0), (0, 3 * RW_W + LORA_PAD - RW_COLS)))[:, None, :]
    valid = None if t_valid == tp else t_valid
    r, w, k, v, nkk, bb, bonus, g = _rw_pre(
        p, shift_pad, wts["mu_pad"], wts["w0"], wts["w2p"], wts["a0"], wts["a2p"], wts["g2p"],
        wts["kkw"], wts["kaw"], wts["rkw"], layer, nb=nb, tp=tp, rows=rows, valid=valid)

    def key_lanes(x):
        x = x.reshape(nb, tp, H, kh, ki_n)[:, :t_valid]
        return x.transpose(1, 4, 3, 0, 2).reshape(t_valid, ki_n, LANES)

    vT = v.reshape(nb, tp, H, N)[:, :t_valid].transpose(1, 3, 0, 2).reshape(t_valid, N, 1, nb * H)
    vT = jnp.broadcast_to(vT, (t_valid, N, kh, nb * H)).reshape(t_valid, N, LANES)
    s0T = s0.reshape(nb, H, N, kh, ki_n).transpose(4, 2, 3, 0, 1).reshape(ki_n, N, LANES)
    yT, sT = _rw_scan(key_lanes(r), key_lanes(w), key_lanes(k), key_lanes(bb), key_lanes(nkk), vT, s0T,
                      tt=64)
    y = yT[:, :, :nb * H].reshape(t_valid, N, nb, H).transpose(2, 0, 3, 1).reshape(nb, t_valid, RW_W)
    if t_valid != tp:
        y = jnp.pad(y, ((0, 0), (0, tp - t_valid), (0, 0)))
    y = y.reshape(nb * tp, RW_W)
    s_out = sT.reshape(ki_n, N, kh, nb, H).transpose(3, 4, 1, 2, 0).reshape(nb, H, N, N)
    o = _rw_post(y, bonus, g, wts["lnw"], wts["lnb"], layer)
    return o, s_out


def _kv_prep_kernel(ak_ref, av_ref, aki_ref, kn_ref, k_o, v_o, ki_o):
    gain = kn_ref[...]
    for n in range(AT_KV_HEADS):
        x = ak_ref[:, n * HEAD_DIM:(n + 1) * HEAD_DIM]
        ms = jnp.mean(x * x, axis=-1, keepdims=True)
        k_o[:, n * HEAD_DIM:(n + 1) * HEAD_DIM] = x * lax.rsqrt(ms + NORM_EPS) * gain
    v_o[...] = av_ref[...]
    ki_o[...] = aki_ref[...]


def _kv_prep(p, k_norm, layer):
    m = p.shape[0]
    tr = min(512, m)
    return pl.pallas_call(
        _kv_prep_kernel,
        grid=(m // tr,),
        in_specs=[pl.BlockSpec((tr, KV_W), lambda i: (i, OFF_AK // KV_W)),
                  pl.BlockSpec((tr, KV_W), lambda i: (i, OFF_AV // KV_W)),
                  pl.BlockSpec((tr, IDX_DIM), lambda i: (i, OFF_AKI // IDX_DIM)),
                  pl.BlockSpec((None, 1, HEAD_DIM), lambda i: (layer, 0, 0))],
        out_specs=[pl.BlockSpec((tr, KV_W), lambda i: (i, 0)),
                   pl.BlockSpec((tr, KV_W), lambda i: (i, 0)),
                   pl.BlockSpec((tr, IDX_DIM), lambda i: (i, 0))],
        out_shape=[jax.ShapeDtypeStruct((m, KV_W), f32), jax.ShapeDtypeStruct((m, KV_W), f32),
                   jax.ShapeDtypeStruct((m, IDX_DIM), f32)],
        compiler_params=_cparams(("parallel",)),
    )(p, p, p, k_norm)


def _index_scores(qi, wi_col, keys):
    rws = qi.shape[0]
    qs = jnp.concatenate([qi[:, h * IDX_DIM:(h + 1) * IDX_DIM] for h in range(IDX_HEADS)], axis=0)
    d = jnp.maximum(_dot_nt(qs, keys, precision=HI), 0.0) * wi_col
    s = d[0:rws]
    for h in range(1, IDX_HEADS):
        s = s + d[h * rws:(h + 1) * rws]
    return s


def _wi_column(awi):
    scale = IDX_HEADS ** -0.5 * IDX_DIM ** -0.5
    return jnp.concatenate([awi[:, h:h + 1] for h in range(IDX_HEADS)], axis=0) * scale


def _sortable(score):
    bits = lax.bitcast_convert_type(score, jnp.int32)
    key = jnp.where(bits < 0, bits ^ jnp.int32(0x7FFFFFFF), bits)
    return jnp.where(score == 0.0, 0, key)


def _count(mask):
    return jnp.sum(mask.astype(f32), axis=1, keepdims=True)


def _select_topk(skey, n_sel, n_keys):
    rws = skey.shape[0]
    nsel = jnp.float32(n_sel)
    int_min = jnp.int32(-2 ** 31)
    zero = jnp.zeros((rws, 1), jnp.int32)
    cand = jnp.where(_count(skey >= zero) >= nsel, zero, zero + int_min)

    def bit_step(it, cand):
        trial = cand + jnp.left_shift(jnp.int32(1), 30 - it)
        return jnp.where(_count(skey >= trial) >= nsel, trial, cand)

    tau = lax.fori_loop(0, 31, bit_step, cand)
    gt = skey > tau
    eq = skey == tau
    need = nsel - _count(gt)
    idx = lax.broadcasted_iota(jnp.int32, skey.shape, 1)
    nbits = int(n_keys).bit_length()

    def idx_step(it, x):
        trial = x + jnp.left_shift(jnp.int32(1), nbits - 1 - it)
        ok = (trial <= n_keys) & (_count(eq & (idx < trial)) < need)
        return jnp.where(ok, trial, x)

    x = lax.fori_loop(0, nbits, idx_step, zero)
    return gt | (eq & (idx <= x))


def _q_heads(aq, gain, n):
    outs = []
    for g in range(AT_GROUP):
        h = n * AT_GROUP + g
        x = aq[:, h * HEAD_DIM:(h + 1) * HEAD_DIM]
        ms = jnp.mean(x * x, axis=-1, keepdims=True)
        outs.append(x * lax.rsqrt(ms + NORM_EPS) * gain)
    return jnp.concatenate(outs, axis=0)


def _dsa_prompt_kernel(aq_ref, aqi_ref, awi_ref, k_ref, v_ref, ki_ref, qn_ref, o_ref, sel_ref,
                       *, t_len, n_sel):
    i = pl.program_id(1)
    qb = Q_BLOCK
    nkb = t_len // qb
    qi = aqi_ref[...]
    wi_col = _wi_column(awi_ref[...])
    q_pos = i * qb + lax.broadcasted_iota(jnp.int32, (qb, 1), 0)

    for j in range(nkb):
        cols = slice(j * qb, (j + 1) * qb)

        @pl.when(j <= i)
        def _():
            s = _index_scores(qi, wi_col, ki_ref[cols, :])
            key_pos = j * qb + lax.broadcasted_iota(jnp.int32, (qb, qb), 1)
            s = jnp.where(key_pos <= q_pos, s, NEG_BIG)
            sel_ref[:, cols] = _sortable(s)

        @pl.when(j > i)
        def _():
            sel_ref[:, cols] = _sortable(jnp.full((qb, qb), NEG_BIG, f32))

    chosen = _select_topk(sel_ref[...], n_sel, t_len)
    key_pos = lax.broadcasted_iota(jnp.int32, (qb, t_len), 1)
    keep = jnp.where(chosen & (key_pos <= q_pos), 1.0, 0.0)
    keep4 = jnp.concatenate([keep] * AT_GROUP, axis=0) > 0.5
    gain = qn_ref[...]
    aq = aq_ref[...]
    for n in range(AT_KV_HEADS):
        kn = k_ref[:, n * HEAD_DIM:(n + 1) * HEAD_DIM].astype(bf16)
        vn = v_ref[:, n * HEAD_DIM:(n + 1) * HEAD_DIM].astype(bf16)
        q4 = _q_heads(aq, gain, n).astype(bf16)
        s = _dot_nt(q4, kn) * (HEAD_DIM ** -0.5)
        s = jnp.where(keep4, s, NEG_BIG)
        m = jnp.max(s, axis=1, keepdims=True)
        e = jnp.exp(s - m)
        l = jnp.sum(e, axis=1, keepdims=True)
        o = _dot(e.astype(bf16), vn) / l
        for g in range(AT_GROUP):
            h = n * AT_GROUP + g
            o_ref[:, h * HEAD_DIM:(h + 1) * HEAD_DIM] = o[g * qb:(g + 1) * qb].astype(o_ref.dtype)


def _dsa_prompt(p, k, v, ki, q_norm, layer, *, nb, t_len):
    n_sel = min(TOPK_MAX, t_len // 4)
    nq = t_len // Q_BLOCK
    return pl.pallas_call(
        functools.partial(_dsa_prompt_kernel, t_len=t_len, n_sel=n_sel),
        grid=(nb, nq),
        in_specs=[pl.BlockSpec((Q_BLOCK, AT_W), lambda b, i: (b * nq + i, OFF_AQ // AT_W)),
                  pl.BlockSpec((Q_BLOCK, IDX_HEADS * IDX_DIM),
                               lambda b, i: (b * nq + i, OFF_AQI // (IDX_HEADS * IDX_DIM))),
                  pl.BlockSpec((Q_BLOCK, LANES), lambda b, i: (b * nq + i, OFF_AWI // LANES)),
                  pl.BlockSpec((t_len, KV_W), lambda b, i: (b, 0)),
                  pl.BlockSpec((t_len, KV_W), lambda b, i: (b, 0)),
                  pl.BlockSpec((t_len, IDX_DIM), lambda b, i: (b, 0)),
                  pl.BlockSpec((None, 1, HEAD_DIM), lambda b, i: (layer, 0, 0))],
        out_specs=pl.BlockSpec((Q_BLOCK, AT_W), lambda b, i: (b * nq + i, 0)),
        out_shape=jax.ShapeDtypeStruct((nb * t_len, AT_W), bf16),
        scratch_shapes=[pltpu.VMEM((Q_BLOCK, t_len), jnp.int32)],
        compiler_params=_cparams(("parallel", "arbitrary")),
    )(p, p, p, k, v, ki, q_norm)


def _dsa_sample_score_kernel(pt_ref, aqi_ref, awi_ref, page_ref, knew_ref, o_ref, *, n_pages, valid):
    pg = pl.program_id(1)
    rws = SAMPLE_ROWS
    ps = page_ref.shape[0]
    qi = aqi_ref[...]
    wi_col = _wi_column(awi_ref[...])

    @pl.when(pg < n_pages)
    def _():
        o_ref[...] = _index_scores(qi, wi_col, page_ref[...])

    @pl.when(pg == n_pages)
    def _():
        keys = jnp.concatenate([knew_ref[...], jnp.zeros((ps - rws, IDX_DIM), f32)], axis=0)
        s = _index_scores(qi, wi_col, keys)
        key_i = lax.broadcasted_iota(jnp.int32, (rws, ps), 1)
        q_i = lax.broadcasted_iota(jnp.int32, (rws, ps), 0)
        o_ref[...] = jnp.where((key_i <= q_i) & (key_i < valid), s, NEG_BIG)


def _dsa_sample_scores(p, ki_new, cache_kidx, page_table, layer, *, nb, valid):
    n_pages = page_table.shape[1]
    ps = cache_kidx.shape[2]
    rws = SAMPLE_ROWS
    grid_spec = pltpu.PrefetchScalarGridSpec(
        num_scalar_prefetch=1,
        grid=(nb, n_pages + 1),
        in_specs=[pl.BlockSpec((rws, IDX_HEADS * IDX_DIM),
                               lambda b, g, pt: (b, OFF_AQI // (IDX_HEADS * IDX_DIM))),
                  pl.BlockSpec((rws, LANES), lambda b, g, pt: (b, OFF_AWI // LANES)),
                  pl.BlockSpec((None, None, ps, IDX_DIM),
                               lambda b, g, pt: (layer, pt[b, jnp.minimum(g, n_pages - 1)], 0, 0)),
                  pl.BlockSpec((rws, IDX_DIM), lambda b, g, pt: (b, 0))],
        out_specs=pl.BlockSpec((None, rws, ps), lambda b, g, pt: (b, 0, g)),
    )
    return pl.pallas_call(
        functools.partial(_dsa_sample_score_kernel, n_pages=n_pages, valid=valid),
        grid_spec=grid_spec,
        out_shape=jax.ShapeDtypeStruct((nb, rws, (n_pages + 1) * ps), f32),
        compiler_params=_cparams(("parallel", "arbitrary")),
    )(page_table, p, p, cache_kidx, ki_new)


def _dsa_sample_select_kernel(s_ref, o_ref, *, n_sel, n_keys):
    chosen = _select_topk(_sortable(s_ref[...]), n_sel, n_keys)
    o_ref[...] = jnp.where(chosen & (s_ref[...] > 0.5 * NEG_BIG), 1.0, 0.0)


def _dsa_sample_select(scores, n_sel):
    nb, rws, n_keys = scores.shape
    spec = pl.BlockSpec((None, rws, n_keys), lambda b: (b, 0, 0))
    return pl.pallas_call(
        functools.partial(_dsa_sample_select_kernel, n_sel=n_sel, n_keys=n_keys),
        grid=(nb,),
        in_specs=[spec],
        out_specs=spec,
        out_shape=jax.ShapeDtypeStruct(scores.shape, f32),
        compiler_params=_cparams(("parallel",)),
    )(scores)


def _dsa_sample_attn_kernel(pt_ref, aq_ref, keep_ref, kpage_ref, vpage_ref, knew_ref, vnew_ref, qn_ref,
                            o_ref, m_ref, l_ref, acc_ref, *, n_pages):
    pg = pl.program_id(1)
    rws = SAMPLE_ROWS
    ps = kpage_ref.shape[0]

    @pl.when(pg == 0)
    def _():
        m_ref[...] = jnp.full(m_ref.shape, NEG_BIG, f32)
        l_ref[...] = jnp.zeros(l_ref.shape, f32)
        acc_ref[...] = jnp.zeros(acc_ref.shape, f32)

    def step(kblk, vblk):
        keep4 = jnp.concatenate([keep_ref[...]] * AT_GROUP, axis=0) > 0.5
        gain = qn_ref[...]
        aq = aq_ref[...]
        for n in range(AT_KV_HEADS):
            kn = kblk[:, n * HEAD_DIM:(n + 1) * HEAD_DIM].astype(bf16)
            vn = vblk[:, n * HEAD_DIM:(n + 1) * HEAD_DIM].astype(bf16)
            q4 = _q_heads(aq, gain, n).astype(bf16)
            s = _dot_nt(q4, kn) * (HEAD_DIM ** -0.5)
            s = jnp.where(keep4, s, NEG_BIG)
            m_old = m_ref[n]
            m_new = jnp.maximum(m_old, jnp.max(s, axis=1, keepdims=True))
            alpha = jnp.exp(m_old - m_new)
            e = jnp.where(keep4, jnp.exp(s - m_new), 0.0)
            l_ref[n] = alpha * l_ref[n] + jnp.sum(e, axis=1, keepdims=True)
            acc_ref[n] = alpha * acc_ref[n] + _dot(e.astype(bf16), vn)
            m_ref[n] = m_new

    @pl.when(pg < n_pages)
    def _():
        step(kpage_ref[...], vpage_ref[...])

    @pl.when(pg == n_pages)
    def _():
        pad = jnp.zeros((ps - rws, KV_W), f32)
        step(jnp.concatenate([knew_ref[...], pad], axis=0), jnp.concatenate([vnew_ref[...], pad], axis=0))
        for n in range(AT_KV_HEADS):
            o = acc_ref[n] / l_ref[n]
            for g in range(AT_GROUP):
                h = n * AT_GROUP + g
                o_ref[:, h * HEAD_DIM:(h + 1) * HEAD_DIM] = o[g * rws:(g + 1) * rws].astype(o_ref.dtype)


def _dsa_sample_attn(p, keep, cache_k, cache_v, k_new, v_new, q_norm, page_table, layer, *, nb):
    n_pages = page_table.shape[1]
    ps = cache_k.shape[2]
    rws = SAMPLE_ROWS
    page = pl.BlockSpec((None, None, ps, KV_W),
                        lambda b, g, pt: (layer, pt[b, jnp.minimum(g, n_pages - 1)], 0, 0))
    new = pl.BlockSpec((rws, KV_W), lambda b, g, pt: (b, 0))
    grid_spec = pltpu.PrefetchScalarGridSpec(
        num_scalar_prefetch=1,
        grid=(nb, n_pages + 1),
        in_specs=[pl.BlockSpec((rws, AT_W), lambda b, g, pt: (b, OFF_AQ // AT_W)),
                  pl.BlockSpec((None, rws, ps), lambda b, g, pt: (b, 0, g)),
                  page, page, new, new,
                  pl.BlockSpec((None, 1, HEAD_DIM), lambda b, g, pt: (layer, 0, 0))],
        out_specs=pl.BlockSpec((rws, AT_W), lambda b, g, pt: (b, 0)),
        scratch_shapes=[pltpu.VMEM((AT_KV_HEADS, AT_GROUP * rws, 1), f32),
                        pltpu.VMEM((AT_KV_HEADS, AT_GROUP * rws, 1), f32),
                        pltpu.VMEM((AT_KV_HEADS, AT_GROUP * rws, HEAD_DIM), f32)],
    )
    return pl.pallas_call(
        functools.partial(_dsa_sample_attn_kernel, n_pages=n_pages),
        grid_spec=grid_spec,
        out_shape=jax.ShapeDtypeStruct((nb * rws, AT_W), bf16),
        compiler_params=_cparams(("parallel", "arbitrary")),
    )(page_table, p, keep, cache_k, cache_v, k_new, v_new, q_norm)


def _prep_weights(w_in, rwkv_mu, rwkv_w2, rwkv_a2, rwkv_g2, w_out, w_gate, w_up, w_down):
    depth, d, _ = w_in.shape
    hg, rw, at = jnp.split(w_in.astype(bf16), [HG_COLS, HG_COLS + RW_COLS], axis=-1)
    aq, ak, av, aqi, aki, awi = jnp.split(at, np.cumsum([AT_W, KV_W, KV_W, IDX_HEADS * IDX_DIM, IDX_DIM])
                                          .tolist(), axis=-1)
    r, k, v, lora = jnp.split(rw, [RW_W, 2 * RW_W, 3 * RW_W], axis=-1)
    z = lambda n: jnp.zeros((depth, d, n), bf16)
    w_in_p = jnp.concatenate(
        [hg, aq, aqi, ak, av, aki, awi, z(OFF_LORA - OFF_AWI - IDX_HEADS), lora, z(LORA_PAD - RW_LORA), r, k, v],
        axis=-1)
    assert w_in_p.shape[-1] == IN_PAD
    mu_r, mu_l = rwkv_mu[:, :3 * RW_W], rwkv_mu[:, 3 * RW_W:]
    mu_pad = jnp.concatenate([mu_r, mu_l, jnp.zeros((depth, LORA_PAD - RW_LORA), f32)], axis=-1)[:, None, :]
    zl = lambda n: jnp.zeros((depth, n, RW_W), bf16)
    w2p = jnp.concatenate([rwkv_w2.astype(bf16), zl(LORA_PAD - RW_DECAY_LORA)], axis=1)
    a2p = jnp.concatenate([zl(RW_DECAY_LORA), rwkv_a2.astype(bf16),
                           zl(LORA_PAD - RW_DECAY_LORA - RW_AAA_LORA)], axis=1)
    g2p = jnp.concatenate([zl(RW_DECAY_LORA + RW_AAA_LORA), rwkv_g2.astype(bf16), zl(LORA_PAD - RW_LORA)], axis=1)
    return dict(w_in=w_in_p, mu_pad=mu_pad, w2p=w2p, a2p=a2p, g2p=g2p,
                w_out=w_out.astype(bf16), w_gate=w_gate.astype(bf16), w_up=w_up.astype(bf16),
                w_down=w_down.astype(bf16))


def _shift_row(p_row):
    return jnp.concatenate([p_row[..., OFF_R:OFF_R + 3 * RW_W], p_row[..., OFF_LORA:OFF_LORA + RW_LORA]], axis=-1)


def _k_tile(f):
    half = f // 2
    return half if f % 2 == 0 and half % LANES == 0 else f


def _layer(x, layer, wts, *, nb, tp, t_valid, hg_s0, rw_s0, shift0, attend):
    h = _rmsnorm(x, wts["ln1"], layer)
    p = _matmul(h, wts["w_in"], layer, tm=1024, tn=512, tk=h.shape[1])
    valid = None if t_valid == tp else t_valid
    rows = min(256, tp)
    o_hg, hg_s = _gla(p, wts["lbs"], wts["hgrn_norm"], hg_s0, layer, nb=nb, tp=tp, rows=rows, valid=valid)
    o_rw, rw_s = _rwkv(p, shift0, rw_s0, wts, layer, nb=nb, tp=tp, t_valid=t_valid, rows=rows)
    k, v, ki = _kv_prep(p, wts["k_norm"], layer)
    o_at = attend(p, k, v, ki)
    mix = jnp.concatenate([o_hg, o_rw, o_at], axis=-1)
    x = _matmul(mix, wts["w_out"], layer, tm=1024, tn=512, tk=mix.shape[1], res=x)
    h2 = _rmsnorm(x, wts["ln2"], layer)
    act = _swiglu(h2, wts["w_gate"], wts["w_up"], layer, tm=1024, tn=256)
    x = _matmul(act, wts["w_down"], layer, tm=1024, tn=512, tk=_k_tile(act.shape[1]), res=x)
    shift = _shift_row(p.reshape(nb, tp, IN_PAD)[:, t_valid - 1])
    return x, (k, v, ki, hg_s, rw_s, shift)


def kernel(x_prompt, x_sample, cache_k, cache_v, cache_kidx, state_hgrn, state_rwkv, state_shift, page_table,
           ln1, w_in, hgrn_lb, hgrn_norm, rwkv_mu, rwkv_w0, rwkv_w2, rwkv_a0, rwkv_a2, rwkv_g2, rwkv_kk,
           rwkv_ka, rwkv_rk, rwkv_lnx_w, rwkv_lnx_b, q_norm, k_norm, w_out, ln2, w_gate, w_up, w_down):
    depth = w_in.shape[0]
    B, T, D = x_prompt.shape
    DB, DS, _ = x_sample.shape
    n_pool, page_size = cache_k.shape[1], cache_k.shape[2]
    past = page_table.shape[1] * page_size

    wts = _prep_weights(w_in, rwkv_mu, rwkv_w2, rwkv_a2, rwkv_g2, w_out, w_gate, w_up, w_down)
    lb_p = jax.nn.softmax(hgrn_lb.astype(f32), axis=0)
    row = lambda a: a.astype(f32).reshape(depth, 1, -1)
    wts.update(lbs=(jnp.cumsum(lb_p, axis=0) - lb_p[0:1])[:, None, :], hgrn_norm=row(hgrn_norm),
               ln1=row(ln1), ln2=row(ln2), w0=row(rwkv_w0), a0=row(rwkv_a0), kkw=row(rwkv_kk),
               kaw=row(rwkv_ka), rkw=row(rwkv_rk), lnw=row(rwkv_lnx_w), lnb=row(rwkv_lnx_b),
               q_norm=row(q_norm), k_norm=row(k_norm))
    ck = cache_k.reshape(depth, n_pool, page_size, KV_W)
    cv = cache_v.reshape(depth, n_pool, page_size, KV_W)

    xp = x_prompt.reshape(B * T, D)
    outs_p = []
    zeros_hg = jnp.zeros((B, HG_HEADS, HG_DK, HG_DV), f32)
    zeros_rw = jnp.zeros((B, RW_HEADS, RW_HD, RW_HD), f32)
    zeros_sh = jnp.zeros((B, RW_COLS), f32)
    for l in range(depth):
        attend = lambda p, k, v, ki, l=l: _dsa_prompt(p, k, v, ki, wts["q_norm"], l, nb=B, t_len=T)
        xp, st = _layer(xp, l, wts, nb=B, tp=T, t_valid=T, hg_s0=zeros_hg, rw_s0=zeros_rw, shift0=zeros_sh,
                        attend=attend)
        outs_p.append(st)

    TP = SAMPLE_ROWS
    xs = jnp.pad(x_sample, ((0, 0), (0, TP - DS), (0, 0))).reshape(DB * TP, D)
    n_sel_s = min(TOPK_MAX, (past + DS) // 4)
    outs_s = []
    for l in range(depth):
        def attend(p, k, v, ki, l=l):
            scores = _dsa_sample_scores(p, ki, cache_kidx, page_table, l, nb=DB, valid=DS)
            keep = _dsa_sample_select(scores, n_sel_s)
            return _dsa_sample_attn(p, keep, ck, cv, k, v, wts["q_norm"], page_table, l, nb=DB)

        xs, st = _layer(xs, l, wts, nb=DB, tp=TP, t_valid=DS, hg_s0=state_hgrn[l], rw_s0=state_rwkv[l],
                        shift0=state_shift[l], attend=attend)
        outs_s.append(st)

    def stack(outs, i):
        return jnp.stack([o[i] for o in outs])

    k_p = stack(outs_p, 0).reshape(depth, B, T, AT_KV_HEADS, HEAD_DIM)
    v_p = stack(outs_p, 1).reshape(depth, B, T, AT_KV_HEADS, HEAD_DIM)
    ki_p = stack(outs_p, 2).reshape(depth, B, T, IDX_DIM)
    cut = lambda a, w: a.reshape(depth, DB, TP, *w)[:, :, :DS]
    k_s = cut(stack(outs_s, 0), (AT_KV_HEADS, HEAD_DIM))
    v_s = cut(stack(outs_s, 1), (AT_KV_HEADS, HEAD_DIM))
    ki_s = cut(stack(outs_s, 2), (IDX_DIM,))
    y_p = xp.reshape(B, T, D)
    y_s = xs.reshape(DB, TP, D)[:, :DS]
    return (y_p, y_s, k_p, v_p, ki_p, stack(outs_p, 3), stack(outs_p, 4), stack(outs_p, 5),
            k_s, v_s, ki_s, stack(outs_s, 3), stack(outs_s, 4), stack(outs_s, 5))
```

```python
import functools

import jax
import jax.numpy as jnp
import numpy as np
from jax import lax
from jax.experimental import pallas as pl
from jax.experimental.pallas import tpu as pltpu

f32 = jnp.float32
bf16 = jnp.bfloat16

HG_HEADS, HG_DK, HG_DV = 8, 128, 128
HG_W = HG_HEADS * HG_DV
EXP_CLIP = 60.0
RW_HEADS, RW_HD = 16, 64
RW_W = RW_HEADS * RW_HD
RW_DECAY_LORA, RW_AAA_LORA, RW_GATE_LORA = 64, 64, 160
RW_LORA = RW_DECAY_LORA + RW_AAA_LORA + RW_GATE_LORA
RW_GN_EPS = 64e-5
RW_COLS = 3 * RW_W + RW_LORA
AT_HEADS, AT_KV_HEADS, HEAD_DIM = 16, 4, 128
AT_GROUP = AT_HEADS // AT_KV_HEADS
AT_W = AT_HEADS * HEAD_DIM
KV_W = AT_KV_HEADS * HEAD_DIM
IDX_HEADS, IDX_DIM = 8, 128
TOPK_MAX = 256
Q_BLOCK = 128
NEG_BIG = -1e30
NORM_EPS = 1e-6
HG_COLS = 2 * HG_HEADS * HG_DK + 2 * HG_W
AT_COLS = AT_W + 2 * KV_W + IDX_HEADS * IDX_DIM + IDX_DIM + IDX_HEADS

LANES = 128
VMEM_LIMIT = 56 * 1024 * 1024
SAMPLE_ROWS = 16
GLA_CHUNK = 128
GLA_SUB = 16
DSA_TIERS = 4
SAMPLE_PAGES = 16

LORA_PAD = 512
OFF_HG = 0
OFF_AQ = HG_COLS
OFF_AQI = OFF_AQ + AT_W
OFF_AK = OFF_AQI + IDX_HEADS * IDX_DIM
OFF_AV = OFF_AK + KV_W
OFF_AKI = OFF_AV + KV_W
OFF_AWI = OFF_AKI + IDX_DIM
OFF_LORA = 8704
OFF_R = OFF_LORA + LORA_PAD
OFF_K = OFF_R + RW_W
OFF_V = OFF_K + RW_W
IN_PAD = OFF_V + RW_W


def _cparams(sem):
    return pltpu.CompilerParams(dimension_semantics=sem, vmem_limit_bytes=VMEM_LIMIT)


def _sigmoid(x):
    return jax.nn.sigmoid(x)


def _dot(a, b, precision=None):
    return jnp.dot(a, b, preferred_element_type=f32, precision=precision)


def _dot_nt(a, b, precision=None):
    return lax.dot_general(a, b, (((1,), (1,)), ((), ())), preferred_element_type=f32, precision=precision)


def _dot_tn(a, b, precision=None):
    return lax.dot_general(a, b, (((0,), (0,)), ((), ())), preferred_element_type=f32, precision=precision)


HI = lax.Precision.HIGHEST


def _rmsnorm_kernel(x_ref, g_ref, o_ref):
    x = x_ref[...]
    ms = jnp.mean(x * x, axis=-1, keepdims=True)
    o_ref[...] = (x * lax.rsqrt(ms + NORM_EPS) * g_ref[...]).astype(o_ref.dtype)


def _rmsnorm(x, g, layer):
    m, d = x.shape
    tr = min(256, m)
    return pl.pallas_call(
        _rmsnorm_kernel,
        grid=(m // tr,),
        in_specs=[pl.BlockSpec((tr, d), lambda i: (i, 0)),
                  pl.BlockSpec((None, 1, d), lambda i: (layer, 0, 0))],
        out_specs=pl.BlockSpec((tr, d), lambda i: (i, 0)),
        out_shape=jax.ShapeDtypeStruct((m, d), bf16),
        compiler_params=_cparams(("parallel",)),
    )(x, g)


def _mm_kernel(*refs, nk, has_res):
    if has_res:
        a_ref, b_ref, r_ref, o_ref = refs[:4]
    else:
        a_ref, b_ref, o_ref = refs[:3]
        r_ref = None
    part = _dot(a_ref[...], b_ref[...])
    if nk == 1:
        o_ref[...] = (part + r_ref[...]) if has_res else part
        return
    acc_ref = refs[-1]
    k = pl.program_id(2)

    @pl.when(k == 0)
    def _():
        acc_ref[...] = part

    @pl.when(k > 0)
    def _():
        acc_ref[...] += part

    @pl.when(k == nk - 1)
    def _():
        o_ref[...] = (acc_ref[...] + r_ref[...]) if has_res else acc_ref[...]


def _matmul(a, w, layer, *, tm, tn, tk, res=None):
    m, kdim = a.shape
    n = w.shape[-1]
    tm = min(tm, m)
    tn = min(tn, n)
    nk = kdim // tk
    assert m % tm == 0 and n % tn == 0 and kdim % tk == 0
    in_specs = [pl.BlockSpec((tm, tk), lambda i, j, k: (i, k)),
                pl.BlockSpec((None, tk, tn), lambda i, j, k: (layer, k, j))]
    args = [a, w]
    if res is not None:
        in_specs.append(pl.BlockSpec((tm, tn), lambda i, j, k: (i, j)))
        args.append(res)
    scratch = [pltpu.VMEM((tm, tn), f32)] if nk > 1 else []
    return pl.pallas_call(
        functools.partial(_mm_kernel, nk=nk, has_res=res is not None),
        grid=(m // tm, n // tn, nk),
        in_specs=in_specs,
        out_specs=pl.BlockSpec((tm, tn), lambda i, j, k: (i, j)),
        out_shape=jax.ShapeDtypeStruct((m, n), f32),
        scratch_shapes=scratch,
        compiler_params=_cparams(("parallel", "parallel", "arbitrary")),
    )(*args)


def _swiglu_kernel(a_ref, wg_ref, wu_ref, o_ref):
    a = a_ref[...]
    g = _dot(a, wg_ref[...])
    u = _dot(a, wu_ref[...])
    o_ref[...] = (g * _sigmoid(g) * u).astype(o_ref.dtype)


def _swiglu(a, wg, wu, layer, *, tm, tn):
    m, d = a.shape
    n = wg.shape[-1]
    tm = min(tm, m)
    assert m % tm == 0 and n % tn == 0
    wspec = pl.BlockSpec((None, d, tn), lambda i, j: (layer, 0, j))
    return pl.pallas_call(
        _swiglu_kernel,
        grid=(m // tm, n // tn),
        in_specs=[pl.BlockSpec((tm, d), lambda i, j: (i, 0)), wspec, wspec],
        out_specs=pl.BlockSpec((tm, tn), lambda i, j: (i, j)),
        out_shape=jax.ShapeDtypeStruct((m, n), bf16),
        compiler_params=_cparams(("parallel", "parallel")),
    )(a, wg, wu)


def _gla_kernel(pq_ref, pf_ref, pi_ref, pg_ref, lb_ref, g_ref, s0_ref, o_ref, sout_ref, st_ref,
                *, rows, valid):
    i = pl.program_id(2)
    chunk, sub = GLA_CHUNK, GLA_SUB
    live_rows = min(rows, chunk)

    @pl.when(i == 0)
    def _():
        st_ref[...] = s0_ref[...].T

    lb = lb_ref[...]
    gain = g_ref[...]
    n_live = live_rows if valid is None else min(valid, live_rows)
    nsb = -(-n_live // sub)
    crow = lax.broadcasted_iota(jnp.int32, (chunk, 1), 0)
    tri = (lax.broadcasted_iota(jnp.int32, (chunk, chunk), 0)
           >= lax.broadcasted_iota(jnp.int32, (chunk, chunk), 1)).astype(f32)
    sub_row = lax.broadcasted_iota(jnp.int32, (sub, 1), 0)

    def padded(x):
        if live_rows == chunk:
            return x
        return jnp.concatenate([x, jnp.zeros((chunk - live_rows, x.shape[1]), x.dtype)], axis=0)

    def one_chunk(c, carry):
        rs = pl.ds(pl.multiple_of(c * live_rows, live_rows), live_rows)
        pq = padded(pq_ref[rs, :])
        fr = padded(pf_ref[rs, :])
        v = padded(pi_ref[rs, :])
        q = pq * _sigmoid(pq) * (HG_DK ** -0.5)
        log_sig = jnp.minimum(fr, 0.0) - jnp.log1p(jnp.exp(-jnp.abs(fr)))
        log_f = log_sig + jnp.log1p(lb * jnp.exp(jnp.minimum(-fr, EXP_CLIP)))
        k = (1.0 - lb) * _sigmoid(-fr)
        if valid is not None or live_rows != chunk:
            live = crow < live_rows
            if valid is not None:
                live = live & (i * rows + c * live_rows + crow < valid)
            log_f = jnp.where(live, log_f, 0.0)
            k = jnp.where(live, k, 0.0)
        b = _dot(tri, log_f, precision=HI)
        st = st_ref[...]
        o = _dot_nt((q * jnp.exp(b)).astype(bf16), st.astype(bf16))
        v16 = v.astype(bf16)
        a_rows, d_rows = [], []
        for ib in range(nsb):
            lo = ib * sub
            q_i = q[lo:lo + sub]
            b_i = b[lo:lo + sub]
            if ib == 0:
                a_rows.append(jnp.zeros((sub, chunk), f32))
            else:
                ref_row = b[lo - 1:lo]
                qt = (q_i * jnp.exp(b_i - ref_row)).astype(bf16)
                kt = jnp.where(crow < lo, k * jnp.exp(jnp.minimum(ref_row - b, 0.0)), 0.0).astype(bf16)
                a_rows.append(_dot_nt(qt, kt))
            d_i = jnp.zeros((sub, HG_DV), f32)
            for s in range(sub):
                gs = lo + s
                term = q_i * jnp.exp(jnp.minimum(b_i - b[gs:gs + 1], 0.0)) * k[gs:gs + 1]
                a_col = jnp.sum(term, axis=1, keepdims=True)
                d_i = d_i + jnp.where(sub_row >= s, a_col, 0.0) * v[gs:gs + 1]
            d_rows.append(d_i)
        if nsb * sub < chunk:
            a_rows.append(jnp.zeros((chunk - nsb * sub, chunk), f32))
            d_rows.append(jnp.zeros((chunk - nsb * sub, HG_DV), f32))
        a_off = jnp.concatenate(a_rows, axis=0)
        o = o + _dot(a_off.astype(bf16), v16) + jnp.concatenate(d_rows, axis=0)
        b_last = b[chunk - 1:chunk]
        kd = (k * jnp.exp(b_last - b)).astype(bf16)
        st_ref[...] = st * jnp.exp(b_last) + _dot_tn(v16, kd)
        o = o[:live_rows]
        ms = jnp.mean(o * o, axis=-1, keepdims=True)
        y = o * lax.rsqrt(ms + NORM_EPS) * gain * _sigmoid(pg_ref[rs, :])
        o_ref[rs, :] = y.astype(o_ref.dtype)
        return carry

    lax.fori_loop(0, rows // live_rows, one_chunk, 0)

    @pl.when(i == pl.num_programs(2) - 1)
    def _():
        sout_ref[...] = st_ref[...].T


def _gla(p, lb, gain, s0, layer, *, nb, tp, rows, valid):
    nblk = tp // rows
    H = HG_HEADS

    def pspec(part):
        return pl.BlockSpec((rows, LANES), lambda b, h, i: (b * nblk + i, part * H + h))

    return pl.pallas_call(
        functools.partial(_gla_kernel, rows=rows, valid=valid),
        grid=(nb, H, nblk),
        in_specs=[pspec(0), pspec(1), pspec(2), pspec(3),
                  pl.BlockSpec((None, 1, LANES), lambda b, h, i: (layer, 0, h)),
                  pl.BlockSpec((None, 1, LANES), lambda b, h, i: (layer, 0, h)),
                  pl.BlockSpec((None, None, HG_DK, HG_DV), lambda b, h, i: (b, h, 0, 0))],
        out_specs=[pl.BlockSpec((rows, LANES), lambda b, h, i: (b * nblk + i, h)),
                   pl.BlockSpec((None, None, HG_DK, HG_DV), lambda b, h, i: (b, h, 0, 0))],
        out_shape=[jax.ShapeDtypeStruct((nb * tp, HG_W), bf16),
                   jax.ShapeDtypeStruct((nb, H, HG_DK, HG_DV), f32)],
        scratch_shapes=[pltpu.VMEM((HG_DV, HG_DK), f32)],
        compiler_params=_cparams(("parallel", "parallel", "arbitrary")),
    )(p, p, p, p, lb, gain, s0)


def _head_sum(x, hm):
    cols = [_dot(x[:, c * LANES:(c + 1) * LANES], hm, precision=HI) for c in range(RW_W // LANES)]
    return jnp.concatenate(cols, axis=1)


def _head_matrix():
    r = lax.broadcasted_iota(jnp.int32, (LANES, LANES), 0) // RW_HD
    c = lax.broadcasted_iota(jnp.int32, (LANES, LANES), 1) // RW_HD
    return (r == c).astype(f32)


def _rw_pre_kernel(pr_ref, pk_ref, pv_ref, pl_ref, sr_ref, sk_ref, sv_ref, sl_ref,
                   mr_ref, mk_ref, mv_ref, ml_ref, w0_ref, w2_ref, a0_ref, a2_ref, g2_ref,
                   kkw_ref, kaw_ref, rkw_ref,
                   r_o, w_o, k_o, v_o, nkk_o, b_o, bonus_o, g_o,
                   cr_ref, ck_ref, cv_ref, cl_ref, *, rows, valid):
    i = pl.program_id(1)

    @pl.when(i == 0)
    def _():
        cr_ref[...] = sr_ref[...]
        ck_ref[...] = sk_ref[...]
        cv_ref[...] = sv_ref[...]
        cl_ref[...] = sl_ref[...]

    def mixed(p_ref, c_ref, m_ref):
        cur = p_ref[...]
        rolled = pltpu.roll(cur, 1, axis=0)
        rowid = lax.broadcasted_iota(jnp.int32, cur.shape, 0)
        prev = jnp.where(rowid == 0, c_ref[...], rolled)
        c_ref[...] = cur[rows - 1:rows, :]
        return cur + (prev - cur) * m_ref[...]

    r = mixed(pr_ref, cr_ref, mr_ref)
    k = mixed(pk_ref, ck_ref, mk_ref)
    v = mixed(pv_ref, cv_ref, mv_ref)
    xl = mixed(pl_ref, cl_ref, ml_ref)

    zw = w0_ref[...] + _dot(jnp.tanh(xl).astype(bf16), w2_ref[...])
    w = jnp.minimum(zw, 0.0) - jnp.log1p(jnp.exp(-jnp.abs(zw))) - 0.5
    decay = jnp.exp(-jnp.exp(w))
    a = _sigmoid(a0_ref[...] + _dot(xl.astype(bf16), a2_ref[...]))
    g = _dot(_sigmoid(xl).astype(bf16), g2_ref[...])

    hm = _head_matrix()
    kk = k * kkw_ref[...]
    kk = kk * lax.rsqrt(jnp.maximum(_head_sum(kk * kk, hm), 1e-24))
    k2 = k * (1.0 + (a - 1.0) * kaw_ref[...])
    bonus = _head_sum(r * k2 * rkw_ref[...], hm) * v
    nkk = -kk
    bb = kk * a
    if valid is not None:
        tok = i * rows + lax.broadcasted_iota(jnp.int32, (rows, 1), 0)
        live = tok < valid
        decay = jnp.where(live, decay, 1.0)
        k2 = jnp.where(live, k2, 0.0)
        v = jnp.where(live, v, 0.0)
        nkk = jnp.where(live, nkk, 0.0)
        bb = jnp.where(live, bb, 0.0)
    r_o[...] = r
    w_o[...] = decay
    k_o[...] = k2
    v_o[...] = v
    nkk_o[...] = nkk
    b_o[...] = bb
    bonus_o[...] = bonus
    g_o[...] = g


def _rw_pre(p, shift_pad, mu_pad, w0, w2p, a0, a2p, g2p, kkw, kaw, rkw, layer, *, nb, tp, rows, valid):
    nblk = tp // rows
    W = RW_W

    def pspec(off, width):
        return pl.BlockSpec((rows, width), lambda b, i: (b * nblk + i, off // width))

    def sspec(off, width):
        return pl.BlockSpec((None, 1, width), lambda b, i: (b, 0, off // width))

    def mspec(off, width):
        return pl.BlockSpec((None, 1, width), lambda b, i: (layer, 0, off // width))

    def vec():
        return pl.BlockSpec((None, 1, W), lambda b, i: (layer, 0, 0))

    def lora():
        return pl.BlockSpec((None, LORA_PAD, W), lambda b, i: (layer, 0, 0))

    out_spec = pl.BlockSpec((rows, W), lambda b, i: (b * nblk + i, 0))
    out_shape = jax.ShapeDtypeStruct((nb * tp, W), f32)
    return pl.pallas_call(
        functools.partial(_rw_pre_kernel, rows=rows, valid=valid),
        grid=(nb, nblk),
        in_specs=[pspec(OFF_R, W), pspec(OFF_K, W), pspec(OFF_V, W), pspec(OFF_LORA, LORA_PAD),
                  sspec(0, W), sspec(W, W), sspec(2 * W, W), sspec(3 * W, LORA_PAD),
                  mspec(0, W), mspec(W, W), mspec(2 * W, W), mspec(3 * W, LORA_PAD),
                  vec(), lora(), vec(), lora(), lora(), vec(), vec(), vec()],
        out_specs=[out_spec] * 8,
        out_shape=[out_shape] * 8,
        scratch_shapes=[pltpu.VMEM((1, W), f32), pltpu.VMEM((1, W), f32), pltpu.VMEM((1, W), f32),
                        pltpu.VMEM((1, LORA_PAD), f32)],
        compiler_params=_cparams(("parallel", "arbitrary")),
    )(p, p, p, p, shift_pad, shift_pad, shift_pad, shift_pad, mu_pad, mu_pad, mu_pad, mu_pad,
      w0, w2p, a0, a2p, g2p, kkw, kaw, rkw)


def _rw_scan_kernel(r_ref, w_ref, k_ref, b_ref, nkk_ref, v_ref, s0_ref, y_ref, sout_ref, s_ref, ybuf_ref,
                    *, tt, ki_n, fold, nbh):
    i = pl.program_id(0)

    @pl.when(i == 0)
    def _():
        s_ref[...] = s0_ref[...]

    def lane_total(x):
        return x + pltpu.roll(x, LANES // 2, axis=1) if fold else x

    half = RW_HD // 2

    def token(t, carry):
        for lo in (0, half):
            vr = slice(lo, lo + half)
            acc = [jnp.zeros((half, LANES), f32), jnp.zeros((half, LANES), f32)]
            for ki in range(ki_n):
                acc[ki % 2] = acc[ki % 2] + s_ref[ki, vr, :] * nkk_ref[t, pl.ds(ki, 1), :]
            sa = lane_total(acc[0] + acc[1])
            vt = v_ref[t, vr, :]
            y = [jnp.zeros((half, LANES), f32), jnp.zeros((half, LANES), f32)]
            for ki in range(ki_n):
                s_new = (s_ref[ki, vr, :] * w_ref[t, pl.ds(ki, 1), :] + sa * b_ref[t, pl.ds(ki, 1), :]
                         + vt * k_ref[t, pl.ds(ki, 1), :])
                s_ref[ki, vr, :] = s_new
                y[ki % 2] = y[ki % 2] + s_new * r_ref[t, pl.ds(ki, 1), :]
            ybuf_ref[t, vr, :] = lane_total(y[0] + y[1])
        return carry

    lax.fori_loop(0, tt, token, 0)

    def flip(j, carry):
        pair = jnp.concatenate([ybuf_ref[2 * j], ybuf_ref[2 * j + 1]], axis=0).T
        y_ref[2 * j] = pair[:nbh, :RW_HD]
        y_ref[2 * j + 1] = pair[:nbh, RW_HD:]
        return carry

    lax.fori_loop(0, tt // 2, flip, 0)

    @pl.when(i == pl.num_programs(0) - 1)
    def _():
        sout_ref[...] = s_ref[...]


def _rw_scan(rT, wT, kT, bT, nkkT, vT, s0T, *, tt, nbh):
    t_len, ki_n, _ = rT.shape
    fold = ki_n * 2 == RW_HD
    assert fold or ki_n == RW_HD
    tt = min(tt, t_len)
    assert tt % 2 == 0 and t_len % tt == 0
    op = pl.BlockSpec((tt, ki_n, LANES), lambda i: (i, 0, 0))
    vs = pl.BlockSpec((tt, RW_HD, LANES), lambda i: (i, 0, 0))
    ss = pl.BlockSpec((ki_n, RW_HD, LANES), lambda i: (0, 0, 0))
    return pl.pallas_call(
        functools.partial(_rw_scan_kernel, tt=tt, ki_n=ki_n, fold=fold, nbh=nbh),
        grid=(t_len // tt,),
        in_specs=[op, op, op, op, op, vs, ss],
        out_specs=[pl.BlockSpec((tt, nbh, RW_HD), lambda i: (i, 0, 0)), ss],
        out_shape=[jax.ShapeDtypeStruct((t_len, nbh, RW_HD), f32),
                   jax.ShapeDtypeStruct((ki_n, RW_HD, LANES), f32)],
        scratch_shapes=[pltpu.VMEM((ki_n, RW_HD, LANES), f32), pltpu.VMEM((tt, RW_HD, LANES), f32)],
        compiler_params=_cparams(("arbitrary",)),
    )(rT, wT, kT, bT, nkkT, vT, s0T)


def _rw_post_kernel(y_ref, bonus_ref, g_ref, lw_ref, lb_ref, o_ref):
    hm = _head_matrix()
    y = y_ref[...]
    mean = _head_sum(y, hm) * (1.0 / RW_HD)
    d = y - mean
    var = _head_sum(d * d, hm) * (1.0 / RW_HD)
    yn = d * lax.rsqrt(var + RW_GN_EPS) * lw_ref[...] + lb_ref[...]
    o_ref[...] = ((yn + bonus_ref[...]) * g_ref[...]).astype(o_ref.dtype)


def _rw_post(y, bonus, g, lnw, lnb, layer):
    m = y.shape[0]
    tr = min(512, m)
    spec = pl.BlockSpec((tr, RW_W), lambda i: (i, 0))
    vec = pl.BlockSpec((None, 1, RW_W), lambda i: (layer, 0, 0))
    return pl.pallas_call(
        _rw_post_kernel,
        grid=(m // tr,),
        in_specs=[spec, spec, spec, vec, vec],
        out_specs=spec,
        out_shape=jax.ShapeDtypeStruct((m, RW_W), bf16),
        compiler_params=_cparams(("parallel",)),
    )(y, bonus, g, lnw, lnb)


def _rwkv(p, shift0, s0, wts, layer, *, nb, tp, t_valid, rows):
    H, N = RW_HEADS, RW_HD
    kh = LANES // (nb * H)
    assert kh in (1, 2) and kh * nb * H == LANES
    ki_n = N // kh
    shift_pad = jnp.pad(shift0, ((0, 0), (0, 3 * RW_W + LORA_PAD - RW_COLS)))[:, None, :]
    valid = None if t_valid == tp else t_valid
    r, w, k, v, nkk, bb, bonus, g = _rw_pre(
        p, shift_pad, wts["mu_pad"], wts["w0"], wts["w2p"], wts["a0"], wts["a2p"], wts["g2p"],
        wts["kkw"], wts["kaw"], wts["rkw"], layer, nb=nb, tp=tp, rows=rows, valid=valid)

    def key_lanes(x):
        x = x.reshape(nb, tp, H, kh, ki_n)[:, :t_valid]
        return x.transpose(1, 4, 3, 0, 2).reshape(t_valid, ki_n, LANES)

    vT = v.reshape(nb, tp, H, N)[:, :t_valid].transpose(1, 3, 0, 2).reshape(t_valid, N, 1, nb * H)
    vT = jnp.broadcast_to(vT, (t_valid, N, kh, nb * H)).reshape(t_valid, N, LANES)
    s0T = s0.reshape(nb, H, N, kh, ki_n).transpose(4, 2, 3, 0, 1).reshape(ki_n, N, LANES)
    yT, sT = _rw_scan(key_lanes(r), key_lanes(w), key_lanes(k), key_lanes(bb), key_lanes(nkk), vT, s0T,
                      tt=64, nbh=nb * H)
    y = yT.reshape(t_valid, nb, RW_W).transpose(1, 0, 2)
    if t_valid != tp:
        y = jnp.pad(y, ((0, 0), (0, tp - t_valid), (0, 0)))
    y = y.reshape(nb * tp, RW_W)
    s_out = sT.reshape(ki_n, N, kh, nb, H).transpose(3, 4, 1, 2, 0).reshape(nb, H, N, N)
    o = _rw_post(y, bonus, g, wts["lnw"], wts["lnb"], layer)
    return o, s_out


def _kv_prep_kernel(ak_ref, av_ref, aki_ref, kn_ref, k_o, v_o, ki_o):
    gain = kn_ref[...]
    for n in range(AT_KV_HEADS):
        x = ak_ref[:, n * HEAD_DIM:(n + 1) * HEAD_DIM]
        ms = jnp.mean(x * x, axis=-1, keepdims=True)
        k_o[:, n * HEAD_DIM:(n + 1) * HEAD_DIM] = x * lax.rsqrt(ms + NORM_EPS) * gain
    v_o[...] = av_ref[...]
    ki_o[...] = aki_ref[...]


def _kv_prep(p, k_norm, layer):
    m = p.shape[0]
    tr = min(512, m)
    return pl.pallas_call(
        _kv_prep_kernel,
        grid=(m // tr,),
        in_specs=[pl.BlockSpec((tr, KV_W), lambda i: (i, OFF_AK // KV_W)),
                  pl.BlockSpec((tr, KV_W), lambda i: (i, OFF_AV // KV_W)),
                  pl.BlockSpec((tr, IDX_DIM), lambda i: (i, OFF_AKI // IDX_DIM)),
                  pl.BlockSpec((None, 1, HEAD_DIM), lambda i: (layer, 0, 0))],
        out_specs=[pl.BlockSpec((tr, KV_W), lambda i: (i, 0)),
                   pl.BlockSpec((tr, KV_W), lambda i: (i, 0)),
                   pl.BlockSpec((tr, IDX_DIM), lambda i: (i, 0))],
        out_shape=[jax.ShapeDtypeStruct((m, KV_W), f32), jax.ShapeDtypeStruct((m, KV_W), f32),
                   jax.ShapeDtypeStruct((m, IDX_DIM), f32)],
        compiler_params=_cparams(("parallel",)),
    )(p, p, p, k_norm)


def _index_scores(qi, wi_col, keys):
    rws = qi.shape[0]
    qs = jnp.concatenate([qi[:, h * IDX_DIM:(h + 1) * IDX_DIM] for h in range(IDX_HEADS)], axis=0)
    d = jnp.maximum(_dot_nt(qs, keys, precision=HI), 0.0) * wi_col
    s = d[0:rws]
    for h in range(1, IDX_HEADS):
        s = s + d[h * rws:(h + 1) * rws]
    return s


def _wi_column(awi):
    scale = IDX_HEADS ** -0.5 * IDX_DIM ** -0.5
    return jnp.concatenate([awi[:, h:h + 1] for h in range(IDX_HEADS)], axis=0) * scale


def _sortable(score):
    bits = lax.bitcast_convert_type(score, jnp.int32)
    key = jnp.where(bits < 0, bits ^ jnp.int32(0x7FFFFFFF), bits)
    return jnp.where(score == 0.0, 0, key)


def _count(mask):
    return jnp.sum(mask.astype(f32), axis=1, keepdims=True)


def _select_topk(skey, n_sel, n_keys):
    rws = skey.shape[0]
    nsel = jnp.float32(n_sel)
    int_min = jnp.int32(-2 ** 31)
    zero = jnp.zeros((rws, 1), jnp.int32)
    cand = jnp.where(_count(skey >= zero) >= nsel, zero, zero + int_min)

    def bit_step(it, cand):
        trial = cand + jnp.left_shift(jnp.int32(1), 30 - it)
        return jnp.where(_count(skey >= trial) >= nsel, trial, cand)

    tau = lax.fori_loop(0, 31, bit_step, cand)
    gt = skey > tau
    eq = skey == tau
    need = nsel - _count(gt)
    idx = lax.broadcasted_iota(jnp.int32, skey.shape, 1)
    nbits = int(n_keys).bit_length()

    def idx_step(it, x):
        trial = x + jnp.left_shift(jnp.int32(1), nbits - 1 - it)
        ok = (trial <= n_keys) & (_count(eq & (idx < trial)) < need)
        return jnp.where(ok, trial, x)

    surplus = jnp.max(_count(eq) - need) > 0.0
    x = lax.cond(surplus, lambda: lax.fori_loop(0, nbits, idx_step, zero), lambda: zero + n_keys)
    return gt | (eq & (idx <= x))


def _q_heads(aq, gain, n):
    outs = []
    for g in range(AT_GROUP):
        h = n * AT_GROUP + g
        x = aq[:, h * HEAD_DIM:(h + 1) * HEAD_DIM]
        ms = jnp.mean(x * x, axis=-1, keepdims=True)
        outs.append(x * lax.rsqrt(ms + NORM_EPS) * gain)
    return jnp.concatenate(outs, axis=0)


def _dsa_prompt_kernel(aq_ref, aqi_ref, awi_ref, k_ref, v_ref, ki_ref, qn_ref, o_ref, sel_ref,
                       *, t_len, n_sel):
    i = pl.program_id(1)
    qb = Q_BLOCK
    nkb = t_len // qb
    qi = aqi_ref[...]
    wi_col = _wi_column(awi_ref[...])
    q_pos = i * qb + lax.broadcasted_iota(jnp.int32, (qb, 1), 0)

    for j in range(nkb):
        cols = slice(j * qb, (j + 1) * qb)

        @pl.when(j <= i)
        def _():
            s = _index_scores(qi, wi_col, ki_ref[cols, :])
            key_pos = j * qb + lax.broadcasted_iota(jnp.int32, (qb, qb), 1)
            s = jnp.where(key_pos <= q_pos, s, NEG_BIG)
            sel_ref[:, cols] = _sortable(s)

        @pl.when(j > i)
        def _():
            sel_ref[:, cols] = _sortable(jnp.full((qb, qb), NEG_BIG, f32))

    def attend(width):
        chosen = _select_topk(sel_ref[:, :width], n_sel, width)
        key_pos = lax.broadcasted_iota(jnp.int32, (qb, width), 1)
        keep = jnp.where(chosen & (key_pos <= q_pos), 1.0, 0.0)
        keep4 = jnp.concatenate([keep] * AT_GROUP, axis=0) > 0.5
        gain = qn_ref[...]
        aq = aq_ref[...]
        for n in range(AT_KV_HEADS):
            kn = k_ref[:width, n * HEAD_DIM:(n + 1) * HEAD_DIM].astype(bf16)
            vn = v_ref[:width, n * HEAD_DIM:(n + 1) * HEAD_DIM].astype(bf16)
            q4 = _q_heads(aq, gain, n).astype(bf16)
            s = _dot_nt(q4, kn) * (HEAD_DIM ** -0.5)
            s = jnp.where(keep4, s, NEG_BIG)
            m = jnp.max(s, axis=1, keepdims=True)
            e = jnp.exp(s - m)
            l = jnp.sum(e, axis=1, keepdims=True)
            o = _dot(e.astype(bf16), vn) / l
            for g in range(AT_GROUP):
                h = n * AT_GROUP + g
                o_ref[:, h * HEAD_DIM:(h + 1) * HEAD_DIM] = o[g * qb:(g + 1) * qb].astype(o_ref.dtype)

    n_tier = DSA_TIERS if nkb % DSA_TIERS == 0 else 1
    per_tier = nkb // n_tier
    for tier in range(n_tier):
        pl.when(i // per_tier == tier)(functools.partial(attend, (tier + 1) * per_tier * qb))


def _dsa_prompt(p, k, v, ki, q_norm, layer, *, nb, t_len):
    n_sel = min(TOPK_MAX, t_len // 4)
    nq = t_len // Q_BLOCK
    return pl.pallas_call(
        functools.partial(_dsa_prompt_kernel, t_len=t_len, n_sel=n_sel),
        grid=(nb, nq),
        in_specs=[pl.BlockSpec((Q_BLOCK, AT_W), lambda b, i: (b * nq + i, OFF_AQ // AT_W)),
                  pl.BlockSpec((Q_BLOCK, IDX_HEADS * IDX_DIM),
                               lambda b, i: (b * nq + i, OFF_AQI // (IDX_HEADS * IDX_DIM))),
                  pl.BlockSpec((Q_BLOCK, LANES), lambda b, i: (b * nq + i, OFF_AWI // LANES)),
                  pl.BlockSpec((t_len, KV_W), lambda b, i: (b, 0)),
                  pl.BlockSpec((t_len, KV_W), lambda b, i: (b, 0)),
                  pl.BlockSpec((t_len, IDX_DIM), lambda b, i: (b, 0)),
                  pl.BlockSpec((None, 1, HEAD_DIM), lambda b, i: (layer, 0, 0))],
        out_specs=pl.BlockSpec((Q_BLOCK, AT_W), lambda b, i: (b * nq + i, 0)),
        out_shape=jax.ShapeDtypeStruct((nb * t_len, AT_W), bf16),
        scratch_shapes=[pltpu.VMEM((Q_BLOCK, t_len), jnp.int32)],
        compiler_params=_cparams(("parallel", "arbitrary")),
    )(p, p, p, k, v, ki, q_norm)


def _dsa_sample_score_kernel(pt_ref, aqi_ref, awi_ref, *refs, n_steps, npg, valid):
    page_refs, knew_ref, o_ref = refs[:npg], refs[npg], refs[npg + 1]
    g = pl.program_id(1)
    rws = SAMPLE_ROWS
    ps = page_refs[0].shape[0]
    qi = aqi_ref[...]
    wi_col = _wi_column(awi_ref[...])

    @pl.when(g < n_steps)
    def _():
        keys = jnp.concatenate([r[...] for r in page_refs], axis=0)
        o_ref[...] = _index_scores(qi, wi_col, keys)

    @pl.when(g == n_steps)
    def _():
        keys = jnp.concatenate([knew_ref[...], jnp.zeros((ps - rws, IDX_DIM), f32)], axis=0)
        s = _index_scores(qi, wi_col, keys)
        key_i = lax.broadcasted_iota(jnp.int32, (rws, ps), 1)
        q_i = lax.broadcasted_iota(jnp.int32, (rws, ps), 0)
        o_ref[:, :ps] = jnp.where((key_i <= q_i) & (key_i < valid), s, NEG_BIG)
        if npg > 1:
            o_ref[:, ps:] = jnp.full((rws, (npg - 1) * ps), NEG_BIG, f32)


def _page_specs(n_pages, npg, ps, width, layer):
    def spec(j):
        return pl.BlockSpec((None, None, ps, width),
                            lambda b, g, pt: (layer, pt[b, jnp.minimum(g * npg + j, n_pages - 1)], 0, 0))
    return [spec(j) for j in range(npg)]


def _dsa_sample_scores(p, ki_new, cache_kidx, page_table, layer, *, nb, valid):
    n_pages = page_table.shape[1]
    ps = cache_kidx.shape[2]
    rws = SAMPLE_ROWS
    npg = min(SAMPLE_PAGES, n_pages)
    assert n_pages % npg == 0
    n_steps = n_pages // npg
    grid_spec = pltpu.PrefetchScalarGridSpec(
        num_scalar_prefetch=1,
        grid=(nb, n_steps + 1),
        in_specs=[pl.BlockSpec((rws, IDX_HEADS * IDX_DIM),
                               lambda b, g, pt: (b, OFF_AQI // (IDX_HEADS * IDX_DIM))),
                  pl.BlockSpec((rws, LANES), lambda b, g, pt: (b, OFF_AWI // LANES))]
        + _page_specs(n_pages, npg, ps, IDX_DIM, layer)
        + [pl.BlockSpec((rws, IDX_DIM), lambda b, g, pt: (b, 0))],
        out_specs=pl.BlockSpec((None, rws, npg * ps), lambda b, g, pt: (b, 0, g)),
    )
    return pl.pallas_call(
        functools.partial(_dsa_sample_score_kernel, n_steps=n_steps, npg=npg, valid=valid),
        grid_spec=grid_spec,
        out_shape=jax.ShapeDtypeStruct((nb, rws, (n_steps + 1) * npg * ps), f32),
        compiler_params=_cparams(("parallel", "arbitrary")),
    )(page_table, p, p, *([cache_kidx] * npg), ki_new)


def _dsa_sample_select_kernel(s_ref, o_ref, *, n_sel, n_keys):
    chosen = _select_topk(_sortable(s_ref[...]), n_sel, n_keys)
    o_ref[...] = jnp.where(chosen & (s_ref[...] > 0.5 * NEG_BIG), 1.0, 0.0)


def _dsa_sample_select(scores, n_sel):
    nb, rws, n_keys = scores.shape
    spec = pl.BlockSpec((None, rws, n_keys), lambda b: (b, 0, 0))
    return pl.pallas_call(
        functools.partial(_dsa_sample_select_kernel, n_sel=n_sel, n_keys=n_keys),
        grid=(nb,),
        in_specs=[spec],
        out_specs=spec,
        out_shape=jax.ShapeDtypeStruct(scores.shape, f32),
        compiler_params=_cparams(("parallel",)),
    )(scores)


def _dsa_sample_attn_kernel(pt_ref, aq_ref, keep_ref, *refs, n_steps, npg):
    kpage_refs, vpage_refs = refs[:npg], refs[npg:2 * npg]
    knew_ref, vnew_ref, qn_ref, o_ref, q_ref, m_ref, l_ref, acc_ref = refs[2 * npg:]
    g = pl.program_id(1)
    rws = SAMPLE_ROWS
    ps = kpage_refs[0].shape[0]

    @pl.when(g == 0)
    def _():
        m_ref[...] = jnp.full(m_ref.shape, NEG_BIG, f32)
        l_ref[...] = jnp.zeros(l_ref.shape, f32)
        acc_ref[...] = jnp.zeros(acc_ref.shape, f32)
        for n in range(AT_KV_HEADS):
            q_ref[n] = _q_heads(aq_ref[...], qn_ref[...], n).astype(bf16)

    def step(k_of, v_of, keep):
        keep4 = jnp.concatenate([keep] * AT_GROUP, axis=0) > 0.5
        for n in range(AT_KV_HEADS):
            s = _dot_nt(q_ref[n], k_of(n)) * (HEAD_DIM ** -0.5)
            s = jnp.where(keep4, s, NEG_BIG)
            m_old = m_ref[n]
            m_new = jnp.maximum(m_old, jnp.max(s, axis=1, keepdims=True))
            alpha = jnp.exp(m_old - m_new)
            e = jnp.where(keep4, jnp.exp(s - m_new), 0.0)
            l_ref[n] = alpha * l_ref[n] + jnp.sum(e, axis=1, keepdims=True)
            acc_ref[n] = alpha * acc_ref[n] + _dot(e.astype(bf16), v_of(n))
            m_ref[n] = m_new

    def head_cols(blocks, n):
        cols = [blk[:, n * HEAD_DIM:(n + 1) * HEAD_DIM].astype(bf16) for blk in blocks]
        return jnp.concatenate(cols, axis=0) if len(cols) > 1 else cols[0]

    @pl.when(g < n_steps)
    def _():
        step(functools.partial(head_cols, kpage_refs), functools.partial(head_cols, vpage_refs), keep_ref[...])

    @pl.when(g == n_steps)
    def _():
        pad = jnp.zeros((ps - rws, KV_W), f32)
        knew = jnp.concatenate([knew_ref[...], pad], axis=0)
        vnew = jnp.concatenate([vnew_ref[...], pad], axis=0)
        step(functools.partial(head_cols, [knew]), functools.partial(head_cols, [vnew]), keep_ref[:, :ps])
        for n in range(AT_KV_HEADS):
            o = acc_ref[n] / l_ref[n]
            for gq in range(AT_GROUP):
                h = n * AT_GROUP + gq
                o_ref[:, h * HEAD_DIM:(h + 1) * HEAD_DIM] = o[gq * rws:(gq + 1) * rws].astype(o_ref.dtype)


def _dsa_sample_attn(p, keep, cache_k, cache_v, k_new, v_new, q_norm, page_table, layer, *, nb):
    n_pages = page_table.shape[1]
    ps = cache_k.shape[2]
    rws = SAMPLE_ROWS
    npg = min(SAMPLE_PAGES, n_pages)
    n_steps = n_pages // npg
    new = pl.BlockSpec((rws, KV_W), lambda b, g, pt: (b, 0))
    grid_spec = pltpu.PrefetchScalarGridSpec(
        num_scalar_prefetch=1,
        grid=(nb, n_steps + 1),
        in_specs=[pl.BlockSpec((rws, AT_W), lambda b, g, pt: (b, OFF_AQ // AT_W)),
                  pl.BlockSpec((None, rws, npg * ps), lambda b, g, pt: (b, 0, g))]
        + _page_specs(n_pages, npg, ps, KV_W, layer) + _page_specs(n_pages, npg, ps, KV_W, layer)
        + [new, new, pl.BlockSpec((None, 1, HEAD_DIM), lambda b, g, pt: (layer, 0, 0))],
        out_specs=pl.BlockSpec((rws, AT_W), lambda b, g, pt: (b, 0)),
        scratch_shapes=[pltpu.VMEM((AT_KV_HEADS, AT_GROUP * rws, HEAD_DIM), bf16),
                        pltpu.VMEM((AT_KV_HEADS, AT_GROUP * rws, 1), f32),
                        pltpu.VMEM((AT_KV_HEADS, AT_GROUP * rws, 1), f32),
                        pltpu.VMEM((AT_KV_HEADS, AT_GROUP * rws, HEAD_DIM), f32)],
    )
    return pl.pallas_call(
        functools.partial(_dsa_sample_attn_kernel, n_steps=n_steps, npg=npg),
        grid_spec=grid_spec,
        out_shape=jax.ShapeDtypeStruct((nb * rws, AT_W), bf16),
        compiler_params=_cparams(("parallel", "arbitrary")),
    )(page_table, p, keep, *([cache_k] * npg), *([cache_v] * npg), k_new, v_new, q_norm)


def _prep_weights(w_in, rwkv_mu, rwkv_w2, rwkv_a2, rwkv_g2, w_out, w_gate, w_up, w_down):
    depth, d, _ = w_in.shape
    hg, rw, at = jnp.split(w_in.astype(bf16), [HG_COLS, HG_COLS + RW_COLS], axis=-1)
    aq, ak, av, aqi, aki, awi = jnp.split(at, np.cumsum([AT_W, KV_W, KV_W, IDX_HEADS * IDX_DIM, IDX_DIM])
                                          .tolist(), axis=-1)
    r, k, v, lora = jnp.split(rw, [RW_W, 2 * RW_W, 3 * RW_W], axis=-1)
    z = lambda n: jnp.zeros((depth, d, n), bf16)
    w_in_p = jnp.concatenate(
        [hg, aq, aqi, ak, av, aki, awi, z(OFF_LORA - OFF_AWI - IDX_HEADS), lora, z(LORA_PAD - RW_LORA), r, k, v],
        axis=-1)
    assert w_in_p.shape[-1] == IN_PAD
    mu_r, mu_l = rwkv_mu[:, :3 * RW_W], rwkv_mu[:, 3 * RW_W:]
    mu_pad = jnp.concatenate([mu_r, mu_l, jnp.zeros((depth, LORA_PAD - RW_LORA), f32)], axis=-1)[:, None, :]
    zl = lambda n: jnp.zeros((depth, n, RW_W), bf16)
    w2p = jnp.concatenate([rwkv_w2.astype(bf16), zl(LORA_PAD - RW_DECAY_LORA)], axis=1)
    a2p = jnp.concatenate([zl(RW_DECAY_LORA), rwkv_a2.astype(bf16),
                           zl(LORA_PAD - RW_DECAY_LORA - RW_AAA_LORA)], axis=1)
    g2p = jnp.concatenate([zl(RW_DECAY_LORA + RW_AAA_LORA), rwkv_g2.astype(bf16), zl(LORA_PAD - RW_LORA)], axis=1)
    return dict(w_in=w_in_p, mu_pad=mu_pad, w2p=w2p, a2p=a2p, g2p=g2p,
                w_out=w_out.astype(bf16), w_gate=w_gate.astype(bf16), w_up=w_up.astype(bf16),
                w_down=w_down.astype(bf16))


def _shift_row(p_row):
    return jnp.concatenate([p_row[..., OFF_R:OFF_R + 3 * RW_W], p_row[..., OFF_LORA:OFF_LORA + RW_LORA]], axis=-1)


def _k_tile(f):
    half = f // 2
    return half if f % 2 == 0 and half % LANES == 0 else f


def _layer(x, layer, wts, *, nb, tp, t_valid, hg_s0, rw_s0, shift0, attend):
    h = _rmsnorm(x, wts["ln1"], layer)
    p = _matmul(h, wts["w_in"], layer, tm=1024, tn=512, tk=h.shape[1])
    valid = None if t_valid == tp else t_valid
    rows = min(256, tp)
    o_hg, hg_s = _gla(p, wts["lbs"], wts["hgrn_norm"], hg_s0, layer, nb=nb, tp=tp, rows=rows, valid=valid)
    o_rw, rw_s = _rwkv(p, shift0, rw_s0, wts, layer, nb=nb, tp=tp, t_valid=t_valid, rows=rows)
    k, v, ki = _kv_prep(p, wts["k_norm"], layer)
    o_at = attend(p, k, v, ki)
    mix = jnp.concatenate([o_hg, o_rw, o_at], axis=-1)
    x = _matmul(mix, wts["w_out"], layer, tm=1024, tn=512, tk=mix.shape[1], res=x)
    h2 = _rmsnorm(x, wts["ln2"], layer)
    act = _swiglu(h2, wts["w_gate"], wts["w_up"], layer, tm=1024, tn=256)
    x = _matmul(act, wts["w_down"], layer, tm=1024, tn=512, tk=_k_tile(act.shape[1]), res=x)
    shift = _shift_row(p.reshape(nb, tp, IN_PAD)[:, t_valid - 1])
    return x, (k, v, ki, hg_s, rw_s, shift)


def kernel(x_prompt, x_sample, cache_k, cache_v, cache_kidx, state_hgrn, state_rwkv, state_shift, page_table,
           ln1, w_in, hgrn_lb, hgrn_norm, rwkv_mu, rwkv_w0, rwkv_w2, rwkv_a0, rwkv_a2, rwkv_g2, rwkv_kk,
           rwkv_ka, rwkv_rk, rwkv_lnx_w, rwkv_lnx_b, q_norm, k_norm, w_out, ln2, w_gate, w_up, w_down):
    depth = w_in.shape[0]
    B, T, D = x_prompt.shape
    DB, DS, _ = x_sample.shape
    n_pool, page_size = cache_k.shape[1], cache_k.shape[2]
    past = page_table.shape[1] * page_size

    wts = _prep_weights(w_in, rwkv_mu, rwkv_w2, rwkv_a2, rwkv_g2, w_out, w_gate, w_up, w_down)
    lb_p = jax.nn.softmax(hgrn_lb.astype(f32), axis=0)
    row = lambda a: a.astype(f32).reshape(depth, 1, -1)
    wts.update(lbs=(jnp.cumsum(lb_p, axis=0) - lb_p[0:1])[:, None, :], hgrn_norm=row(hgrn_norm),
               ln1=row(ln1), ln2=row(ln2), w0=row(rwkv_w0), a0=row(rwkv_a0), kkw=row(rwkv_kk),
               kaw=row(rwkv_ka), rkw=row(rwkv_rk), lnw=row(rwkv_lnx_w), lnb=row(rwkv_lnx_b),
               q_norm=row(q_norm), k_norm=row(k_norm))
    ck = cache_k.reshape(depth, n_pool, page_size, KV_W)
    cv = cache_v.reshape(depth, n_pool, page_size, KV_W)

    xp = x_prompt.reshape(B * T, D)
    outs_p = []
    zeros_hg = jnp.zeros((B, HG_HEADS, HG_DK, HG_DV), f32)
    zeros_rw = jnp.zeros((B, RW_HEADS, RW_HD, RW_HD), f32)
    zeros_sh = jnp.zeros((B, RW_COLS), f32)
    for l in range(depth):
        attend = lambda p, k, v, ki, l=l: _dsa_prompt(p, k, v, ki, wts["q_norm"], l, nb=B, t_len=T)
        xp, st = _layer(xp, l, wts, nb=B, tp=T, t_valid=T, hg_s0=zeros_hg, rw_s0=zeros_rw, shift0=zeros_sh,
                        attend=attend)
        outs_p.append(st)

    TP = SAMPLE_ROWS
    xs = jnp.pad(x_sample, ((0, 0), (0, TP - DS), (0, 0))).reshape(DB * TP, D)
    n_sel_s = min(TOPK_MAX, (past + DS) // 4)
    outs_s = []
    for l in range(depth):
        def attend(p, k, v, ki, l=l):
            scores = _dsa_sample_scores(p, ki, cache_kidx, page_table, l, nb=DB, valid=DS)
            keep = _dsa_sample_select(scores, n_sel_s)
            return _dsa_sample_attn(p, keep, ck, cv, k, v, wts["q_norm"], page_table, l, nb=DB)

        xs, st = _layer(xs, l, wts, nb=DB, tp=TP, t_valid=DS, hg_s0=state_hgrn[l], rw_s0=state_rwkv[l],
                        shift0=state_shift[l], attend=attend)
        outs_s.append(st)

    def stack(outs, i):
        return jnp.stack([o[i] for o in outs])

    k_p = stack(outs_p, 0).reshape(depth, B, T, AT_KV_HEADS, HEAD_DIM)
    v_p = stack(outs_p, 1).reshape(depth, B, T, AT_KV_HEADS, HEAD_DIM)
    ki_p = stack(outs_p, 2).reshape(depth, B, T, IDX_DIM)
    cut = lambda a, w: a.reshape(depth, DB, TP, *w)[:, :, :DS]
    k_s = cut(stack(outs_s, 0), (AT_KV_HEADS, HEAD_DIM))
    v_s = cut(stack(outs_s, 1), (AT_KV_HEADS, HEAD_DIM))
    ki_s = cut(stack(outs_s, 2), (IDX_DIM,))
    y_p = xp.reshape(B, T, D)
    y_s = xs.reshape(DB, TP, D)[:, :DS]
    return (y_p, y_s, k_p, v_p, ki_p, stack(outs_p, 3), stack(outs_p, 4), stack(outs_p, 5),
            k_s, v_s, ki_s, stack(outs_s, 3), stack(outs_s, 4), stack(outs_s, 5))
```

```python
import functools

import jax
import jax.numpy as jnp
import numpy as np
from jax import lax
from jax.experimental import pallas as pl
from jax.experimental.pallas import tpu as pltpu

f32 = jnp.float32
bf16 = jnp.bfloat16

HG_HEADS, HG_DK, HG_DV = 8, 128, 128
HG_W = HG_HEADS * HG_DV
EXP_CLIP = 60.0
RW_HEADS, RW_HD = 16, 64
RW_W = RW_HEADS * RW_HD
RW_DECAY_LORA, RW_AAA_LORA, RW_GATE_LORA = 64, 64, 160
RW_LORA = RW_DECAY_LORA + RW_AAA_LORA + RW_GATE_LORA
RW_GN_EPS = 64e-5
RW_COLS = 3 * RW_W + RW_LORA
AT_HEADS, AT_KV_HEADS, HEAD_DIM = 16, 4, 128
AT_GROUP = AT_HEADS // AT_KV_HEADS
AT_W = AT_HEADS * HEAD_DIM
KV_W = AT_KV_HEADS * HEAD_DIM
IDX_HEADS, IDX_DIM = 8, 128
TOPK_MAX = 256
Q_BLOCK = 128
NEG_BIG = -1e30
NORM_EPS = 1e-6
HG_COLS = 2 * HG_HEADS * HG_DK + 2 * HG_W
AT_COLS = AT_W + 2 * KV_W + IDX_HEADS * IDX_DIM + IDX_DIM + IDX_HEADS

LANES = 128
VMEM_LIMIT = 56 * 1024 * 1024
SAMPLE_ROWS = 16
GLA_CHUNK = 128
GLA_SUB = 16
RW_UNROLL = 32
DSA_TIERS = 4
DSA_KEY_CHUNK = 256
SAMPLE_PAGES = 16

LORA_PAD = 512
OFF_HG = 0
OFF_AQ = HG_COLS
OFF_AQI = OFF_AQ + AT_W
OFF_AK = OFF_AQI + IDX_HEADS * IDX_DIM
OFF_AV = OFF_AK + KV_W
OFF_AKI = OFF_AV + KV_W
OFF_AWI = OFF_AKI + IDX_DIM
OFF_LORA = 8704
OFF_R = OFF_LORA + LORA_PAD
OFF_K = OFF_R + RW_W
OFF_V = OFF_K + RW_W
IN_PAD = OFF_V + RW_W


def _cparams(sem):
    return pltpu.CompilerParams(dimension_semantics=sem, vmem_limit_bytes=VMEM_LIMIT)


def _sigmoid(x):
    return jax.nn.sigmoid(x)


def _dot(a, b, precision=None):
    return jnp.dot(a, b, preferred_element_type=f32, precision=precision)


def _dot_nt(a, b, precision=None):
    return lax.dot_general(a, b, (((1,), (1,)), ((), ())), preferred_element_type=f32, precision=precision)


def _dot_tn(a, b, precision=None):
    return lax.dot_general(a, b, (((0,), (0,)), ((), ())), preferred_element_type=f32, precision=precision)


HI = lax.Precision.HIGHEST


def _rmsnorm_kernel(x_ref, g_ref, o_ref):
    x = x_ref[...]
    ms = jnp.mean(x * x, axis=-1, keepdims=True)
    o_ref[...] = (x * lax.rsqrt(ms + NORM_EPS) * g_ref[...]).astype(o_ref.dtype)


def _rmsnorm(x, g, layer):
    m, d = x.shape
    tr = min(256, m)
    return pl.pallas_call(
        _rmsnorm_kernel,
        grid=(m // tr,),
        in_specs=[pl.BlockSpec((tr, d), lambda i: (i, 0)),
                  pl.BlockSpec((None, 1, d), lambda i: (layer, 0, 0))],
        out_specs=pl.BlockSpec((tr, d), lambda i: (i, 0)),
        out_shape=jax.ShapeDtypeStruct((m, d), bf16),
        compiler_params=_cparams(("parallel",)),
    )(x, g)


def _mm_kernel(*refs, nk, has_res):
    if has_res:
        a_ref, b_ref, r_ref, o_ref = refs[:4]
    else:
        a_ref, b_ref, o_ref = refs[:3]
        r_ref = None
    part = _dot(a_ref[...], b_ref[...])
    if nk == 1:
        o_ref[...] = (part + r_ref[...]) if has_res else part
        return
    acc_ref = refs[-1]
    k = pl.program_id(2)

    @pl.when(k == 0)
    def _():
        acc_ref[...] = part

    @pl.when(k > 0)
    def _():
        acc_ref[...] += part

    @pl.when(k == nk - 1)
    def _():
        o_ref[...] = (acc_ref[...] + r_ref[...]) if has_res else acc_ref[...]


def _matmul(a, w, layer, *, tm, tn, tk, res=None):
    m, kdim = a.shape
    n = w.shape[-1]
    tm = min(tm, m)
    tn = min(tn, n)
    nk = kdim // tk
    assert m % tm == 0 and n % tn == 0 and kdim % tk == 0
    in_specs = [pl.BlockSpec((tm, tk), lambda i, j, k: (i, k)),
                pl.BlockSpec((None, tk, tn), lambda i, j, k: (layer, k, j))]
    args = [a, w]
    if res is not None:
        in_specs.append(pl.BlockSpec((tm, tn), lambda i, j, k: (i, j)))
        args.append(res)
    scratch = [pltpu.VMEM((tm, tn), f32)] if nk > 1 else []
    return pl.pallas_call(
        functools.partial(_mm_kernel, nk=nk, has_res=res is not None),
        grid=(m // tm, n // tn, nk),
        in_specs=in_specs,
        out_specs=pl.BlockSpec((tm, tn), lambda i, j, k: (i, j)),
        out_shape=jax.ShapeDtypeStruct((m, n), f32),
        scratch_shapes=scratch,
        compiler_params=_cparams(("parallel", "parallel", "arbitrary")),
    )(*args)


def _swiglu_kernel(a_ref, wg_ref, wu_ref, o_ref):
    a = a_ref[...]
    g = _dot(a, wg_ref[...])
    u = _dot(a, wu_ref[...])
    o_ref[...] = (g * _sigmoid(g) * u).astype(o_ref.dtype)


def _swiglu(a, wg, wu, layer, *, tm, tn):
    m, d = a.shape
    n = wg.shape[-1]
    tm = min(tm, m)
    assert m % tm == 0 and n % tn == 0
    wspec = pl.BlockSpec((None, d, tn), lambda i, j: (layer, 0, j))
    return pl.pallas_call(
        _swiglu_kernel,
        grid=(m // tm, n // tn),
        in_specs=[pl.BlockSpec((tm, d), lambda i, j: (i, 0)), wspec, wspec],
        out_specs=pl.BlockSpec((tm, tn), lambda i, j: (i, j)),
        out_shape=jax.ShapeDtypeStruct((m, n), bf16),
        compiler_params=_cparams(("parallel", "parallel")),
    )(a, wg, wu)


def _gla_kernel(pq_ref, pf_ref, pi_ref, pg_ref, lb_ref, g_ref, s0_ref, o_ref, sout_ref, st_ref,
                *, rows, valid):
    i = pl.program_id(2)
    chunk, sub = GLA_CHUNK, GLA_SUB
    live_rows = min(rows, chunk)

    @pl.when(i == 0)
    def _():
        st_ref[...] = s0_ref[...].T

    lb = lb_ref[...]
    gain = g_ref[...]
    n_live = live_rows if valid is None else min(valid, live_rows)
    nsb = -(-n_live // sub)
    crow = lax.broadcasted_iota(jnp.int32, (chunk, 1), 0)
    tri = (lax.broadcasted_iota(jnp.int32, (chunk, chunk), 0)
           >= lax.broadcasted_iota(jnp.int32, (chunk, chunk), 1)).astype(f32)
    sub_row = lax.broadcasted_iota(jnp.int32, (sub, 1), 0)

    def padded(x):
        if live_rows == chunk:
            return x
        return jnp.concatenate([x, jnp.zeros((chunk - live_rows, x.shape[1]), x.dtype)], axis=0)

    def one_chunk(c, carry):
        rs = pl.ds(pl.multiple_of(c * live_rows, live_rows), live_rows)
        pq = padded(pq_ref[rs, :])
        fr = padded(pf_ref[rs, :])
        v = padded(pi_ref[rs, :])
        q = pq * _sigmoid(pq) * (HG_DK ** -0.5)
        log_sig = jnp.minimum(fr, 0.0) - jnp.log1p(jnp.exp(-jnp.abs(fr)))
        log_f = log_sig + jnp.log1p(lb * jnp.exp(jnp.minimum(-fr, EXP_CLIP)))
        k = (1.0 - lb) * _sigmoid(-fr)
        if valid is not None or live_rows != chunk:
            live = crow < live_rows
            if valid is not None:
                live = live & (i * rows + c * live_rows + crow < valid)
            log_f = jnp.where(live, log_f, 0.0)
            k = jnp.where(live, k, 0.0)
        b = _dot(tri, log_f, precision=HI)
        st = st_ref[...]
        o = _dot_nt((q * jnp.exp(b)).astype(bf16), st.astype(bf16))
        v16 = v.astype(bf16)
        a_rows, d_rows = [], []
        for ib in range(nsb):
            lo = ib * sub
            q_i = q[lo:lo + sub]
            b_i = b[lo:lo + sub]
            if ib == 0:
                a_rows.append(jnp.zeros((sub, chunk), f32))
            else:
                ref_row = b[lo - 1:lo]
                qt = (q_i * jnp.exp(b_i - ref_row)).astype(bf16)
                kt = jnp.where(crow < lo, k * jnp.exp(jnp.minimum(ref_row - b, 0.0)), 0.0).astype(bf16)
                a_rows.append(_dot_nt(qt, kt))
            d_i = jnp.zeros((sub, HG_DV), f32)
            for s in range(sub):
                gs = lo + s
                term = q_i * jnp.exp(jnp.minimum(b_i - b[gs:gs + 1], 0.0)) * k[gs:gs + 1]
                a_col = jnp.sum(term, axis=1, keepdims=True)
                d_i = d_i + jnp.where(sub_row >= s, a_col, 0.0) * v[gs:gs + 1]
            d_rows.append(d_i)
        if nsb * sub < chunk:
            a_rows.append(jnp.zeros((chunk - nsb * sub, chunk), f32))
            d_rows.append(jnp.zeros((chunk - nsb * sub, HG_DV), f32))
        a_off = jnp.concatenate(a_rows, axis=0)
        o = o + _dot(a_off.astype(bf16), v16) + jnp.concatenate(d_rows, axis=0)
        b_last = b[chunk - 1:chunk]
        kd = (k * jnp.exp(b_last - b)).astype(bf16)
        st_ref[...] = st * jnp.exp(b_last) + _dot_tn(v16, kd)
        o = o[:live_rows]
        ms = jnp.mean(o * o, axis=-1, keepdims=True)
        y = o * lax.rsqrt(ms + NORM_EPS) * gain * _sigmoid(pg_ref[rs, :])
        o_ref[rs, :] = y.astype(o_ref.dtype)
        return carry

    lax.fori_loop(0, rows // live_rows, one_chunk, 0)

    @pl.when(i == pl.num_programs(2) - 1)
    def _():
        sout_ref[...] = st_ref[...].T


def _gla(p, lb, gain, s0, layer, *, nb, tp, rows, valid):
    nblk = tp // rows
    H = HG_HEADS

    def pspec(part):
        return pl.BlockSpec((rows, LANES), lambda b, h, i: (b * nblk + i, part * H + h))

    return pl.pallas_call(
        functools.partial(_gla_kernel, rows=rows, valid=valid),
        grid=(nb, H, nblk),
        in_specs=[pspec(0), pspec(1), pspec(2), pspec(3),
                  pl.BlockSpec((None, 1, LANES), lambda b, h, i: (layer, 0, h)),
                  pl.BlockSpec((None, 1, LANES), lambda b, h, i: (layer, 0, h)),
                  pl.BlockSpec((None, None, HG_DK, HG_DV), lambda b, h, i: (b, h, 0, 0))],
        out_specs=[pl.BlockSpec((rows, LANES), lambda b, h, i: (b * nblk + i, h)),
                   pl.BlockSpec((None, None, HG_DK, HG_DV), lambda b, h, i: (b, h, 0, 0))],
        out_shape=[jax.ShapeDtypeStruct((nb * tp, HG_W), bf16),
                   jax.ShapeDtypeStruct((nb, H, HG_DK, HG_DV), f32)],
        scratch_shapes=[pltpu.VMEM((HG_DV, HG_DK), f32)],
        compiler_params=_cparams(("parallel", "parallel", "arbitrary")),
    )(p, p, p, p, lb, gain, s0)


def _head_sum(x, hm):
    cols = [_dot(x[:, c * LANES:(c + 1) * LANES], hm, precision=HI) for c in range(RW_W // LANES)]
    return jnp.concatenate(cols, axis=1)


def _head_matrix():
    r = lax.broadcasted_iota(jnp.int32, (LANES, LANES), 0) // RW_HD
    c = lax.broadcasted_iota(jnp.int32, (LANES, LANES), 1) // RW_HD
    return (r == c).astype(f32)


def _rw_pre_kernel(pr_ref, pk_ref, pv_ref, pl_ref, sr_ref, sk_ref, sv_ref, sl_ref,
                   mr_ref, mk_ref, mv_ref, ml_ref, w0_ref, w2_ref, a0_ref, a2_ref, g2_ref,
                   kkw_ref, kaw_ref, rkw_ref,
                   r_o, w_o, k_o, v_o, nkk_o, b_o, bonus_o, g_o,
                   cr_ref, ck_ref, cv_ref, cl_ref, *, rows, valid):
    i = pl.program_id(1)

    @pl.when(i == 0)
    def _():
        cr_ref[...] = sr_ref[...]
        ck_ref[...] = sk_ref[...]
        cv_ref[...] = sv_ref[...]
        cl_ref[...] = sl_ref[...]

    def mixed(p_ref, c_ref, m_ref):
        cur = p_ref[...]
        rolled = pltpu.roll(cur, 1, axis=0)
        rowid = lax.broadcasted_iota(jnp.int32, cur.shape, 0)
        prev = jnp.where(rowid == 0, c_ref[...], rolled)
        c_ref[...] = cur[rows - 1:rows, :]
        return cur + (prev - cur) * m_ref[...]

    r = mixed(pr_ref, cr_ref, mr_ref)
    k = mixed(pk_ref, ck_ref, mk_ref)
    v = mixed(pv_ref, cv_ref, mv_ref)
    xl = mixed(pl_ref, cl_ref, ml_ref)

    zw = w0_ref[...] + _dot(jnp.tanh(xl).astype(bf16), w2_ref[...])
    w = jnp.minimum(zw, 0.0) - jnp.log1p(jnp.exp(-jnp.abs(zw))) - 0.5
    decay = jnp.exp(-jnp.exp(w))
    a = _sigmoid(a0_ref[...] + _dot(xl.astype(bf16), a2_ref[...]))
    g = _dot(_sigmoid(xl).astype(bf16), g2_ref[...])

    hm = _head_matrix()
    kk = k * kkw_ref[...]
    kk = kk * lax.rsqrt(jnp.maximum(_head_sum(kk * kk, hm), 1e-24))
    k2 = k * (1.0 + (a - 1.0) * kaw_ref[...])
    bonus = _head_sum(r * k2 * rkw_ref[...], hm) * v
    nkk = -kk
    bb = kk * a
    if valid is not None:
        tok = i * rows + lax.broadcasted_iota(jnp.int32, (rows, 1), 0)
        live = tok < valid
        decay = jnp.where(live, decay, 1.0)
        k2 = jnp.where(live, k2, 0.0)
        v = jnp.where(live, v, 0.0)
        nkk = jnp.where(live, nkk, 0.0)
        bb = jnp.where(live, bb, 0.0)
    r_o[...] = r
    w_o[...] = decay
    k_o[...] = k2
    v_o[...] = v
    nkk_o[...] = nkk
    b_o[...] = bb
    bonus_o[...] = bonus
    g_o[...] = g


def _rw_pre(p, shift_pad, mu_pad, w0, w2p, a0, a2p, g2p, kkw, kaw, rkw, layer, *, nb, tp, rows, valid):
    nblk = tp // rows
    W = RW_W

    def pspec(off, width):
        return pl.BlockSpec((rows, width), lambda b, i: (b * nblk + i, off // width))

    def sspec(off, width):
        return pl.BlockSpec((None, 1, width), lambda b, i: (b, 0, off // width))

    def mspec(off, width):
        return pl.BlockSpec((None, 1, width), lambda b, i: (layer, 0, off // width))

    def vec():
        return pl.BlockSpec((None, 1, W), lambda b, i: (layer, 0, 0))

    def lora():
        return pl.BlockSpec((None, LORA_PAD, W), lambda b, i: (layer, 0, 0))

    out_spec = pl.BlockSpec((rows, W), lambda b, i: (b * nblk + i, 0))
    out_shape = jax.ShapeDtypeStruct((nb * tp, W), f32)
    return pl.pallas_call(
        functools.partial(_rw_pre_kernel, rows=rows, valid=valid),
        grid=(nb, nblk),
        in_specs=[pspec(OFF_R, W), pspec(OFF_K, W), pspec(OFF_V, W), pspec(OFF_LORA, LORA_PAD),
                  sspec(0, W), sspec(W, W), sspec(2 * W, W), sspec(3 * W, LORA_PAD),
                  mspec(0, W), mspec(W, W), mspec(2 * W, W), mspec(3 * W, LORA_PAD),
                  vec(), lora(), vec(), lora(), lora(), vec(), vec(), vec()],
        out_specs=[out_spec] * 8,
        out_shape=[out_shape] * 8,
        scratch_shapes=[pltpu.VMEM((1, W), f32), pltpu.VMEM((1, W), f32), pltpu.VMEM((1, W), f32),
                        pltpu.VMEM((1, LORA_PAD), f32)],
        compiler_params=_cparams(("parallel", "arbitrary")),
    )(p, p, p, p, shift_pad, shift_pad, shift_pad, shift_pad, mu_pad, mu_pad, mu_pad, mu_pad,
      w0, w2p, a0, a2p, g2p, kkw, kaw, rkw)


def _rw_scan_kernel(r_ref, w_ref, k_ref, b_ref, nkk_ref, v_ref, s0_ref, y_ref, sout_ref, s_ref,
                    *, tt, ki_n, fold):
    i = pl.program_id(0)

    @pl.when(i == 0)
    def _():
        s_ref[...] = s0_ref[...]

    def lane_total(x):
        return x + pltpu.roll(x, LANES // 2, axis=1) if fold else x

    half = RW_HD // 2
    halves = (slice(0, half), slice(half, RW_HD))
    zero = jnp.zeros((half, LANES), f32)

    def row(ref, t, ki):
        return ref[t, pl.ds(ki, 1), :]

    def first_sa(vr):
        acc = [zero, zero]
        for ki in range(ki_n):
            acc[ki % 2] = acc[ki % 2] + s_ref[ki, vr, :] * row(nkk_ref, 0, ki)
        return lane_total(acc[0] + acc[1])

    def token(t, sa_pair):
        nxt = jnp.minimum(t + 1, tt - 1)
        sa_next = []
        for vr, sa in zip(halves, sa_pair):
            vt = v_ref[t, vr, :]

            def k_block(kb, carry):
                y0, y1, a0, a1 = carry
                for u in range(RW_UNROLL):
                    ki = kb * RW_UNROLL + u
                    s_new = (s_ref[ki, vr, :] * row(w_ref, t, ki) + sa * row(b_ref, t, ki)
                             + vt * row(k_ref, t, ki))
                    s_ref[ki, vr, :] = s_new
                    if u % 2 == 0:
                        y0 = y0 + s_new * row(r_ref, t, ki)
                        a0 = a0 + s_new * row(nkk_ref, nxt, ki)
                    else:
                        y1 = y1 + s_new * row(r_ref, t, ki)
                        a1 = a1 + s_new * row(nkk_ref, nxt, ki)
                return y0, y1, a0, a1

            n_blocks = ki_n // RW_UNROLL
            if n_blocks == 1:
                y0, y1, a0, a1 = k_block(0, (zero, zero, zero, zero))
            else:
                y0, y1, a0, a1 = lax.fori_loop(0, n_blocks, k_block, (zero, zero, zero, zero))
            y_ref[t, vr, :] = lane_total(y0 + y1)
            sa_next.append(lane_total(a0 + a1))
        return tuple(sa_next)

    lax.fori_loop(0, tt, token, tuple(first_sa(vr) for vr in halves))

    @pl.when(i == pl.num_programs(0) - 1)
    def _():
        sout_ref[...] = s_ref[...]


def _rw_scan(rT, wT, kT, bT, nkkT, vT, s0T, *, tt):
    t_len, ki_n, _ = rT.shape
    fold = ki_n * 2 == RW_HD
    assert fold or ki_n == RW_HD
    tt = min(tt, t_len)
    assert t_len % tt == 0 and ki_n % RW_UNROLL == 0
    op = pl.BlockSpec((tt, ki_n, LANES), lambda i: (i, 0, 0))
    vs = pl.BlockSpec((tt, RW_HD, LANES), lambda i: (i, 0, 0))
    ss = pl.BlockSpec((ki_n, RW_HD, LANES), lambda i: (0, 0, 0))
    return pl.pallas_call(
        functools.partial(_rw_scan_kernel, tt=tt, ki_n=ki_n, fold=fold),
        grid=(t_len // tt,),
        in_specs=[op, op, op, op, op, vs, ss],
        out_specs=[vs, ss],
        out_shape=[jax.ShapeDtypeStruct((t_len, RW_HD, LANES), f32),
                   jax.ShapeDtypeStruct((ki_n, RW_HD, LANES), f32)],
        scratch_shapes=[pltpu.VMEM((ki_n, RW_HD, LANES), f32)],
        compiler_params=_cparams(("arbitrary",)),
    )(rT, wT, kT, bT, nkkT, vT, s0T)


def _rw_post_kernel(y_ref, bonus_ref, g_ref, lw_ref, lb_ref, o_ref):
    hm = _head_matrix()
    y = y_ref[...]
    mean = _head_sum(y, hm) * (1.0 / RW_HD)
    d = y - mean
    var = _head_sum(d * d, hm) * (1.0 / RW_HD)
    yn = d * lax.rsqrt(var + RW_GN_EPS) * lw_ref[...] + lb_ref[...]
    o_ref[...] = ((yn + bonus_ref[...]) * g_ref[...]).astype(o_ref.dtype)


def _rw_post(y, bonus, g, lnw, lnb, layer):
    m = y.shape[0]
    tr = min(512, m)
    spec = pl.BlockSpec((tr, RW_W), lambda i: (i, 0))
    vec = pl.BlockSpec((None, 1, RW_W), lambda i: (layer, 0, 0))
    return pl.pallas_call(
        _rw_post_kernel,
        grid=(m // tr,),
        in_specs=[spec, spec, spec, vec, vec],
        out_specs=spec,
        out_shape=jax.ShapeDtypeStruct((m, RW_W), bf16),
        compiler_params=_cparams(("parallel",)),
    )(y, bonus, g, lnw, lnb)


def _rwkv(p, shift0, s0, wts, layer, *, nb, tp, t_valid, rows):
    H, N = RW_HEADS, RW_HD
    kh = LANES // (nb * H)
    assert kh in (1, 2) and kh * nb * H == LANES
    ki_n = N // kh
    shift_pad = jnp.pad(shift0, ((0, 0), (0, 3 * RW_W + LORA_PAD - RW_COLS)))[:, None, :]
    valid = None if t_valid == tp else t_valid
    r, w, k, v, nkk, bb, bonus, g = _rw_pre(
        p, shift_pad, wts["mu_pad"], wts["w0"], wts["w2p"], wts["a0"], wts["a2p"], wts["g2p"],
        wts["kkw"], wts["kaw"], wts["rkw"], layer, nb=nb, tp=tp, rows=rows, valid=valid)

    def key_lanes(x):
        x = x.reshape(nb, tp, H, kh, ki_n)[:, :t_valid]
        return x.transpose(1, 4, 3, 0, 2).reshape(t_valid, ki_n, LANES)

    vT = jnp.broadcast_to(v.reshape(nb, tp, H, 1, N)[:, :t_valid], (nb, t_valid, H, kh, N))
    vT = vT.transpose(1, 4, 3, 0, 2).reshape(t_valid, N, LANES)
    s0T = s0.reshape(nb, H, N, kh, ki_n).transpose(4, 2, 3, 0, 1).reshape(ki_n, N, LANES)
    yT, sT = _rw_scan(key_lanes(r), key_lanes(w), key_lanes(k), key_lanes(bb), key_lanes(nkk), vT, s0T,
                      tt=64)
    y = yT[:, :, :nb * H].reshape(t_valid, N, nb, H).transpose(2, 0, 3, 1).reshape(nb, t_valid, RW_W)
    if t_valid != tp:
        y = jnp.pad(y, ((0, 0), (0, tp - t_valid), (0, 0)))
    y = y.reshape(nb * tp, RW_W)
    s_out = sT.reshape(ki_n, N, kh, nb, H).transpose(3, 4, 1, 2, 0).reshape(nb, H, N, N)
    o = _rw_post(y, bonus, g, wts["lnw"], wts["lnb"], layer)
    return o, s_out


def _kv_prep_kernel(ak_ref, av_ref, aki_ref, kn_ref, k_o, v_o, ki_o):
    gain = kn_ref[...]
    for n in range(AT_KV_HEADS):
        x = ak_ref[:, n * HEAD_DIM:(n + 1) * HEAD_DIM]
        ms = jnp.mean(x * x, axis=-1, keepdims=True)
        k_o[:, n * HEAD_DIM:(n + 1) * HEAD_DIM] = x * lax.rsqrt(ms + NORM_EPS) * gain
    v_o[...] = av_ref[...]
    ki_o[...] = aki_ref[...]


def _kv_prep(p, k_norm, layer):
    m = p.shape[0]
    tr = min(512, m)
    return pl.pallas_call(
        _kv_prep_kernel,
        grid=(m // tr,),
        in_specs=[pl.BlockSpec((tr, KV_W), lambda i: (i, OFF_AK // KV_W)),
                  pl.BlockSpec((tr, KV_W), lambda i: (i, OFF_AV // KV_W)),
                  pl.BlockSpec((tr, IDX_DIM), lambda i: (i, OFF_AKI // IDX_DIM)),
                  pl.BlockSpec((None, 1, HEAD_DIM), lambda i: (layer, 0, 0))],
        out_specs=[pl.BlockSpec((tr, KV_W), lambda i: (i, 0)),
                   pl.BlockSpec((tr, KV_W), lambda i: (i, 0)),
                   pl.BlockSpec((tr, IDX_DIM), lambda i: (i, 0))],
        out_shape=[jax.ShapeDtypeStruct((m, KV_W), f32), jax.ShapeDtypeStruct((m, KV_W), f32),
                   jax.ShapeDtypeStruct((m, IDX_DIM), f32)],
        compiler_params=_cparams(("parallel",)),
    )(p, p, p, k_norm)


def _index_scores(qi, wi_col, keys):
    rws = qi.shape[0]
    qs = jnp.concatenate([qi[:, h * IDX_DIM:(h + 1) * IDX_DIM] for h in range(IDX_HEADS)], axis=0)
    d = jnp.maximum(_dot_nt(qs, keys, precision=HI), 0.0) * wi_col
    s = d[0:rws]
    for h in range(1, IDX_HEADS):
        s = s + d[h * rws:(h + 1) * rws]
    return s


def _wi_column(awi):
    scale = IDX_HEADS ** -0.5 * IDX_DIM ** -0.5
    return jnp.concatenate([awi[:, h:h + 1] for h in range(IDX_HEADS)], axis=0) * scale


def _sortable(score):
    bits = lax.bitcast_convert_type(score, jnp.int32)
    key = jnp.where(bits < 0, bits ^ jnp.int32(0x7FFFFFFF), bits)
    return jnp.where(score == 0.0, 0, key)


def _count(mask):
    return jnp.sum(mask.astype(f32), axis=1, keepdims=True)


def _select_topk(skey, n_sel, n_keys):
    rws = skey.shape[0]
    nsel = jnp.float32(n_sel)
    int_min = jnp.int32(-2 ** 31)
    zero = jnp.zeros((rws, 1), jnp.int32)
    cand = jnp.where(_count(skey >= zero) >= nsel, zero, zero + int_min)

    def bit_step(it, cand):
        trial = cand + jnp.left_shift(jnp.int32(1), 30 - it)
        return jnp.where(_count(skey >= trial) >= nsel, trial, cand)

    tau = lax.fori_loop(0, 31, bit_step, cand)
    gt = skey > tau
    eq = skey == tau
    need = nsel - _count(gt)
    idx = lax.broadcasted_iota(jnp.int32, skey.shape, 1)
    nbits = int(n_keys).bit_length()

    def idx_step(it, x):
        trial = x + jnp.left_shift(jnp.int32(1), nbits - 1 - it)
        ok = (trial <= n_keys) & (_count(eq & (idx < trial)) < need)
        return jnp.where(ok, trial, x)

    surplus = jnp.max(_count(eq) - need) > 0.0
    x = lax.cond(surplus, lambda: lax.fori_loop(0, nbits, idx_step, zero), lambda: zero + n_keys)
    return gt | (eq & (idx <= x))


def _q_heads(aq, gain, n):
    outs = []
    for g in range(AT_GROUP):
        h = n * AT_GROUP + g
        x = aq[:, h * HEAD_DIM:(h + 1) * HEAD_DIM]
        ms = jnp.mean(x * x, axis=-1, keepdims=True)
        outs.append(x * lax.rsqrt(ms + NORM_EPS) * gain)
    return jnp.concatenate(outs, axis=0)


def _dsa_prompt_kernel(aq_ref, aqi_ref, awi_ref, k_ref, v_ref, ki_ref, qn_ref, o_ref,
                       sel_ref, keep_ref, q_ref, m_ref, l_ref, acc_ref, *, t_len, n_sel):
    i = pl.program_id(1)
    qb = Q_BLOCK
    nkb = t_len // qb
    qi = aqi_ref[...]
    wi_col = _wi_column(awi_ref[...])
    q_pos = i * qb + lax.broadcasted_iota(jnp.int32, (qb, 1), 0)

    for j in range(nkb):
        cols = slice(j * qb, (j + 1) * qb)

        @pl.when(j <= i)
        def _():
            s = _index_scores(qi, wi_col, ki_ref[cols, :])
            key_pos = j * qb + lax.broadcasted_iota(jnp.int32, (qb, qb), 1)
            s = jnp.where(key_pos <= q_pos, s, NEG_BIG)
            sel_ref[:, cols] = _sortable(s)

        @pl.when(j > i)
        def _():
            sel_ref[:, cols] = _sortable(jnp.full((qb, qb), NEG_BIG, f32))

    kc = DSA_KEY_CHUNK if t_len % DSA_KEY_CHUNK == 0 else qb

    def select(width):
        chosen = _select_topk(sel_ref[:, :width], n_sel, width)
        key_pos = lax.broadcasted_iota(jnp.int32, (qb, width), 1)
        keep = jnp.where(chosen & (key_pos <= q_pos), 1.0, 0.0)
        for c in range(width // kc):
            tiles = [keep[:, a:a + qb].T for a in range(c * kc, (c + 1) * kc, qb)]
            keep_ref[c] = jnp.concatenate(tiles, axis=0) if len(tiles) > 1 else tiles[0]

    n_tier = DSA_TIERS if nkb % DSA_TIERS == 0 and (nkb // DSA_TIERS * qb) % kc == 0 else 1
    per_tier = nkb // n_tier
    for tier in range(n_tier):
        pl.when(i // per_tier == tier)(functools.partial(select, (tier + 1) * per_tier * qb))

    gain = qn_ref[...]
    for h in range(AT_HEADS):
        x = aq_ref[:, h * HEAD_DIM:(h + 1) * HEAD_DIM]
        ms = jnp.mean(x * x, axis=-1, keepdims=True)
        q_ref[h] = (x * lax.rsqrt(ms + NORM_EPS) * gain).T.astype(bf16)
    m_ref[...] = jnp.full(m_ref.shape, NEG_BIG, f32)
    l_ref[...] = jnp.zeros(l_ref.shape, f32)
    acc_ref[...] = jnp.zeros(acc_ref.shape, f32)

    def key_chunk(c, carry):
        rows = pl.ds(pl.multiple_of(c * kc, kc), kc)
        keep = keep_ref[c] > 0.5
        for n in range(AT_KV_HEADS):
            kn = k_ref[rows, n * HEAD_DIM:(n + 1) * HEAD_DIM].astype(bf16)
            vnt = v_ref[rows, n * HEAD_DIM:(n + 1) * HEAD_DIM].T.astype(bf16)
            for g in range(AT_GROUP):
                h = n * AT_GROUP + g
                s = _dot(kn, q_ref[h]) * (HEAD_DIM ** -0.5)
                s = jnp.where(keep, s, NEG_BIG)
                m_old = m_ref[h]
                m_new = jnp.maximum(m_old, jnp.max(s, axis=0, keepdims=True))
                alpha = jnp.exp(m_old - m_new)
                e = jnp.where(keep, jnp.exp(s - m_new), 0.0)
                l_ref[h] = alpha * l_ref[h] + jnp.sum(e, axis=0, keepdims=True)
                acc_ref[h] = alpha * acc_ref[h] + _dot(vnt, e.astype(bf16))
                m_ref[h] = m_new
        return carry

    lax.fori_loop(0, ((i + 1) * qb + kc - 1) // kc, key_chunk, 0)
    for h in range(AT_HEADS):
        o_ref[:, h * HEAD_DIM:(h + 1) * HEAD_DIM] = (acc_ref[h] / l_ref[h]).T.astype(o_ref.dtype)


def _dsa_prompt(p, k, v, ki, q_norm, layer, *, nb, t_len):
    n_sel = min(TOPK_MAX, t_len // 4)
    nq = t_len // Q_BLOCK
    kc = DSA_KEY_CHUNK if t_len % DSA_KEY_CHUNK == 0 else Q_BLOCK
    return pl.pallas_call(
        functools.partial(_dsa_prompt_kernel, t_len=t_len, n_sel=n_sel),
        grid=(nb, nq),
        in_specs=[pl.BlockSpec((Q_BLOCK, AT_W), lambda b, i: (b * nq + i, OFF_AQ // AT_W)),
                  pl.BlockSpec((Q_BLOCK, IDX_HEADS * IDX_DIM),
                               lambda b, i: (b * nq + i, OFF_AQI // (IDX_HEADS * IDX_DIM))),
                  pl.BlockSpec((Q_BLOCK, LANES), lambda b, i: (b * nq + i, OFF_AWI // LANES)),
                  pl.BlockSpec((t_len, KV_W), lambda b, i: (b, 0)),
                  pl.BlockSpec((t_len, KV_W), lambda b, i: (b, 0)),
                  pl.BlockSpec((t_len, IDX_DIM), lambda b, i: (b, 0)),
                  pl.BlockSpec((None, 1, HEAD_DIM), lambda b, i: (layer, 0, 0))],
        out_specs=pl.BlockSpec((Q_BLOCK, AT_W), lambda b, i: (b * nq + i, 0)),
        out_shape=jax.ShapeDtypeStruct((nb * t_len, AT_W), bf16),
        scratch_shapes=[pltpu.VMEM((Q_BLOCK, t_len), jnp.int32),
                        pltpu.VMEM((t_len // kc, kc, Q_BLOCK), f32),
                        pltpu.VMEM((AT_HEADS, HEAD_DIM, Q_BLOCK), bf16),
                        pltpu.VMEM((AT_HEADS, 1, Q_BLOCK), f32),
                        pltpu.VMEM((AT_HEADS, 1, Q_BLOCK), f32),
                        pltpu.VMEM((AT_HEADS, HEAD_DIM, Q_BLOCK), f32)],
        compiler_params=_cparams(("parallel", "arbitrary")),
    )(p, p, p, k, v, ki, q_norm)


def _dsa_sample_score_kernel(pt_ref, aqi_ref, awi_ref, *refs, n_steps, npg, valid):
    page_refs, knew_ref, o_ref = refs[:npg], refs[npg], refs[npg + 1]
    g = pl.program_id(1)
    rws = SAMPLE_ROWS
    ps = page_refs[0].shape[0]
    qi = aqi_ref[...]
    wi_col = _wi_column(awi_ref[...])

    @pl.when(g < n_steps)
    def _():
        keys = jnp.concatenate([r[...] for r in page_refs], axis=0)
        o_ref[...] = _index_scores(qi, wi_col, keys)

    @pl.when(g == n_steps)
    def _():
        keys = jnp.concatenate([knew_ref[...], jnp.zeros((ps - rws, IDX_DIM), f32)], axis=0)
        s = _index_scores(qi, wi_col, keys)
        key_i = lax.broadcasted_iota(jnp.int32, (rws, ps), 1)
        q_i = lax.broadcasted_iota(jnp.int32, (rws, ps), 0)
        o_ref[:, :ps] = jnp.where((key_i <= q_i) & (key_i < valid), s, NEG_BIG)
        if npg > 1:
            o_ref[:, ps:] = jnp.full((rws, (npg - 1) * ps), NEG_BIG, f32)


def _page_specs(n_pages, npg, ps, width, layer):
    def spec(j):
        return pl.BlockSpec((None, None, ps, width),
                            lambda b, g, pt: (layer, pt[b, jnp.minimum(g * npg + j, n_pages - 1)], 0, 0))
    return [spec(j) for j in range(npg)]


def _dsa_sample_scores(p, ki_new, cache_kidx, page_table, layer, *, nb, valid):
    n_pages = page_table.shape[1]
    ps = cache_kidx.shape[2]
    rws = SAMPLE_ROWS
    npg = min(SAMPLE_PAGES, n_pages)
    assert n_pages % npg == 0
    n_steps = n_pages // npg
    grid_spec = pltpu.PrefetchScalarGridSpec(
        num_scalar_prefetch=1,
        grid=(nb, n_steps + 1),
        in_specs=[pl.BlockSpec((rws, IDX_HEADS * IDX_DIM),
                               lambda b, g, pt: (b, OFF_AQI // (IDX_HEADS * IDX_DIM))),
                  pl.BlockSpec((rws, LANES), lambda b, g, pt: (b, OFF_AWI // LANES))]
        + _page_specs(n_pages, npg, ps, IDX_DIM, layer)
        + [pl.BlockSpec((rws, IDX_DIM), lambda b, g, pt: (b, 0))],
        out_specs=pl.BlockSpec((None, rws, npg * ps), lambda b, g, pt: (b, 0, g)),
    )
    return pl.pallas_call(
        functools.partial(_dsa_sample_score_kernel, n_steps=n_steps, npg=npg, valid=valid),
        grid_spec=grid_spec,
        out_shape=jax.ShapeDtypeStruct((nb, rws, (n_steps + 1) * npg * ps), f32),
        compiler_params=_cparams(("parallel", "arbitrary")),
    )(page_table, p, p, *([cache_kidx] * npg), ki_new)


def _dsa_sample_select_kernel(s_ref, o_ref, *, n_sel, n_keys):
    chosen = _select_topk(_sortable(s_ref[...]), n_sel, n_keys)
    o_ref[...] = jnp.where(chosen & (s_ref[...] > 0.5 * NEG_BIG), 1.0, 0.0)


def _dsa_sample_select(scores, n_sel):
    nb, rws, n_keys = scores.shape
    spec = pl.BlockSpec((None, rws, n_keys), lambda b: (b, 0, 0))
    return pl.pallas_call(
        functools.partial(_dsa_sample_select_kernel, n_sel=n_sel, n_keys=n_keys),
        grid=(nb,),
        in_specs=[spec],
        out_specs=spec,
        out_shape=jax.ShapeDtypeStruct(scores.shape, f32),
        compiler_params=_cparams(("parallel",)),
    )(scores)


def _dsa_sample_attn_kernel(pt_ref, aq_ref, keep_ref, *refs, n_steps, npg):
    kpage_refs, vpage_refs = refs[:npg], refs[npg:2 * npg]
    knew_ref, vnew_ref, qn_ref, o_ref, q_ref, m_ref, l_ref, acc_ref = refs[2 * npg:]
    g = pl.program_id(1)
    rws = SAMPLE_ROWS
    ps = kpage_refs[0].shape[0]

    @pl.when(g == 0)
    def _():
        m_ref[...] = jnp.full(m_ref.shape, NEG_BIG, f32)
        l_ref[...] = jnp.zeros(l_ref.shape, f32)
        acc_ref[...] = jnp.zeros(acc_ref.shape, f32)
        for n in range(AT_KV_HEADS):
            q_ref[n] = _q_heads(aq_ref[...], qn_ref[...], n).astype(bf16)

    def step(k_of, v_of, keep):
        keep4 = jnp.concatenate([keep] * AT_GROUP, axis=0) > 0.5
        for n in range(AT_KV_HEADS):
            s = _dot_nt(q_ref[n], k_of(n)) * (HEAD_DIM ** -0.5)
            s = jnp.where(keep4, s, NEG_BIG)
            m_old = m_ref[n]
            m_new = jnp.maximum(m_old, jnp.max(s, axis=1, keepdims=True))
            alpha = jnp.exp(m_old - m_new)
            e = jnp.where(keep4, jnp.exp(s - m_new), 0.0)
            l_ref[n] = alpha * l_ref[n] + jnp.sum(e, axis=1, keepdims=True)
            acc_ref[n] = alpha * acc_ref[n] + _dot(e.astype(bf16), v_of(n))
            m_ref[n] = m_new

    def head_cols(blocks, n):
        cols = [blk[:, n * HEAD_DIM:(n + 1) * HEAD_DIM].astype(bf16) for blk in blocks]
        return jnp.concatenate(cols, axis=0) if len(cols) > 1 else cols[0]

    @pl.when(g < n_steps)
    def _():
        step(functools.partial(head_cols, kpage_refs), functools.partial(head_cols, vpage_refs), keep_ref[...])

    @pl.when(g == n_steps)
    def _():
        pad = jnp.zeros((ps - rws, KV_W), f32)
        knew = jnp.concatenate([knew_ref[...], pad], axis=0)
        vnew = jnp.concatenate([vnew_ref[...], pad], axis=0)
        step(functools.partial(head_cols, [knew]), functools.partial(head_cols, [vnew]), keep_ref[:, :ps])
        for n in range(AT_KV_HEADS):
            o = acc_ref[n] / l_ref[n]
            for gq in range(AT_GROUP):
                h = n * AT_GROUP + gq
                o_ref[:, h * HEAD_DIM:(h + 1) * HEAD_DIM] = o[gq * rws:(gq + 1) * rws].astype(o_ref.dtype)


def _dsa_sample_attn(p, keep, cache_k, cache_v, k_new, v_new, q_norm, page_table, layer, *, nb):
    n_pages = page_table.shape[1]
    ps = cache_k.shape[2]
    rws = SAMPLE_ROWS
    npg = min(SAMPLE_PAGES, n_pages)
    n_steps = n_pages // npg
    new = pl.BlockSpec((rws, KV_W), lambda b, g, pt: (b, 0))
    grid_spec = pltpu.PrefetchScalarGridSpec(
        num_scalar_prefetch=1,
        grid=(nb, n_steps + 1),
        in_specs=[pl.BlockSpec((rws, AT_W), lambda b, g, pt: (b, OFF_AQ // AT_W)),
                  pl.BlockSpec((None, rws, npg * ps), lambda b, g, pt: (b, 0, g))]
        + _page_specs(n_pages, npg, ps, KV_W, layer) + _page_specs(n_pages, npg, ps, KV_W, layer)
        + [new, new, pl.BlockSpec((None, 1, HEAD_DIM), lambda b, g, pt: (layer, 0, 0))],
        out_specs=pl.BlockSpec((rws, AT_W), lambda b, g, pt: (b, 0)),
        scratch_shapes=[pltpu.VMEM((AT_KV_HEADS, AT_GROUP * rws, HEAD_DIM), bf16),
                        pltpu.VMEM((AT_KV_HEADS, AT_GROUP * rws, 1), f32),
                        pltpu.VMEM((AT_KV_HEADS, AT_GROUP * rws, 1), f32),
                        pltpu.VMEM((AT_KV_HEADS, AT_GROUP * rws, HEAD_DIM), f32)],
    )
    return pl.pallas_call(
        functools.partial(_dsa_sample_attn_kernel, n_steps=n_steps, npg=npg),
        grid_spec=grid_spec,
        out_shape=jax.ShapeDtypeStruct((nb * rws, AT_W), bf16),
        compiler_params=_cparams(("parallel", "arbitrary")),
    )(page_table, p, keep, *([cache_k] * npg), *([cache_v] * npg), k_new, v_new, q_norm)


def _prep_weights(w_in, rwkv_mu, rwkv_w2, rwkv_a2, rwkv_g2, w_out, w_gate, w_up, w_down):
    depth, d, _ = w_in.shape
    hg, rw, at = jnp.split(w_in.astype(bf16), [HG_COLS, HG_COLS + RW_COLS], axis=-1)
    aq, ak, av, aqi, aki, awi = jnp.split(at, np.cumsum([AT_W, KV_W, KV_W, IDX_HEADS * IDX_DIM, IDX_DIM])
                                          .tolist(), axis=-1)
    r, k, v, lora = jnp.split(rw, [RW_W, 2 * RW_W, 3 * RW_W], axis=-1)
    z = lambda n: jnp.zeros((depth, d, n), bf16)
    w_in_p = jnp.concatenate(
        [hg, aq, aqi, ak, av, aki, awi, z(OFF_LORA - OFF_AWI - IDX_HEADS), lora, z(LORA_PAD - RW_LORA), r, k, v],
        axis=-1)
    assert w_in_p.shape[-1] == IN_PAD
    mu_r, mu_l = rwkv_mu[:, :3 * RW_W], rwkv_mu[:, 3 * RW_W:]
    mu_pad = jnp.concatenate([mu_r, mu_l, jnp.zeros((depth, LORA_PAD - RW_LORA), f32)], axis=-1)[:, None, :]
    zl = lambda n: jnp.zeros((depth, n, RW_W), bf16)
    w2p = jnp.concatenate([rwkv_w2.astype(bf16), zl(LORA_PAD - RW_DECAY_LORA)], axis=1)
    a2p = jnp.concatenate([zl(RW_DECAY_LORA), rwkv_a2.astype(bf16),
                           zl(LORA_PAD - RW_DECAY_LORA - RW_AAA_LORA)], axis=1)
    g2p = jnp.concatenate([zl(RW_DECAY_LORA + RW_AAA_LORA), rwkv_g2.astype(bf16), zl(LORA_PAD - RW_LORA)], axis=1)
    return dict(w_in=w_in_p, mu_pad=mu_pad, w2p=w2p, a2p=a2p, g2p=g2p,
                w_out=w_out.astype(bf16), w_gate=w_gate.astype(bf16), w_up=w_up.astype(bf16),
                w_down=w_down.astype(bf16))


def _shift_row(p_row):
    return jnp.concatenate([p_row[..., OFF_R:OFF_R + 3 * RW_W], p_row[..., OFF_LORA:OFF_LORA + RW_LORA]], axis=-1)


def _k_tile(f):
    half = f // 2
    return half if f % 2 == 0 and half % LANES == 0 else f


def _layer(x, layer, wts, *, nb, tp, t_valid, hg_s0, rw_s0, shift0, attend):
    h = _rmsnorm(x, wts["ln1"], layer)
    p = _matmul(h, wts["w_in"], layer, tm=1024, tn=512, tk=h.shape[1])
    valid = None if t_valid == tp else t_valid
    rows = min(256, tp)
    o_hg, hg_s = _gla(p, wts["lbs"], wts["hgrn_norm"], hg_s0, layer, nb=nb, tp=tp, rows=rows, valid=valid)
    o_rw, rw_s = _rwkv(p, shift0, rw_s0, wts, layer, nb=nb, tp=tp, t_valid=t_valid, rows=rows)
    k, v, ki = _kv_prep(p, wts["k_norm"], layer)
    o_at = attend(p, k, v, ki)
    mix = jnp.concatenate([o_hg, o_rw, o_at], axis=-1)
    x = _matmul(mix, wts["w_out"], layer, tm=1024, tn=512, tk=mix.shape[1], res=x)
    h2 = _rmsnorm(x, wts["ln2"], layer)
    act = _swiglu(h2, wts["w_gate"], wts["w_up"], layer, tm=1024, tn=256)
    x = _matmul(act, wts["w_down"], layer, tm=1024, tn=512, tk=_k_tile(act.shape[1]), res=x)
    shift = _shift_row(p.reshape(nb, tp, IN_PAD)[:, t_valid - 1])
    return x, (k, v, ki, hg_s, rw_s, shift)


def kernel(x_prompt, x_sample, cache_k, cache_v, cache_kidx, state_hgrn, state_rwkv, state_shift, page_table,
           ln1, w_in, hgrn_lb, hgrn_norm, rwkv_mu, rwkv_w0, rwkv_w2, rwkv_a0, rwkv_a2, rwkv_g2, rwkv_kk,
           rwkv_ka, rwkv_rk, rwkv_lnx_w, rwkv_lnx_b, q_norm, k_norm, w_out, ln2, w_gate, w_up, w_down):
    depth = w_in.shape[0]
    B, T, D = x_prompt.shape
    DB, DS, _ = x_sample.shape
    n_pool, page_size = cache_k.shape[1], cache_k.shape[2]
    past = page_table.shape[1] * page_size

    wts = _prep_weights(w_in, rwkv_mu, rwkv_w2, rwkv_a2, rwkv_g2, w_out, w_gate, w_up, w_down)
    lb_p = jax.nn.softmax(hgrn_lb.astype(f32), axis=0)
    row = lambda a: a.astype(f32).reshape(depth, 1, -1)
    wts.update(lbs=(jnp.cumsum(lb_p, axis=0) - lb_p[0:1])[:, None, :], hgrn_norm=row(hgrn_norm),
               ln1=row(ln1), ln2=row(ln2), w0=row(rwkv_w0), a0=row(rwkv_a0), kkw=row(rwkv_kk),
               kaw=row(rwkv_ka), rkw=row(rwkv_rk), lnw=row(rwkv_lnx_w), lnb=row(rwkv_lnx_b),
               q_norm=row(q_norm), k_norm=row(k_norm))
    ck = cache_k.reshape(depth, n_pool, page_size, KV_W)
    cv = cache_v.reshape(depth, n_pool, page_size, KV_W)

    xp = x_prompt.reshape(B * T, D)
    outs_p = []
    zeros_hg = jnp.zeros((B, HG_HEADS, HG_DK, HG_DV), f32)
    zeros_rw = jnp.zeros((B, RW_HEADS, RW_HD, RW_HD), f32)
    zeros_sh = jnp.zeros((B, RW_COLS), f32)
    for l in range(depth):
        attend = lambda p, k, v, ki, l=l: _dsa_prompt(p, k, v, ki, wts["q_norm"], l, nb=B, t_len=T)
        xp, st = _layer(xp, l, wts, nb=B, tp=T, t_valid=T, hg_s0=zeros_hg, rw_s0=zeros_rw, shift0=zeros_sh,
                        attend=attend)
        outs_p.append(st)

    TP = SAMPLE_ROWS
    xs = jnp.pad(x_sample, ((0, 0), (0, TP - DS), (0, 0))).reshape(DB * TP, D)
    n_sel_s = min(TOPK_MAX, (past + DS) // 4)
    outs_s = []
    for l in range(depth):
        def attend(p, k, v, ki, l=l):
            scores = _dsa_sample_scores(p, ki, cache_kidx, page_table, l, nb=DB, valid=DS)
            keep = _dsa_sample_select(scores, n_sel_s)
            return _dsa_sample_attn(p, keep, ck, cv, k, v, wts["q_norm"], page_table, l, nb=DB)

        xs, st = _layer(xs, l, wts, nb=DB, tp=TP, t_valid=DS, hg_s0=state_hgrn[l], rw_s0=state_rwkv[l],
                        shift0=state_shift[l], attend=attend)
        outs_s.append(st)

    def stack(outs, i):
        return jnp.stack([o[i] for o in outs])

    k_p = stack(outs_p, 0).reshape(depth, B, T, AT_KV_HEADS, HEAD_DIM)
    v_p = stack(outs_p, 1).reshape(depth, B, T, AT_KV_HEADS, HEAD_DIM)
    ki_p = stack(outs_p, 2).reshape(depth, B, T, IDX_DIM)
    cut = lambda a, w: a.reshape(depth, DB, TP, *w)[:, :, :DS]
    k_s = cut(stack(outs_s, 0), (AT_KV_HEADS, HEAD_DIM))
    v_s = cut(stack(outs_s, 1), (AT_KV_HEADS, HEAD_DIM))
    ki_s = cut(stack(outs_s, 2), (IDX_DIM,))
    y_p = xp.reshape(B, T, D)
    y_s = xs.reshape(DB, TP, D)[:, :DS]
    return (y_p, y_s, k_p, v_p, ki_p, stack(outs_p, 3), stack(outs_p, 4), stack(outs_p, 5),
            k_s, v_s, ki_s, stack(outs_s, 3), stack(outs_s, 4), stack(outs_s, 5))
```

```python
import functools

import jax
import jax.numpy as jnp
import numpy as np
from jax import lax
from jax.experimental import pallas as pl
from jax.experimental.pallas import tpu as pltpu

f32 = jnp.float32
bf16 = jnp.bfloat16

HG_HEADS, HG_DK, HG_DV = 8, 128, 128
HG_W = HG_HEADS * HG_DV
EXP_CLIP = 60.0
RW_HEADS, RW_HD = 16, 64
RW_W = RW_HEADS * RW_HD
RW_DECAY_LORA, RW_AAA_LORA, RW_GATE_LORA = 64, 64, 160
RW_LORA = RW_DECAY_LORA + RW_AAA_LORA + RW_GATE_LORA
RW_GN_EPS = 64e-5
RW_COLS = 3 * RW_W + RW_LORA
AT_HEADS, AT_KV_HEADS, HEAD_DIM = 16, 4, 128
AT_GROUP = AT_HEADS // AT_KV_HEADS
AT_W = AT_HEADS * HEAD_DIM
KV_W = AT_KV_HEADS * HEAD_DIM
IDX_HEADS, IDX_DIM = 8, 128
TOPK_MAX = 256
Q_BLOCK = 128
NEG_BIG = -1e30
NORM_EPS = 1e-6
HG_COLS = 2 * HG_HEADS * HG_DK + 2 * HG_W
AT_COLS = AT_W + 2 * KV_W + IDX_HEADS * IDX_DIM + IDX_DIM + IDX_HEADS

LANES = 128
VMEM_LIMIT = 56 * 1024 * 1024
SAMPLE_ROWS = 16
GLA_CHUNK = 128
GLA_SUB = 16
RW_UNROLL = 32
DSA_TIERS = 4
COUNT_ROWS = 64
DSA_KEY_CHUNK = 256
SAMPLE_PAGES = 16

LORA_PAD = 512
OFF_HG = 0
OFF_AQ = HG_COLS
OFF_AQI = OFF_AQ + AT_W
OFF_AK = OFF_AQI + IDX_HEADS * IDX_DIM
OFF_AV = OFF_AK + KV_W
OFF_AKI = OFF_AV + KV_W
OFF_AWI = OFF_AKI + IDX_DIM
OFF_LORA = 8704
OFF_R = OFF_LORA + LORA_PAD
OFF_K = OFF_R + RW_W
OFF_V = OFF_K + RW_W
IN_PAD = OFF_V + RW_W


def _cparams(sem):
    return pltpu.CompilerParams(dimension_semantics=sem, vmem_limit_bytes=VMEM_LIMIT)


def _sigmoid(x):
    return jax.nn.sigmoid(x)


def _dot(a, b, precision=None):
    return jnp.dot(a, b, preferred_element_type=f32, precision=precision)


def _dot_nt(a, b, precision=None):
    return lax.dot_general(a, b, (((1,), (1,)), ((), ())), preferred_element_type=f32, precision=precision)


def _dot_tn(a, b, precision=None):
    return lax.dot_general(a, b, (((0,), (0,)), ((), ())), preferred_element_type=f32, precision=precision)


HI = lax.Precision.HIGHEST


def _rmsnorm_kernel(x_ref, g_ref, o_ref):
    x = x_ref[...]
    ms = jnp.mean(x * x, axis=-1, keepdims=True)
    o_ref[...] = (x * lax.rsqrt(ms + NORM_EPS) * g_ref[...]).astype(o_ref.dtype)


def _rmsnorm(x, g, layer):
    m, d = x.shape
    tr = min(256, m)
    return pl.pallas_call(
        _rmsnorm_kernel,
        grid=(m // tr,),
        in_specs=[pl.BlockSpec((tr, d), lambda i: (i, 0)),
                  pl.BlockSpec((None, 1, d), lambda i: (layer, 0, 0))],
        out_specs=pl.BlockSpec((tr, d), lambda i: (i, 0)),
        out_shape=jax.ShapeDtypeStruct((m, d), bf16),
        compiler_params=_cparams(("parallel",)),
    )(x, g)


def _mm_kernel(*refs, nk, has_res):
    if has_res:
        a_ref, b_ref, r_ref, o_ref = refs[:4]
    else:
        a_ref, b_ref, o_ref = refs[:3]
        r_ref = None
    part = _dot(a_ref[...], b_ref[...])
    if nk == 1:
        o_ref[...] = (part + r_ref[...]) if has_res else part
        return
    acc_ref = refs[-1]
    k = pl.program_id(2)

    @pl.when(k == 0)
    def _():
        acc_ref[...] = part

    @pl.when(k > 0)
    def _():
        acc_ref[...] += part

    @pl.when(k == nk - 1)
    def _():
        o_ref[...] = (acc_ref[...] + r_ref[...]) if has_res else acc_ref[...]


def _matmul(a, w, layer, *, tm, tn, tk, res=None):
    m, kdim = a.shape
    n = w.shape[-1]
    tm = min(tm, m)
    tn = min(tn, n)
    nk = kdim // tk
    assert m % tm == 0 and n % tn == 0 and kdim % tk == 0
    in_specs = [pl.BlockSpec((tm, tk), lambda i, j, k: (i, k)),
                pl.BlockSpec((None, tk, tn), lambda i, j, k: (layer, k, j))]
    args = [a, w]
    if res is not None:
        in_specs.append(pl.BlockSpec((tm, tn), lambda i, j, k: (i, j)))
        args.append(res)
    scratch = [pltpu.VMEM((tm, tn), f32)] if nk > 1 else []
    return pl.pallas_call(
        functools.partial(_mm_kernel, nk=nk, has_res=res is not None),
        grid=(m // tm, n // tn, nk),
        in_specs=in_specs,
        out_specs=pl.BlockSpec((tm, tn), lambda i, j, k: (i, j)),
        out_shape=jax.ShapeDtypeStruct((m, n), f32),
        scratch_shapes=scratch,
        compiler_params=_cparams(("parallel", "parallel", "arbitrary")),
    )(*args)


def _swiglu_kernel(a_ref, wg_ref, wu_ref, o_ref):
    a = a_ref[...]
    g = _dot(a, wg_ref[...])
    u = _dot(a, wu_ref[...])
    o_ref[...] = (g * _sigmoid(g) * u).astype(o_ref.dtype)


def _swiglu(a, wg, wu, layer, *, tm, tn):
    m, d = a.shape
    n = wg.shape[-1]
    tm = min(tm, m)
    assert m % tm == 0 and n % tn == 0
    wspec = pl.BlockSpec((None, d, tn), lambda i, j: (layer, 0, j))
    return pl.pallas_call(
        _swiglu_kernel,
        grid=(m // tm, n // tn),
        in_specs=[pl.BlockSpec((tm, d), lambda i, j: (i, 0)), wspec, wspec],
        out_specs=pl.BlockSpec((tm, tn), lambda i, j: (i, j)),
        out_shape=jax.ShapeDtypeStruct((m, n), bf16),
        compiler_params=_cparams(("parallel", "parallel")),
    )(a, wg, wu)


def _gla_kernel(pq_ref, pf_ref, pi_ref, pg_ref, lb_ref, g_ref, s0_ref, o_ref, sout_ref, st_ref,
                *, rows, valid):
    i = pl.program_id(2)
    chunk, sub = GLA_CHUNK, GLA_SUB
    live_rows = min(rows, chunk)

    @pl.when(i == 0)
    def _():
        st_ref[...] = s0_ref[...].T

    lb = lb_ref[...]
    gain = g_ref[...]
    n_live = live_rows if valid is None else min(valid, live_rows)
    nsb = -(-n_live // sub)
    crow = lax.broadcasted_iota(jnp.int32, (chunk, 1), 0)
    tri = (lax.broadcasted_iota(jnp.int32, (chunk, chunk), 0)
           >= lax.broadcasted_iota(jnp.int32, (chunk, chunk), 1)).astype(f32)
    sub_row = lax.broadcasted_iota(jnp.int32, (sub, 1), 0)

    def padded(x):
        if live_rows == chunk:
            return x
        return jnp.concatenate([x, jnp.zeros((chunk - live_rows, x.shape[1]), x.dtype)], axis=0)

    def one_chunk(c, carry):
        rs = pl.ds(pl.multiple_of(c * live_rows, live_rows), live_rows)
        pq = padded(pq_ref[rs, :])
        fr = padded(pf_ref[rs, :])
        v = padded(pi_ref[rs, :])
        q = pq * _sigmoid(pq) * (HG_DK ** -0.5)
        log_sig = jnp.minimum(fr, 0.0) - jnp.log1p(jnp.exp(-jnp.abs(fr)))
        log_f = log_sig + jnp.log1p(lb * jnp.exp(jnp.minimum(-fr, EXP_CLIP)))
        k = (1.0 - lb) * _sigmoid(-fr)
        if valid is not None or live_rows != chunk:
            live = crow < live_rows
            if valid is not None:
                live = live & (i * rows + c * live_rows + crow < valid)
            log_f = jnp.where(live, log_f, 0.0)
            k = jnp.where(live, k, 0.0)
        b = _dot(tri, log_f, precision=HI)
        st = st_ref[...]
        o = _dot_nt((q * jnp.exp(b)).astype(bf16), st.astype(bf16))
        v16 = v.astype(bf16)
        a_rows, d_rows = [], []
        for ib in range(nsb):
            lo = ib * sub
            q_i = q[lo:lo + sub]
            b_i = b[lo:lo + sub]
            if ib == 0:
                a_rows.append(jnp.zeros((sub, chunk), f32))
            else:
                ref_row = b[lo - 1:lo]
                qt = (q_i * jnp.exp(b_i - ref_row)).astype(bf16)
                kt = jnp.where(crow < lo, k * jnp.exp(jnp.minimum(ref_row - b, 0.0)), 0.0).astype(bf16)
                a_rows.append(_dot_nt(qt, kt))
            d_i = jnp.zeros((sub, HG_DV), f32)
            for s in range(sub):
                gs = lo + s
                term = q_i * jnp.exp(jnp.minimum(b_i - b[gs:gs + 1], 0.0)) * k[gs:gs + 1]
                a_col = jnp.sum(term, axis=1, keepdims=True)
                d_i = d_i + jnp.where(sub_row >= s, a_col, 0.0) * v[gs:gs + 1]
            d_rows.append(d_i)
        if nsb * sub < chunk:
            a_rows.append(jnp.zeros((chunk - nsb * sub, chunk), f32))
            d_rows.append(jnp.zeros((chunk - nsb * sub, HG_DV), f32))
        a_off = jnp.concatenate(a_rows, axis=0)
        o = o + _dot(a_off.astype(bf16), v16) + jnp.concatenate(d_rows, axis=0)
        b_last = b[chunk - 1:chunk]
        kd = (k * jnp.exp(b_last - b)).astype(bf16)
        st_ref[...] = st * jnp.exp(b_last) + _dot_tn(v16, kd)
        o = o[:live_rows]
        ms = jnp.mean(o * o, axis=-1, keepdims=True)
        y = o * lax.rsqrt(ms + NORM_EPS) * gain * _sigmoid(pg_ref[rs, :])
        o_ref[rs, :] = y.astype(o_ref.dtype)
        return carry

    lax.fori_loop(0, rows // live_rows, one_chunk, 0)

    @pl.when(i == pl.num_programs(2) - 1)
    def _():
        sout_ref[...] = st_ref[...].T


def _gla(p, lb, gain, s0, layer, *, nb, tp, rows, valid):
    nblk = tp // rows
    H = HG_HEADS

    def pspec(part):
        return pl.BlockSpec((rows, LANES), lambda b, h, i: (b * nblk + i, part * H + h))

    return pl.pallas_call(
        functools.partial(_gla_kernel, rows=rows, valid=valid),
        grid=(nb, H, nblk),
        in_specs=[pspec(0), pspec(1), pspec(2), pspec(3),
                  pl.BlockSpec((None, 1, LANES), lambda b, h, i: (layer, 0, h)),
                  pl.BlockSpec((None, 1, LANES), lambda b, h, i: (layer, 0, h)),
                  pl.BlockSpec((None, None, HG_DK, HG_DV), lambda b, h, i: (b, h, 0, 0))],
        out_specs=[pl.BlockSpec((rows, LANES), lambda b, h, i: (b * nblk + i, h)),
                   pl.BlockSpec((None, None, HG_DK, HG_DV), lambda b, h, i: (b, h, 0, 0))],
        out_shape=[jax.ShapeDtypeStruct((nb * tp, HG_W), bf16),
                   jax.ShapeDtypeStruct((nb, H, HG_DK, HG_DV), f32)],
        scratch_shapes=[pltpu.VMEM((HG_DV, HG_DK), f32)],
        compiler_params=_cparams(("parallel", "parallel", "arbitrary")),
    )(p, p, p, p, lb, gain, s0)


def _head_sum(x, hm):
    cols = [_dot(x[:, c * LANES:(c + 1) * LANES], hm, precision=HI) for c in range(RW_W // LANES)]
    return jnp.concatenate(cols, axis=1)


def _head_matrix():
    r = lax.broadcasted_iota(jnp.int32, (LANES, LANES), 0) // RW_HD
    c = lax.broadcasted_iota(jnp.int32, (LANES, LANES), 1) // RW_HD
    return (r == c).astype(f32)


def _rw_pre_kernel(pr_ref, pk_ref, pv_ref, pl_ref, sr_ref, sk_ref, sv_ref, sl_ref,
                   mr_ref, mk_ref, mv_ref, ml_ref, w0_ref, w2_ref, a0_ref, a2_ref, g2_ref,
                   kkw_ref, kaw_ref, rkw_ref,
                   r_o, w_o, k_o, v_o, nkk_o, b_o, bonus_o, g_o,
                   cr_ref, ck_ref, cv_ref, cl_ref, *, rows, valid):
    i = pl.program_id(1)

    @pl.when(i == 0)
    def _():
        cr_ref[...] = sr_ref[...]
        ck_ref[...] = sk_ref[...]
        cv_ref[...] = sv_ref[...]
        cl_ref[...] = sl_ref[...]

    def mixed(p_ref, c_ref, m_ref):
        cur = p_ref[...]
        rolled = pltpu.roll(cur, 1, axis=0)
        rowid = lax.broadcasted_iota(jnp.int32, cur.shape, 0)
        prev = jnp.where(rowid == 0, c_ref[...], rolled)
        c_ref[...] = cur[rows - 1:rows, :]
        return cur + (prev - cur) * m_ref[...]

    r = mixed(pr_ref, cr_ref, mr_ref)
    k = mixed(pk_ref, ck_ref, mk_ref)
    v = mixed(pv_ref, cv_ref, mv_ref)
    xl = mixed(pl_ref, cl_ref, ml_ref)

    zw = w0_ref[...] + _dot(jnp.tanh(xl).astype(bf16), w2_ref[...])
    w = jnp.minimum(zw, 0.0) - jnp.log1p(jnp.exp(-jnp.abs(zw))) - 0.5
    decay = jnp.exp(-jnp.exp(w))
    a = _sigmoid(a0_ref[...] + _dot(xl.astype(bf16), a2_ref[...]))
    g = _dot(_sigmoid(xl).astype(bf16), g2_ref[...])

    hm = _head_matrix()
    kk = k * kkw_ref[...]
    kk = kk * lax.rsqrt(jnp.maximum(_head_sum(kk * kk, hm), 1e-24))
    k2 = k * (1.0 + (a - 1.0) * kaw_ref[...])
    bonus = _head_sum(r * k2 * rkw_ref[...], hm) * v
    nkk = -kk
    bb = kk * a
    if valid is not None:
        tok = i * rows + lax.broadcasted_iota(jnp.int32, (rows, 1), 0)
        live = tok < valid
        decay = jnp.where(live, decay, 1.0)
        k2 = jnp.where(live, k2, 0.0)
        v = jnp.where(live, v, 0.0)
        nkk = jnp.where(live, nkk, 0.0)
        bb = jnp.where(live, bb, 0.0)
    r_o[...] = r
    w_o[...] = decay
    k_o[...] = k2
    v_o[...] = v
    nkk_o[...] = nkk
    b_o[...] = bb
    bonus_o[...] = bonus
    g_o[...] = g


def _rw_pre(p, shift_pad, mu_pad, w0, w2p, a0, a2p, g2p, kkw, kaw, rkw, layer, *, nb, tp, rows, valid):
    nblk = tp // rows
    W = RW_W

    def pspec(off, width):
        return pl.BlockSpec((rows, width), lambda b, i: (b * nblk + i, off // width))

    def sspec(off, width):
        return pl.BlockSpec((None, 1, width), lambda b, i: (b, 0, off // width))

    def mspec(off, width):
        return pl.BlockSpec((None, 1, width), lambda b, i: (layer, 0, off // width))

    def vec():
        return pl.BlockSpec((None, 1, W), lambda b, i: (layer, 0, 0))

    def lora():
        return pl.BlockSpec((None, LORA_PAD, W), lambda b, i: (layer, 0, 0))

    out_spec = pl.BlockSpec((rows, W), lambda b, i: (b * nblk + i, 0))
    out_shape = jax.ShapeDtypeStruct((nb * tp, W), f32)
    return pl.pallas_call(
        functools.partial(_rw_pre_kernel, rows=rows, valid=valid),
        grid=(nb, nblk),
        in_specs=[pspec(OFF_R, W), pspec(OFF_K, W), pspec(OFF_V, W), pspec(OFF_LORA, LORA_PAD),
                  sspec(0, W), sspec(W, W), sspec(2 * W, W), sspec(3 * W, LORA_PAD),
                  mspec(0, W), mspec(W, W), mspec(2 * W, W), mspec(3 * W, LORA_PAD),
                  vec(), lora(), vec(), lora(), lora(), vec(), vec(), vec()],
        out_specs=[out_spec] * 8,
        out_shape=[out_shape] * 8,
        scratch_shapes=[pltpu.VMEM((1, W), f32), pltpu.VMEM((1, W), f32), pltpu.VMEM((1, W), f32),
                        pltpu.VMEM((1, LORA_PAD), f32)],
        compiler_params=_cparams(("parallel", "arbitrary")),
    )(p, p, p, p, shift_pad, shift_pad, shift_pad, shift_pad, mu_pad, mu_pad, mu_pad, mu_pad,
      w0, w2p, a0, a2p, g2p, kkw, kaw, rkw)


def _rw_scan_kernel(r_ref, w_ref, k_ref, b_ref, nkk_ref, v_ref, s0_ref, y_ref, sout_ref, s_ref,
                    *, tt, ki_n, fold):
    i = pl.program_id(0)

    @pl.when(i == 0)
    def _():
        s_ref[...] = s0_ref[...]

    def lane_total(x):
        return x + pltpu.roll(x, LANES // 2, axis=1) if fold else x

    half = RW_HD // 2
    halves = (slice(0, half), slice(half, RW_HD))
    zero = jnp.zeros((half, LANES), f32)

    def row(ref, t, ki):
        return ref[t, pl.ds(ki, 1), :]

    def first_sa(vr):
        acc = [zero, zero]
        for ki in range(ki_n):
            acc[ki % 2] = acc[ki % 2] + s_ref[ki, vr, :] * row(nkk_ref, 0, ki)
        return lane_total(acc[0] + acc[1])

    def token(t, sa_pair):
        nxt = jnp.minimum(t + 1, tt - 1)
        sa_next = []
        for vr, sa in zip(halves, sa_pair):
            vt = v_ref[t, vr, :]

            def k_block(kb, carry):
                y0, y1, a0, a1 = carry
                for u in range(RW_UNROLL):
                    ki = kb * RW_UNROLL + u
                    s_new = (s_ref[ki, vr, :] * row(w_ref, t, ki) + sa * row(b_ref, t, ki)
                             + vt * row(k_ref, t, ki))
                    s_ref[ki, vr, :] = s_new
                    if u % 2 == 0:
                        y0 = y0 + s_new * row(r_ref, t, ki)
                        a0 = a0 + s_new * row(nkk_ref, nxt, ki)
                    else:
                        y1 = y1 + s_new * row(r_ref, t, ki)
                        a1 = a1 + s_new * row(nkk_ref, nxt, ki)
                return y0, y1, a0, a1

            n_blocks = ki_n // RW_UNROLL
            if n_blocks == 1:
                y0, y1, a0, a1 = k_block(0, (zero, zero, zero, zero))
            else:
                y0, y1, a0, a1 = lax.fori_loop(0, n_blocks, k_block, (zero, zero, zero, zero))
            y_ref[t, vr, :] = lane_total(y0 + y1)
            sa_next.append(lane_total(a0 + a1))
        return tuple(sa_next)

    lax.fori_loop(0, tt, token, tuple(first_sa(vr) for vr in halves))

    @pl.when(i == pl.num_programs(0) - 1)
    def _():
        sout_ref[...] = s_ref[...]


def _rw_scan(rT, wT, kT, bT, nkkT, vT, s0T, *, tt):
    t_len, ki_n, _ = rT.shape
    fold = ki_n * 2 == RW_HD
    assert fold or ki_n == RW_HD
    tt = min(tt, t_len)
    assert t_len % tt == 0 and ki_n % RW_UNROLL == 0
    op = pl.BlockSpec((tt, ki_n, LANES), lambda i: (i, 0, 0))
    vs = pl.BlockSpec((tt, RW_HD, LANES), lambda i: (i, 0, 0))
    ss = pl.BlockSpec((ki_n, RW_HD, LANES), lambda i: (0, 0, 0))
    return pl.pallas_call(
        functools.partial(_rw_scan_kernel, tt=tt, ki_n=ki_n, fold=fold),
        grid=(t_len // tt,),
        in_specs=[op, op, op, op, op, vs, ss],
        out_specs=[vs, ss],
        out_shape=[jax.ShapeDtypeStruct((t_len, RW_HD, LANES), f32),
                   jax.ShapeDtypeStruct((ki_n, RW_HD, LANES), f32)],
        scratch_shapes=[pltpu.VMEM((ki_n, RW_HD, LANES), f32)],
        compiler_params=_cparams(("arbitrary",)),
    )(rT, wT, kT, bT, nkkT, vT, s0T)


def _rw_post_kernel(y_ref, bonus_ref, g_ref, lw_ref, lb_ref, o_ref):
    hm = _head_matrix()
    y = y_ref[...]
    mean = _head_sum(y, hm) * (1.0 / RW_HD)
    d = y - mean
    var = _head_sum(d * d, hm) * (1.0 / RW_HD)
    yn = d * lax.rsqrt(var + RW_GN_EPS) * lw_ref[...] + lb_ref[...]
    o_ref[...] = ((yn + bonus_ref[...]) * g_ref[...]).astype(o_ref.dtype)


def _rw_post(y, bonus, g, lnw, lnb, layer):
    m = y.shape[0]
    tr = min(512, m)
    spec = pl.BlockSpec((tr, RW_W), lambda i: (i, 0))
    vec = pl.BlockSpec((None, 1, RW_W), lambda i: (layer, 0, 0))
    return pl.pallas_call(
        _rw_post_kernel,
        grid=(m // tr,),
        in_specs=[spec, spec, spec, vec, vec],
        out_specs=spec,
        out_shape=jax.ShapeDtypeStruct((m, RW_W), bf16),
        compiler_params=_cparams(("parallel",)),
    )(y, bonus, g, lnw, lnb)


def _rwkv(p, shift0, s0, wts, layer, *, nb, tp, t_valid, rows):
    H, N = RW_HEADS, RW_HD
    kh = LANES // (nb * H)
    assert kh in (1, 2) and kh * nb * H == LANES
    ki_n = N // kh
    shift_pad = jnp.pad(shift0, ((0, 0), (0, 3 * RW_W + LORA_PAD - RW_COLS)))[:, None, :]
    valid = None if t_valid == tp else t_valid
    r, w, k, v, nkk, bb, bonus, g = _rw_pre(
        p, shift_pad, wts["mu_pad"], wts["w0"], wts["w2p"], wts["a0"], wts["a2p"], wts["g2p"],
        wts["kkw"], wts["kaw"], wts["rkw"], layer, nb=nb, tp=tp, rows=rows, valid=valid)

    def key_lanes(x):
        x = x.reshape(nb, tp, H, kh, ki_n)[:, :t_valid]
        return x.transpose(1, 4, 3, 0, 2).reshape(t_valid, ki_n, LANES)

    vT = jnp.broadcast_to(v.reshape(nb, tp, H, 1, N)[:, :t_valid], (nb, t_valid, H, kh, N))
    vT = vT.transpose(1, 4, 3, 0, 2).reshape(t_valid, N, LANES)
    s0T = s0.reshape(nb, H, N, kh, ki_n).transpose(4, 2, 3, 0, 1).reshape(ki_n, N, LANES)
    yT, sT = _rw_scan(key_lanes(r), key_lanes(w), key_lanes(k), key_lanes(bb), key_lanes(nkk), vT, s0T,
                      tt=64)
    y = yT[:, :, :nb * H].reshape(t_valid, N, nb, H).transpose(2, 0, 3, 1).reshape(nb, t_valid, RW_W)
    if t_valid != tp:
        y = jnp.pad(y, ((0, 0), (0, tp - t_valid), (0, 0)))
    y = y.reshape(nb * tp, RW_W)
    s_out = sT.reshape(ki_n, N, kh, nb, H).transpose(3, 4, 1, 2, 0).reshape(nb, H, N, N)
    o = _rw_post(y, bonus, g, wts["lnw"], wts["lnb"], layer)
    return o, s_out


def _kv_prep_kernel(ak_ref, av_ref, aki_ref, kn_ref, k_o, v_o, ki_o):
    gain = kn_ref[...]
    for n in range(AT_KV_HEADS):
        x = ak_ref[:, n * HEAD_DIM:(n + 1) * HEAD_DIM]
        ms = jnp.mean(x * x, axis=-1, keepdims=True)
        k_o[:, n * HEAD_DIM:(n + 1) * HEAD_DIM] = x * lax.rsqrt(ms + NORM_EPS) * gain
    v_o[...] = av_ref[...]
    ki_o[...] = aki_ref[...]


def _kv_prep(p, k_norm, layer):
    m = p.shape[0]
    tr = min(512, m)
    return pl.pallas_call(
        _kv_prep_kernel,
        grid=(m // tr,),
        in_specs=[pl.BlockSpec((tr, KV_W), lambda i: (i, OFF_AK // KV_W)),
                  pl.BlockSpec((tr, KV_W), lambda i: (i, OFF_AV // KV_W)),
                  pl.BlockSpec((tr, IDX_DIM), lambda i: (i, OFF_AKI // IDX_DIM)),
                  pl.BlockSpec((None, 1, HEAD_DIM), lambda i: (layer, 0, 0))],
        out_specs=[pl.BlockSpec((tr, KV_W), lambda i: (i, 0)),
                   pl.BlockSpec((tr, KV_W), lambda i: (i, 0)),
                   pl.BlockSpec((tr, IDX_DIM), lambda i: (i, 0))],
        out_shape=[jax.ShapeDtypeStruct((m, KV_W), f32), jax.ShapeDtypeStruct((m, KV_W), f32),
                   jax.ShapeDtypeStruct((m, IDX_DIM), f32)],
        compiler_params=_cparams(("parallel",)),
    )(p, p, p, k_norm)


def _index_scores(qi, wi_col, keys):
    rws = qi.shape[0]
    qs = jnp.concatenate([qi[:, h * IDX_DIM:(h + 1) * IDX_DIM] for h in range(IDX_HEADS)], axis=0)
    d = jnp.maximum(_dot_nt(qs, keys, precision=HI), 0.0) * wi_col
    s = d[0:rws]
    for h in range(1, IDX_HEADS):
        s = s + d[h * rws:(h + 1) * rws]
    return s


def _wi_column(awi):
    scale = IDX_HEADS ** -0.5 * IDX_DIM ** -0.5
    return jnp.concatenate([awi[:, h:h + 1] for h in range(IDX_HEADS)], axis=0) * scale


def _sortable(score):
    bits = lax.bitcast_convert_type(score, jnp.int32)
    key = jnp.where(bits < 0, bits ^ jnp.int32(0x7FFFFFFF), bits)
    return jnp.where(score == 0.0, 0, key)


def _select_topk(skey, n_sel, n_keys, axis=1):
    one = tuple(1 if a == axis else s for a, s in enumerate(skey.shape))
    nsel = jnp.float32(n_sel)
    int_min = jnp.int32(-2 ** 31)

    def _count(mask):
        ones = mask.astype(f32)
        if axis == 0 and ones.shape[0] % COUNT_ROWS == 0 and ones.shape[0] > COUNT_ROWS:
            ones = jnp.sum(ones.reshape(-1, COUNT_ROWS, ones.shape[1]), axis=0)
        return jnp.sum(ones, axis=axis, keepdims=True)

    zero = jnp.zeros(one, jnp.int32)
    cand = jnp.where(_count(skey >= zero) >= nsel, zero, zero + int_min)

    def bit_step(it, cand):
        trial = cand + jnp.left_shift(jnp.int32(1), 30 - it)
        return jnp.where(_count(skey >= trial) >= nsel, trial, cand)

    tau = lax.fori_loop(0, 31, bit_step, cand)
    gt = skey > tau
    eq = skey == tau
    need = nsel - _count(gt)
    idx = lax.broadcasted_iota(jnp.int32, skey.shape, axis)
    nbits = int(n_keys).bit_length()

    def idx_step(it, x):
        trial = x + jnp.left_shift(jnp.int32(1), nbits - 1 - it)
        ok = (trial <= n_keys) & (_count(eq & (idx < trial)) < need)
        return jnp.where(ok, trial, x)

    surplus = jnp.max(_count(eq) - need) > 0.0
    x = lax.cond(surplus, lambda: lax.fori_loop(0, nbits, idx_step, zero), lambda: zero + n_keys)
    return gt | (eq & (idx <= x))


def _q_heads(aq, gain, n):
    outs = []
    for g in range(AT_GROUP):
        h = n * AT_GROUP + g
        x = aq[:, h * HEAD_DIM:(h + 1) * HEAD_DIM]
        ms = jnp.mean(x * x, axis=-1, keepdims=True)
        outs.append(x * lax.rsqrt(ms + NORM_EPS) * gain)
    return jnp.concatenate(outs, axis=0)


def _dsa_prompt_kernel(aq_ref, aqi_ref, awi_ref, k_ref, v_ref, ki_ref, qn_ref, o_ref,
                       sel_ref, keep_ref, kb_ref, vt_ref, q_ref, m_ref, l_ref, acc_ref, *, t_len, n_sel):
    i = pl.program_id(1)
    qb = Q_BLOCK
    nkb = t_len // qb
    kc = DSA_KEY_CHUNK if t_len % DSA_KEY_CHUNK == 0 else qb

    @pl.when(i == 0)
    def _():
        kb_ref[...] = k_ref[...].astype(bf16)
        for n in range(AT_KV_HEADS):
            for j in range(nkb):
                tile = v_ref[j * qb:(j + 1) * qb, n * HEAD_DIM:(n + 1) * HEAD_DIM].T.astype(bf16)
                lo = (j * qb) % kc
                vt_ref[n, (j * qb) // kc, :, lo:lo + qb] = tile

    qi = aqi_ref[...]
    qs = jnp.concatenate([qi[:, h * IDX_DIM:(h + 1) * IDX_DIM] for h in range(IDX_HEADS)], axis=0)
    wi_t = awi_ref[...].T
    wi_row = jnp.concatenate([wi_t[h:h + 1, :] for h in range(IDX_HEADS)], axis=1)
    wi_row = wi_row * (IDX_HEADS ** -0.5 * IDX_DIM ** -0.5)
    q_pos = i * qb + lax.broadcasted_iota(jnp.int32, (1, qb), 1)

    for j in range(nkb):
        rows = slice(j * qb, (j + 1) * qb)

        @pl.when(j <= i)
        def _():
            d = jnp.maximum(_dot_nt(ki_ref[rows, :], qs, precision=HI), 0.0) * wi_row
            s = d[:, 0:qb]
            for h in range(1, IDX_HEADS):
                s = s + d[:, h * qb:(h + 1) * qb]
            key_pos = j * qb + lax.broadcasted_iota(jnp.int32, (qb, qb), 0)
            s = jnp.where(key_pos <= q_pos, s, NEG_BIG)
            sel_ref[rows, :] = _sortable(s)

        @pl.when(j > i)
        def _():
            sel_ref[rows, :] = _sortable(jnp.full((qb, qb), NEG_BIG, f32))

    def select(width):
        chosen = _select_topk(sel_ref[:width, :], n_sel, width, axis=0)
        key_pos = lax.broadcasted_iota(jnp.int32, (width, qb), 0)
        keep_ref[:width, :] = jnp.where(chosen & (key_pos <= q_pos), 1.0, 0.0)

    n_tier = DSA_TIERS if nkb % DSA_TIERS == 0 and (nkb // DSA_TIERS * qb) % kc == 0 else 1
    per_tier = nkb // n_tier
    for tier in range(n_tier):
        pl.when(i // per_tier == tier)(functools.partial(select, (tier + 1) * per_tier * qb))

    gain = qn_ref[...]
    for h in range(AT_HEADS):
        x = aq_ref[:, h * HEAD_DIM:(h + 1) * HEAD_DIM]
        ms = jnp.mean(x * x, axis=-1, keepdims=True)
        q_ref[h] = (x * lax.rsqrt(ms + NORM_EPS) * gain).astype(bf16)
    m_ref[...] = jnp.full(m_ref.shape, NEG_BIG, f32)
    l_ref[...] = jnp.zeros(l_ref.shape, f32)
    acc_ref[...] = jnp.zeros(acc_ref.shape, f32)

    def key_chunk(c, carry):
        rows = pl.ds(pl.multiple_of(c * kc, kc), kc)
        keep = keep_ref[rows, :] > 0.5
        for n in range(AT_KV_HEADS):
            kn = kb_ref[rows, n * HEAD_DIM:(n + 1) * HEAD_DIM]
            vnt = vt_ref[n, c]
            for g in range(AT_GROUP):
                h = n * AT_GROUP + g
                s = _dot_nt(kn, q_ref[h]) * (HEAD_DIM ** -0.5)
                s = jnp.where(keep, s, NEG_BIG)
                m_old = m_ref[h]
                m_new = jnp.maximum(m_old, jnp.max(s, axis=0, keepdims=True))
                alpha = jnp.exp(m_old - m_new)
                e = jnp.where(keep, jnp.exp(s - m_new), 0.0)
                l_ref[h] = alpha * l_ref[h] + jnp.sum(e, axis=0, keepdims=True)
                acc_ref[h] = alpha * acc_ref[h] + _dot(vnt, e.astype(bf16))
                m_ref[h] = m_new
        return carry

    lax.fori_loop(0, ((i + 1) * qb + kc - 1) // kc, key_chunk, 0)
    for h in range(AT_HEADS):
        o_ref[:, h * HEAD_DIM:(h + 1) * HEAD_DIM] = (acc_ref[h] / l_ref[h]).T.astype(o_ref.dtype)


def _dsa_prompt(p, k, v, ki, q_norm, layer, *, nb, t_len):
    n_sel = min(TOPK_MAX, t_len // 4)
    nq = t_len // Q_BLOCK
    kc = DSA_KEY_CHUNK if t_len % DSA_KEY_CHUNK == 0 else Q_BLOCK
    return pl.pallas_call(
        functools.partial(_dsa_prompt_kernel, t_len=t_len, n_sel=n_sel),
        grid=(nb, nq),
        in_specs=[pl.BlockSpec((Q_BLOCK, AT_W), lambda b, i: (b * nq + i, OFF_AQ // AT_W)),
                  pl.BlockSpec((Q_BLOCK, IDX_HEADS * IDX_DIM),
                               lambda b, i: (b * nq + i, OFF_AQI // (IDX_HEADS * IDX_DIM))),
                  pl.BlockSpec((Q_BLOCK, LANES), lambda b, i: (b * nq + i, OFF_AWI // LANES)),
                  pl.BlockSpec((t_len, KV_W), lambda b, i: (b, 0)),
                  pl.BlockSpec((t_len, KV_W), lambda b, i: (b, 0)),
                  pl.BlockSpec((t_len, IDX_DIM), lambda b, i: (b, 0)),
                  pl.BlockSpec((None, 1, HEAD_DIM), lambda b, i: (layer, 0, 0))],
        out_specs=pl.BlockSpec((Q_BLOCK, AT_W), lambda b, i: (b * nq + i, 0)),
        out_shape=jax.ShapeDtypeStruct((nb * t_len, AT_W), bf16),
        scratch_shapes=[pltpu.VMEM((t_len, Q_BLOCK), jnp.int32),
                        pltpu.VMEM((t_len, Q_BLOCK), f32),
                        pltpu.VMEM((t_len, KV_W), bf16),
                        pltpu.VMEM((AT_KV_HEADS, t_len // kc, HEAD_DIM, kc), bf16),
                        pltpu.VMEM((AT_HEADS, Q_BLOCK, HEAD_DIM), bf16),
                        pltpu.VMEM((AT_HEADS, 1, Q_BLOCK), f32),
                        pltpu.VMEM((AT_HEADS, 1, Q_BLOCK), f32),
                        pltpu.VMEM((AT_HEADS, HEAD_DIM, Q_BLOCK), f32)],
        compiler_params=_cparams(("parallel", "arbitrary")),
    )(p, p, p, k, v, ki, q_norm)


def _dsa_sample_score_kernel(pt_ref, aqi_ref, awi_ref, *refs, n_steps, npg, valid):
    page_refs, knew_ref, o_ref = refs[:npg], refs[npg], refs[npg + 1]
    g = pl.program_id(1)
    rws = SAMPLE_ROWS
    ps = page_refs[0].shape[0]
    qi = aqi_ref[...]
    wi_col = _wi_column(awi_ref[...])

    @pl.when(g < n_steps)
    def _():
        keys = jnp.concatenate([r[...] for r in page_refs], axis=0)
        o_ref[...] = _index_scores(qi, wi_col, keys)

    @pl.when(g == n_steps)
    def _():
        keys = jnp.concatenate([knew_ref[...], jnp.zeros((ps - rws, IDX_DIM), f32)], axis=0)
        s = _index_scores(qi, wi_col, keys)
        key_i = lax.broadcasted_iota(jnp.int32, (rws, ps), 1)
        q_i = lax.broadcasted_iota(jnp.int32, (rws, ps), 0)
        o_ref[:, :ps] = jnp.where((key_i <= q_i) & (key_i < valid), s, NEG_BIG)
        if npg > 1:
            o_ref[:, ps:] = jnp.full((rws, (npg - 1) * ps), NEG_BIG, f32)


def _page_specs(n_pages, npg, page_shape, layer):
    zeros = (0,) * len(page_shape)

    def spec(j):
        return pl.BlockSpec((None, None) + tuple(page_shape),
                            lambda b, g, pt: (layer, pt[b, jnp.minimum(g * npg + j, n_pages - 1)]) + zeros)
    return [spec(j) for j in range(npg)]


def _dsa_sample_scores(p, ki_new, cache_kidx, page_table, layer, *, nb, valid):
    n_pages = page_table.shape[1]
    ps = cache_kidx.shape[2]
    rws = SAMPLE_ROWS
    npg = min(SAMPLE_PAGES, n_pages)
    assert n_pages % npg == 0
    n_steps = n_pages // npg
    grid_spec = pltpu.PrefetchScalarGridSpec(
        num_scalar_prefetch=1,
        grid=(nb, n_steps + 1),
        in_specs=[pl.BlockSpec((rws, IDX_HEADS * IDX_DIM),
                               lambda b, g, pt: (b, OFF_AQI // (IDX_HEADS * IDX_DIM))),
                  pl.BlockSpec((rws, LANES), lambda b, g, pt: (b, OFF_AWI // LANES))]
        + _page_specs(n_pages, npg, (ps, IDX_DIM), layer)
        + [pl.BlockSpec((rws, IDX_DIM), lambda b, g, pt: (b, 0))],
        out_specs=pl.BlockSpec((None, rws, npg * ps), lambda b, g, pt: (b, 0, g)),
    )
    return pl.pallas_call(
        functools.partial(_dsa_sample_score_kernel, n_steps=n_steps, npg=npg, valid=valid),
        grid_spec=grid_spec,
        out_shape=jax.ShapeDtypeStruct((nb, rws, (n_steps + 1) * npg * ps), f32),
        compiler_params=_cparams(("parallel", "arbitrary")),
    )(page_table, p, p, *([cache_kidx] * npg), ki_new)


def _dsa_sample_select_kernel(s_ref, o_ref, *, n_sel, n_keys):
    chosen = _select_topk(_sortable(s_ref[...]), n_sel, n_keys)
    o_ref[...] = jnp.where(chosen & (s_ref[...] > 0.5 * NEG_BIG), 1.0, 0.0)


def _dsa_sample_select(scores, n_sel):
    nb, rws, n_keys = scores.shape
    spec = pl.BlockSpec((None, rws, n_keys), lambda b: (b, 0, 0))
    return pl.pallas_call(
        functools.partial(_dsa_sample_select_kernel, n_sel=n_sel, n_keys=n_keys),
        grid=(nb,),
        in_specs=[spec],
        out_specs=spec,
        out_shape=jax.ShapeDtypeStruct(scores.shape, f32),
        compiler_params=_cparams(("parallel",)),
    )(scores)


def _dsa_sample_attn_kernel(pt_ref, aq_ref, keep_ref, *refs, n_steps, npg):
    kpage_refs, vpage_refs = refs[:npg], refs[npg:2 * npg]
    knew_ref, vnew_ref, qn_ref, o_ref, q_ref, m_ref, l_ref, acc_ref = refs[2 * npg:]
    g = pl.program_id(1)
    rws = SAMPLE_ROWS
    ps = kpage_refs[0].shape[0] // AT_KV_HEADS

    @pl.when(g == 0)
    def _():
        m_ref[...] = jnp.full(m_ref.shape, NEG_BIG, f32)
        l_ref[...] = jnp.zeros(l_ref.shape, f32)
        acc_ref[...] = jnp.zeros(acc_ref.shape, f32)
        for n in range(AT_KV_HEADS):
            q_ref[n] = _q_heads(aq_ref[...], qn_ref[...], n).astype(bf16)

    def step(k_of, v_of, keep):
        keep4 = jnp.concatenate([keep] * AT_GROUP, axis=0) > 0.5
        for n in range(AT_KV_HEADS):
            s = _dot_nt(q_ref[n], k_of(n)) * (HEAD_DIM ** -0.5)
            s = jnp.where(keep4, s, NEG_BIG)
            m_old = m_ref[n]
            m_new = jnp.maximum(m_old, jnp.max(s, axis=1, keepdims=True))
            alpha = jnp.exp(m_old - m_new)
            e = jnp.where(keep4, jnp.exp(s - m_new), 0.0)
            l_ref[n] = alpha * l_ref[n] + jnp.sum(e, axis=1, keepdims=True)
            acc_ref[n] = alpha * acc_ref[n] + _dot(e.astype(bf16), v_of(n))
            m_ref[n] = m_new

    def page_head(pages, n):
        rows = pl.ds(n, ps, stride=AT_KV_HEADS)
        return jnp.concatenate([pg[rows, :].astype(bf16) for pg in pages], axis=0)

    def new_head(block, n):
        return block[:, n * HEAD_DIM:(n + 1) * HEAD_DIM].astype(bf16)

    @pl.when(g < n_steps)
    def _():
        step(functools.partial(page_head, kpage_refs), functools.partial(page_head, vpage_refs), keep_ref[...])

    @pl.when(g == n_steps)
    def _():
        pad = jnp.zeros((ps - rws, KV_W), f32)
        knew = jnp.concatenate([knew_ref[...], pad], axis=0)
        vnew = jnp.concatenate([vnew_ref[...], pad], axis=0)
        step(functools.partial(new_head, knew), functools.partial(new_head, vnew), keep_ref[:, :ps])
        for n in range(AT_KV_HEADS):
            o = acc_ref[n] / l_ref[n]
            for gq in range(AT_GROUP):
                h = n * AT_GROUP + gq
                o_ref[:, h * HEAD_DIM:(h + 1) * HEAD_DIM] = o[gq * rws:(gq + 1) * rws].astype(o_ref.dtype)


def _dsa_sample_attn(p, keep, cache_k, cache_v, k_new, v_new, q_norm, page_table, layer, *, nb):
    n_pages = page_table.shape[1]
    ps = cache_k.shape[2] // AT_KV_HEADS
    rws = SAMPLE_ROWS
    npg = min(SAMPLE_PAGES, n_pages)
    n_steps = n_pages // npg
    new =pl.BlockSpec((rws, KV_W), lambda b, g, pt: (b, 0))
    grid_spec = pltpu.PrefetchScalarGridSpec(
        num_scalar_prefetch=1,
        grid=(nb, n_steps + 1),
        in_specs=[pl.BlockSpec((rws, AT_W), lambda b, g, pt: (b, OFF_AQ // AT_W)),
                  pl.BlockSpec((None, rws, npg * ps), lambda b, g, pt: (b, 0, g))]
        + 2 * _page_specs(n_pages, npg, (ps * AT_KV_HEADS, HEAD_DIM), layer)
        + [new, new, pl.BlockSpec((None, 1, HEAD_DIM), lambda b, g, pt: (layer, 0, 0))],
        out_specs=pl.BlockSpec((rws, AT_W), lambda b, g, pt: (b, 0)),
        scratch_shapes=[pltpu.VMEM((AT_KV_HEADS, AT_GROUP * rws, HEAD_DIM), bf16),
                        pltpu.VMEM((AT_KV_HEADS, AT_GROUP * rws, 1), f32),
                        pltpu.VMEM((AT_KV_HEADS, AT_GROUP * rws, 1), f32),
                        pltpu.VMEM((AT_KV_HEADS, AT_GROUP * rws, HEAD_DIM), f32)],
    )
    return pl.pallas_call(
        functools.partial(_dsa_sample_attn_kernel, n_steps=n_steps, npg=npg),
        grid_spec=grid_spec,
        out_shape=jax.ShapeDtypeStruct((nb * rws, AT_W), bf16),
        compiler_params=_cparams(("parallel", "arbitrary")),
    )(page_table, p, keep, *([cache_k] * npg), *([cache_v] * npg), k_new, v_new, q_norm)


def _prep_weights(w_in, rwkv_mu, rwkv_w2, rwkv_a2, rwkv_g2, w_out, w_gate, w_up, w_down):
    depth, d, _ = w_in.shape
    hg, rw, at = jnp.split(w_in.astype(bf16), [HG_COLS, HG_COLS + RW_COLS], axis=-1)
    aq, ak, av, aqi, aki, awi = jnp.split(at, np.cumsum([AT_W, KV_W, KV_W, IDX_HEADS * IDX_DIM, IDX_DIM])
                                          .tolist(), axis=-1)
    r, k, v, lora = jnp.split(rw, [RW_W, 2 * RW_W, 3 * RW_W], axis=-1)
    z = lambda n: jnp.zeros((depth, d, n), bf16)
    w_in_p = jnp.concatenate(
        [hg, aq, aqi, ak, av, aki, awi, z(OFF_LORA - OFF_AWI - IDX_HEADS), lora, z(LORA_PAD - RW_LORA), r, k, v],
        axis=-1)
    assert w_in_p.shape[-1] == IN_PAD
    mu_r, mu_l = rwkv_mu[:, :3 * RW_W], rwkv_mu[:, 3 * RW_W:]
    mu_pad = jnp.concatenate([mu_r, mu_l, jnp.zeros((depth, LORA_PAD - RW_LORA), f32)], axis=-1)[:, None, :]
    zl = lambda n: jnp.zeros((depth, n, RW_W), bf16)
    w2p = jnp.concatenate([rwkv_w2.astype(bf16), zl(LORA_PAD - RW_DECAY_LORA)], axis=1)
    a2p = jnp.concatenate([zl(RW_DECAY_LORA), rwkv_a2.astype(bf16),
                           zl(LORA_PAD - RW_DECAY_LORA - RW_AAA_LORA)], axis=1)
    g2p = jnp.concatenate([zl(RW_DECAY_LORA + RW_AAA_LORA), rwkv_g2.astype(bf16), zl(LORA_PAD - RW_LORA)], axis=1)
    return dict(w_in=w_in_p, mu_pad=mu_pad, w2p=w2p, a2p=a2p, g2p=g2p,
                w_out=w_out.astype(bf16), w_gate=w_gate.astype(bf16), w_up=w_up.astype(bf16),
                w_down=w_down.astype(bf16))


def _shift_row(p_row):
    return jnp.concatenate([p_row[..., OFF_R:OFF_R + 3 * RW_W], p_row[..., OFF_LORA:OFF_LORA + RW_LORA]], axis=-1)


def _k_tile(f):
    half = f // 2
    return half if f % 2 == 0 and half % LANES == 0 else f


def _layer(x, layer, wts, *, nb, tp, t_valid, hg_s0, rw_s0, shift0, attend):
    h = _rmsnorm(x, wts["ln1"], layer)
    p = _matmul(h, wts["w_in"], layer, tm=1024, tn=512, tk=h.shape[1])
    valid = None if t_valid == tp else t_valid
    rows = min(256, tp)
    o_hg, hg_s = _gla(p, wts["lbs"], wts["hgrn_norm"], hg_s0, layer, nb=nb, tp=tp, rows=rows, valid=valid)
    o_rw, rw_s = _rwkv(p, shift0, rw_s0, wts, layer, nb=nb, tp=tp, t_valid=t_valid, rows=rows)
    k, v, ki = _kv_prep(p, wts["k_norm"], layer)
    o_at = attend(p, k, v, ki)
    mix = jnp.concatenate([o_hg, o_rw, o_at], axis=-1)
    x = _matmul(mix, wts["w_out"], layer, tm=1024, tn=512, tk=mix.shape[1], res=x)
    h2 = _rmsnorm(x, wts["ln2"], layer)
    act = _swiglu(h2, wts["w_gate"], wts["w_up"], layer, tm=1024, tn=256)
    x = _matmul(act, wts["w_down"], layer, tm=1024, tn=512, tk=_k_tile(act.shape[1]), res=x)
    shift = _shift_row(p.reshape(nb, tp, IN_PAD)[:, t_valid - 1])
    return x, (k, v, ki, hg_s, rw_s, shift)


def kernel(x_prompt, x_sample, cache_k, cache_v, cache_kidx, state_hgrn, state_rwkv, state_shift, page_table,
           ln1, w_in, hgrn_lb, hgrn_norm, rwkv_mu, rwkv_w0, rwkv_w2, rwkv_a0, rwkv_a2, rwkv_g2, rwkv_kk,
           rwkv_ka, rwkv_rk, rwkv_lnx_w, rwkv_lnx_b, q_norm, k_norm, w_out, ln2, w_gate, w_up, w_down):
    depth = w_in.shape[0]
    B, T, D = x_prompt.shape
    DB, DS, _ = x_sample.shape
    n_pool, page_size = cache_k.shape[1], cache_k.shape[2]
    ck = cache_k.reshape(depth, n_pool, page_size * AT_KV_HEADS, HEAD_DIM)
    cv = cache_v.reshape(depth, n_pool, page_size * AT_KV_HEADS, HEAD_DIM)
    past = page_table.shape[1] * page_size

    wts = _prep_weights(w_in, rwkv_mu, rwkv_w2, rwkv_a2, rwkv_g2, w_out, w_gate, w_up, w_down)
    lb_p = jax.nn.softmax(hgrn_lb.astype(f32), axis=0)
    row = lambda a: a.astype(f32).reshape(depth, 1, -1)
    wts.update(lbs=(jnp.cumsum(lb_p, axis=0) - lb_p[0:1])[:, None, :], hgrn_norm=row(hgrn_norm),
               ln1=row(ln1), ln2=row(ln2), w0=row(rwkv_w0), a0=row(rwkv_a0), kkw=row(rwkv_kk),
               kaw=row(rwkv_ka), rkw=row(rwkv_rk), lnw=row(rwkv_lnx_w), lnb=row(rwkv_lnx_b),
               q_norm=row(q_norm), k_norm=row(k_norm))

    xp = x_prompt.reshape(B * T, D)
    outs_p = []
    zeros_hg = jnp.zeros((B, HG_HEADS, HG_DK, HG_DV), f32)
    zeros_rw = jnp.zeros((B, RW_HEADS, RW_HD, RW_HD), f32)
    zeros_sh = jnp.zeros((B, RW_COLS), f32)
    for l in range(depth):
        attend = lambda p, k, v, ki, l=l: _dsa_prompt(p, k, v, ki, wts["q_norm"], l, nb=B, t_len=T)
        xp, st = _layer(xp, l, wts, nb=B, tp=T, t_valid=T, hg_s0=zeros_hg, rw_s0=zeros_rw, shift0=zeros_sh,
                        attend=attend)
        outs_p.append(st)

    TP = SAMPLE_ROWS
    xs = jnp.pad(x_sample, ((0, 0), (0, TP - DS), (0, 0))).reshape(DB * TP, D)
    n_sel_s = min(TOPK_MAX, (past + DS) // 4)
    outs_s = []
    for l in range(depth):
        def attend(p, k, v, ki, l=l):
            scores = _dsa_sample_scores(p, ki, cache_kidx, page_table, l, nb=DB, valid=DS)
            keep = _dsa_sample_select(scores, n_sel_s)
            return _dsa_sample_attn(p, keep, ck, cv, k, v, wts["q_norm"], page_table, l, nb=DB)

        xs, st = _layer(xs, l, wts, nb=DB, tp=TP, t_valid=DS, hg_s0=state_hgrn[l], rw_s0=state_rwkv[l],
                        shift0=state_shift[l], attend=attend)
        outs_s.append(st)

    def stack(outs, i):
        return jnp.stack([o[i] for o in outs])

    k_p = stack(outs_p, 0).reshape(depth, B, T, AT_KV_HEADS, HEAD_DIM)
    v_p = stack(outs_p, 1).reshape(depth, B, T, AT_KV_HEADS, HEAD_DIM)
    ki_p = stack(outs_p, 2).reshape(depth, B, T, IDX_DIM)
    cut = lambda a, w: a.reshape(depth, DB, TP, *w)[:, :, :DS]
    k_s = cut(stack(outs_s, 0), (AT_KV_HEADS, HEAD_DIM))
    v_s = cut(stack(outs_s, 1), (AT_KV_HEADS, HEAD_DIM))
    ki_s = cut(stack(outs_s, 2), (IDX_DIM,))
    y_p = xp.reshape(B, T, D)
    y_s = xs.reshape(DB, TP, D)[:, :DS]
    return (y_p, y_s, k_p, v_p, ki_p, stack(outs_p, 3), stack(outs_p, 4), stack(outs_p, 5),
            k_s, v_s, ki_s, stack(outs_s, 3), stack(outs_s, 4), stack(outs_s, 5))
```

```python
import functools

import jax
import jax.numpy as jnp
import numpy as np
from jax import lax
from jax.experimental import pallas as pl
from jax.experimental.pallas import tpu as pltpu

f32 = jnp.float32
bf16 = jnp.bfloat16

HG_HEADS, HG_DK, HG_DV = 8, 128, 128
HG_W = HG_HEADS * HG_DV
EXP_CLIP = 60.0
RW_HEADS, RW_HD = 16, 64
RW_W = RW_HEADS * RW_HD
RW_DECAY_LORA, RW_AAA_LORA, RW_GATE_LORA = 64, 64, 160
RW_LORA = RW_DECAY_LORA + RW_AAA_LORA + RW_GATE_LORA
RW_GN_EPS = 64e-5
RW_COLS = 3 * RW_W + RW_LORA
AT_HEADS, AT_KV_HEADS, HEAD_DIM = 16, 4, 128
AT_GROUP = AT_HEADS // AT_KV_HEADS
AT_W = AT_HEADS * HEAD_DIM
KV_W = AT_KV_HEADS * HEAD_DIM
IDX_HEADS, IDX_DIM = 8, 128
TOPK_MAX = 256
Q_BLOCK = 128
NEG_BIG = -1e30
LOG2_E = 1.4426950408889634
NORM_EPS = 1e-6
HG_COLS = 2 * HG_HEADS * HG_DK + 2 * HG_W
AT_COLS = AT_W + 2 * KV_W + IDX_HEADS * IDX_DIM + IDX_DIM + IDX_HEADS

LANES = 128
VMEM_LIMIT = 56 * 1024 * 1024
SAMPLE_ROWS = 16
GLA_CHUNK = 128
GLA_SUB = 16
GLA_SPAN = 60.0
RW_UNROLL = 32
DSA_TIERS = 4
COUNT_ROWS = 64
DSA_KEY_CHUNK = 256
SAMPLE_PAGES = 16

LORA_PAD = 512
OFF_HG = 0
OFF_AQ = HG_COLS
OFF_AQI = OFF_AQ + AT_W
OFF_AK = OFF_AQI + IDX_HEADS * IDX_DIM
OFF_AV = OFF_AK + KV_W
OFF_AKI = OFF_AV + KV_W
OFF_AWI = OFF_AKI + IDX_DIM
OFF_LORA = 8704
OFF_R = OFF_LORA + LORA_PAD
OFF_K = OFF_R + RW_W
OFF_V = OFF_K + RW_W
IN_PAD = OFF_V + RW_W


def _cparams(sem):
    return pltpu.CompilerParams(dimension_semantics=sem, vmem_limit_bytes=VMEM_LIMIT)


def _sigmoid(x):
    return jax.nn.sigmoid(x)


def _dot(a, b, precision=None):
    return jnp.dot(a, b, preferred_element_type=f32, precision=precision)


def _dot_nt(a, b, precision=None):
    return lax.dot_general(a, b, (((1,), (1,)), ((), ())), preferred_element_type=f32, precision=precision)


def _dot_tn(a, b, precision=None):
    return lax.dot_general(a, b, (((0,), (0,)), ((), ())), preferred_element_type=f32, precision=precision)


HI = lax.Precision.HIGHEST


def _split_bf16(x):
    hi = x.astype(bf16)
    return hi, (x - hi.astype(f32)).astype(bf16)


def _rmsnorm_kernel(x_ref, g_ref, o_ref):
    x = x_ref[...]
    ms = jnp.mean(x * x, axis=-1, keepdims=True)
    o_ref[...] = (x * lax.rsqrt(ms + NORM_EPS) * g_ref[...]).astype(o_ref.dtype)


def _rmsnorm(x, g, layer):
    m, d = x.shape
    tr = min(256, m)
    return pl.pallas_call(
        _rmsnorm_kernel,
        grid=(m // tr,),
        in_specs=[pl.BlockSpec((tr, d), lambda i: (i, 0)),
                  pl.BlockSpec((None, 1, d), lambda i: (layer, 0, 0))],
        out_specs=pl.BlockSpec((tr, d), lambda i: (i, 0)),
        out_shape=jax.ShapeDtypeStruct((m, d), bf16),
        compiler_params=_cparams(("parallel",)),
    )(x, g)


def _mm_kernel(*refs, nk, has_res):
    if has_res:
        a_ref, b_ref, r_ref, o_ref = refs[:4]
    else:
        a_ref, b_ref, o_ref = refs[:3]
        r_ref = None
    part = _dot(a_ref[...], b_ref[...])
    if nk == 1:
        o_ref[...] = (part + r_ref[...]) if has_res else part
        return
    acc_ref = refs[-1]
    k = pl.program_id(2)

    @pl.when(k == 0)
    def _():
        acc_ref[...] = part

    @pl.when(k > 0)
    def _():
        acc_ref[...] += part

    @pl.when(k == nk - 1)
    def _():
        o_ref[...] = (acc_ref[...] + r_ref[...]) if has_res else acc_ref[...]


def _matmul(a, w, layer, *, tm, tn, tk, res=None):
    m, kdim = a.shape
    n = w.shape[-1]
    tm = min(tm, m)
    tn = min(tn, n)
    nk = kdim // tk
    assert m % tm == 0 and n % tn == 0 and kdim % tk == 0
    in_specs = [pl.BlockSpec((tm, tk), lambda i, j, k: (i, k)),
                pl.BlockSpec((None, tk, tn), lambda i, j, k: (layer, k, j))]
    args = [a, w]
    if res is not None:
        in_specs.append(pl.BlockSpec((tm, tn), lambda i, j, k: (i, j)))
        args.append(res)
    scratch = [pltpu.VMEM((tm, tn), f32)] if nk > 1 else []
    return pl.pallas_call(
        functools.partial(_mm_kernel, nk=nk, has_res=res is not None),
        grid=(m // tm, n // tn, nk),
        in_specs=in_specs,
        out_specs=pl.BlockSpec((tm, tn), lambda i, j, k: (i, j)),
        out_shape=jax.ShapeDtypeStruct((m, n), f32),
        scratch_shapes=scratch,
        compiler_params=_cparams(("parallel", "parallel", "arbitrary")),
    )(*args)


def _swiglu_kernel(a_ref, wg_ref, wu_ref, o_ref):
    a = a_ref[...]
    g = _dot(a, wg_ref[...])
    u = _dot(a, wu_ref[...])
    o_ref[...] = (g * _sigmoid(g) * u).astype(o_ref.dtype)


def _swiglu(a, wg, wu, layer, *, tm, tn):
    m, d = a.shape
    n = wg.shape[-1]
    tm = min(tm, m)
    assert m % tm == 0 and n % tn == 0
    wspec = pl.BlockSpec((None, d, tn), lambda i, j: (layer, 0, j))
    return pl.pallas_call(
        _swiglu_kernel,
        grid=(m // tm, n // tn),
        in_specs=[pl.BlockSpec((tm, d), lambda i, j: (i, 0)), wspec, wspec],
        out_specs=pl.BlockSpec((tm, tn), lambda i, j: (i, j)),
        out_shape=jax.ShapeDtypeStruct((m, n), bf16),
        compiler_params=_cparams(("parallel", "parallel")),
    )(a, wg, wu)


def _gla_kernel(pq_ref, pf_ref, pi_ref, pg_ref, lb_ref, g_ref, s0_ref, o_ref, sout_ref, st_ref,
                *, rows, valid):
    i = pl.program_id(2)
    chunk, sub = GLA_CHUNK, GLA_SUB
    live_rows = min(rows, chunk)

    @pl.when(i == 0)
    def _():
        st_ref[...] = s0_ref[...].T

    lb = lb_ref[...]
    gain = g_ref[...]
    n_live = live_rows if valid is None else min(valid, live_rows)
    nsb = -(-n_live // sub)
    crow = lax.broadcasted_iota(jnp.int32, (chunk, 1), 0)
    tri = (lax.broadcasted_iota(jnp.int32, (chunk, chunk), 0)
           >= lax.broadcasted_iota(jnp.int32, (chunk, chunk), 1)).astype(f32)
    sub_row = lax.broadcasted_iota(jnp.int32, (sub, 1), 0)

    def padded(x):
        if live_rows == chunk:
            return x
        return jnp.concatenate([x, jnp.zeros((chunk - live_rows, x.shape[1]), x.dtype)], axis=0)

    def one_chunk(c, carry):
        rs = pl.ds(pl.multiple_of(c * live_rows, live_rows), live_rows)
        pq = padded(pq_ref[rs, :])
        fr = padded(pf_ref[rs, :])
        v = padded(pi_ref[rs, :])
        q = pq * _sigmoid(pq) * (HG_DK ** -0.5)
        log_sig = jnp.minimum(fr, 0.0) - jnp.log1p(jnp.exp(-jnp.abs(fr)))
        log_f = log_sig + jnp.log1p(lb * jnp.exp(jnp.minimum(-fr, EXP_CLIP)))
        k = (1.0 - lb) * _sigmoid(-fr)
        if valid is not None or live_rows != chunk:
            live = crow < live_rows
            if valid is not None:
                live = live & (i * rows + c * live_rows + crow < valid)
            log_f = jnp.where(live, log_f, 0.0)
            k = jnp.where(live, k, 0.0)
        b = _dot(tri, log_f, precision=HI)
        st = st_ref[...]
        o = _dot_nt((q * jnp.exp(b)).astype(bf16), st.astype(bf16))
        v16 = v.astype(bf16)
        zero_row = jnp.zeros((1, HG_DK), f32)
        ref_rows = [zero_row if ib == 0 else b[ib * sub - 1:ib * sub] for ib in range(nsb)]
        tail = chunk - nsb * sub

        def scores(with_diagonal):
            a_rows = []
            for ib in range(nsb):
                lo = ib * sub
                hi = lo + sub if with_diagonal else lo
                if hi == 0:
                    a_rows.append(jnp.zeros((sub, chunk), f32))
                    continue
                qt = (q[lo:lo + sub] * jnp.exp(b[lo:lo + sub] - ref_rows[ib])).astype(bf16)
                grow = jnp.minimum(ref_rows[ib] - b, GLA_SPAN if with_diagonal else 0.0)
                kt = jnp.where(crow < hi, k * jnp.exp(grow), 0.0).astype(bf16)
                a = _dot_nt(qt, kt)
                if with_diagonal:
                    key = lax.broadcasted_iota(jnp.int32, (sub, chunk), 1)
                    a = jnp.where(key <= lo + sub_row, a, 0.0)
                a_rows.append(a)
            if tail:
                a_rows.append(jnp.zeros((tail, chunk), f32))
            return jnp.concatenate(a_rows, axis=0)

        def factored():
            return scores(True), jnp.zeros((chunk, HG_DV), f32)

        def term_by_term():
            d_rows = []
            for ib in range(nsb):
                lo = ib * sub
                q_i, b_i = q[lo:lo + sub], b[lo:lo + sub]
                d_i = jnp.zeros((sub, HG_DV), f32)
                for s in range(sub):
                    gs = lo + s
                    term = q_i * jnp.exp(jnp.minimum(b_i - b[gs:gs + 1], 0.0)) * k[gs:gs + 1]
                    a_col = jnp.sum(term, axis=1, keepdims=True)
                    d_i = d_i + jnp.where(sub_row >= s, a_col, 0.0) * v[gs:gs + 1]
                d_rows.append(d_i)
            if tail:
                d_rows.append(jnp.zeros((tail, HG_DV), f32))
            return scores(False), jnp.concatenate(d_rows, axis=0)

        span = ref_rows[0] - b[sub - 1:sub]
        for ib in range(1, nsb):
            span = jnp.maximum(span, ref_rows[ib] - b[(ib + 1) * sub - 1:(ib + 1) * sub])
        a_mat, d_mat = lax.cond(jnp.max(span) <= GLA_SPAN, factored, term_by_term)
        o = o + _dot(a_mat.astype(bf16), v16) + d_mat
        b_last = b[chunk - 1:chunk]
        kd = (k * jnp.exp(b_last - b)).astype(bf16)
        st_ref[...] = st * jnp.exp(b_last) + _dot_tn(v16, kd)
        o = o[:live_rows]
        ms = jnp.mean(o * o, axis=-1, keepdims=True)
        y = o * lax.rsqrt(ms + NORM_EPS) * gain * _sigmoid(pg_ref[rs, :])
        o_ref[rs, :] = y.astype(o_ref.dtype)
        return carry

    lax.fori_loop(0, rows // live_rows, one_chunk, 0)

    @pl.when(i == pl.num_programs(2) - 1)
    def _():
        sout_ref[...] = st_ref[...].T


def _gla(p, lb, gain, s0, layer, *, nb, tp, rows, valid):
    nblk = tp // rows
    H = HG_HEADS

    def pspec(part):
        return pl.BlockSpec((rows, LANES), lambda b, h, i: (b * nblk + i, part * H + h))

    return pl.pallas_call(
        functools.partial(_gla_kernel, rows=rows, valid=valid),
        grid=(nb, H, nblk),
        in_specs=[pspec(0), pspec(1), pspec(2), pspec(3),
                  pl.BlockSpec((None, 1, LANES), lambda b, h, i: (layer, 0, h)),
                  pl.BlockSpec((None, 1, LANES), lambda b, h, i: (layer, 0, h)),
                  pl.BlockSpec((None, None, HG_DK, HG_DV), lambda b, h, i: (b, h, 0, 0))],
        out_specs=[pl.BlockSpec((rows, LANES), lambda b, h, i: (b * nblk + i, h)),
                   pl.BlockSpec((None, None, HG_DK, HG_DV), lambda b, h, i: (b, h, 0, 0))],
        out_shape=[jax.ShapeDtypeStruct((nb * tp, HG_W), bf16),
                   jax.ShapeDtypeStruct((nb, H, HG_DK, HG_DV), f32)],
        scratch_shapes=[pltpu.VMEM((HG_DV, HG_DK), f32)],
        compiler_params=_cparams(("parallel", "parallel", "arbitrary")),
    )(p, p, p, p, lb, gain, s0)


def _head_sum(x, hm):
    cols = [_dot(x[:, c * LANES:(c + 1) * LANES], hm, precision=HI) for c in range(RW_W // LANES)]
    return jnp.concatenate(cols, axis=1)


def _head_matrix():
    r = lax.broadcasted_iota(jnp.int32, (LANES, LANES), 0) // RW_HD
    c = lax.broadcasted_iota(jnp.int32, (LANES, LANES), 1) // RW_HD
    return (r == c).astype(f32)


def _rw_pre_kernel(pr_ref, pk_ref, pv_ref, pl_ref, sr_ref, sk_ref, sv_ref, sl_ref,
                   mr_ref, mk_ref, mv_ref, ml_ref, w0_ref, w2_ref, a0_ref, a2_ref, g2_ref,
                   kkw_ref, kaw_ref, rkw_ref,
                   r_o, w_o, k_o, v_o, nkk_o, b_o, bonus_o, g_o,
                   cr_ref, ck_ref, cv_ref, cl_ref, *, rows, valid):
    i = pl.program_id(1)

    @pl.when(i == 0)
    def _():
        cr_ref[...] = sr_ref[...]
        ck_ref[...] = sk_ref[...]
        cv_ref[...] = sv_ref[...]
        cl_ref[...] = sl_ref[...]

    def mixed(p_ref, c_ref, m_ref):
        cur = p_ref[...]
        rolled = pltpu.roll(cur, 1, axis=0)
        rowid = lax.broadcasted_iota(jnp.int32, cur.shape, 0)
        prev = jnp.where(rowid == 0, c_ref[...], rolled)
        c_ref[...] = cur[rows - 1:rows, :]
        return cur + (prev - cur) * m_ref[...]

    r = mixed(pr_ref, cr_ref, mr_ref)
    k = mixed(pk_ref, ck_ref, mk_ref)
    v = mixed(pv_ref, cv_ref, mv_ref)
    xl = mixed(pl_ref, cl_ref, ml_ref)

    zw = w0_ref[...] + _dot(jnp.tanh(xl).astype(bf16), w2_ref[...])
    w = jnp.minimum(zw, 0.0) - jnp.log1p(jnp.exp(-jnp.abs(zw))) - 0.5
    decay = jnp.exp(-jnp.exp(w))
    a = _sigmoid(a0_ref[...] + _dot(xl.astype(bf16), a2_ref[...]))
    g = _dot(_sigmoid(xl).astype(bf16), g2_ref[...])

    hm = _head_matrix()
    kk = k * kkw_ref[...]
    kk = kk * lax.rsqrt(jnp.maximum(_head_sum(kk * kk, hm), 1e-24))
    k2 = k * (1.0 + (a - 1.0) * kaw_ref[...])
    bonus = _head_sum(r * k2 * rkw_ref[...], hm) * v
    nkk = -kk
    bb = kk * a
    if valid is not None:
        tok = i * rows + lax.broadcasted_iota(jnp.int32, (rows, 1), 0)
        live = tok < valid
        decay = jnp.where(live, decay, 1.0)
        k2 = jnp.where(live, k2, 0.0)
        v = jnp.where(live, v, 0.0)
        nkk = jnp.where(live, nkk, 0.0)
        bb = jnp.where(live, bb, 0.0)
    r_o[...] = r
    w_o[...] = decay
    k_o[...] = k2
    v_o[...] = v
    nkk_o[...] = nkk
    b_o[...] = bb
    bonus_o[...] = bonus
    g_o[...] = g


def _rw_pre(p, shift_pad, mu_pad, w0, w2p, a0, a2p, g2p, kkw, kaw, rkw, layer, *, nb, tp, rows, valid):
    nblk = tp // rows
    W = RW_W

    def pspec(off, width):
        return pl.BlockSpec((rows, width), lambda b, i: (b * nblk + i, off // width))

    def sspec(off, width):
        return pl.BlockSpec((None, 1, width), lambda b, i: (b, 0, off // width))

    def mspec(off, width):
        return pl.BlockSpec((None, 1, width), lambda b, i: (layer, 0, off // width))

    def vec():
        return pl.BlockSpec((None, 1, W), lambda b, i: (layer, 0, 0))

    def lora():
        return pl.BlockSpec((None, LORA_PAD, W), lambda b, i: (layer, 0, 0))

    out_spec = pl.BlockSpec((rows, W), lambda b, i: (b * nblk + i, 0))
    out_shape = jax.ShapeDtypeStruct((nb * tp, W), f32)
    return pl.pallas_call(
        functools.partial(_rw_pre_kernel, rows=rows, valid=valid),
        grid=(nb, nblk),
        in_specs=[pspec(OFF_R, W), pspec(OFF_K, W), pspec(OFF_V, W), pspec(OFF_LORA, LORA_PAD),
                  sspec(0, W), sspec(W, W), sspec(2 * W, W), sspec(3 * W, LORA_PAD),
                  mspec(0, W), mspec(W, W), mspec(2 * W, W), mspec(3 * W, LORA_PAD),
                  vec(), lora(), vec(), lora(), lora(), vec(), vec(), vec()],
        out_specs=[out_spec] * 8,
        out_shape=[out_shape] * 8,
        scratch_shapes=[pltpu.VMEM((1, W), f32), pltpu.VMEM((1, W), f32), pltpu.VMEM((1, W), f32),
                        pltpu.VMEM((1, LORA_PAD), f32)],
        compiler_params=_cparams(("parallel", "arbitrary")),
    )(p, p, p, p, shift_pad, shift_pad, shift_pad, shift_pad, mu_pad, mu_pad, mu_pad, mu_pad,
      w0, w2p, a0, a2p, g2p, kkw, kaw, rkw)


def _rw_scan_kernel(r_ref, w_ref, k_ref, b_ref, nkk_ref, v_ref, s0_ref, y_ref, sout_ref, s_ref,
                    *, tt, ki_n, fold):
    i = pl.program_id(0)

    @pl.when(i == 0)
    def _():
        s_ref[...] = s0_ref[...]

    def lane_total(x):
        return x + pltpu.roll(x, LANES // 2, axis=1) if fold else x

    half = RW_HD // 2
    halves = (slice(0, half), slice(half, RW_HD))
    zero = jnp.zeros((half, LANES), f32)

    def row(ref, t, ki):
        return ref[t, pl.ds(ki, 1), :]

    def first_sa(vr):
        acc = [zero, zero]
        for ki in range(ki_n):
            acc[ki % 2] = acc[ki % 2] + s_ref[ki, vr, :] * row(nkk_ref, 0, ki)
        return lane_total(acc[0] + acc[1])

    def token(t, sa_pair):
        nxt = jnp.minimum(t + 1, tt - 1)
        sa_next = []
        for vr, sa in zip(halves, sa_pair):
            vt = v_ref[t, vr, :]

            def k_block(kb, carry):
                y0, y1, a0, a1 = carry
                for u in range(RW_UNROLL):
                    ki = kb * RW_UNROLL + u
                    s_new = (s_ref[ki, vr, :] * row(w_ref, t, ki) + sa * row(b_ref, t, ki)
                             + vt * row(k_ref, t, ki))
                    s_ref[ki, vr, :] = s_new
                    if u % 2 == 0:
                        y0 = y0 + s_new * row(r_ref, t, ki)
                        a0 = a0 + s_new * row(nkk_ref, nxt, ki)
                    else:
                        y1 = y1 + s_new * row(r_ref, t, ki)
                        a1 = a1 + s_new * row(nkk_ref, nxt, ki)
                return y0, y1, a0, a1

            n_blocks = ki_n // RW_UNROLL
            if n_blocks == 1:
                y0, y1, a0, a1 = k_block(0, (zero, zero, zero, zero))
            else:
                y0, y1, a0, a1 = lax.fori_loop(0, n_blocks, k_block, (zero, zero, zero, zero))
            y_ref[t, vr, :] = lane_total(y0 + y1)
            sa_next.append(lane_total(a0 + a1))
        return tuple(sa_next)

    lax.fori_loop(0, tt, token, tuple(first_sa(vr) for vr in halves))

    @pl.when(i == pl.num_programs(0) - 1)
    def _():
        sout_ref[...] = s_ref[...]


def _rw_scan(rT, wT, kT, bT, nkkT, vT, s0T, *, tt):
    t_len, ki_n, _ = rT.shape
    fold = ki_n * 2 == RW_HD
    assert fold or ki_n == RW_HD
    tt = min(tt, t_len)
    assert t_len % tt == 0 and ki_n % RW_UNROLL == 0
    op = pl.BlockSpec((tt, ki_n, LANES), lambda i: (i, 0, 0))
    vs = pl.BlockSpec((tt, RW_HD, LANES), lambda i: (i, 0, 0))
    ss = pl.BlockSpec((ki_n, RW_HD, LANES), lambda i: (0, 0, 0))
    return pl.pallas_call(
        functools.partial(_rw_scan_kernel, tt=tt, ki_n=ki_n, fold=fold),
        grid=(t_len // tt,),
        in_specs=[op, op, op, op, op, vs, ss],
        out_specs=[vs, ss],
        out_shape=[jax.ShapeDtypeStruct((t_len, RW_HD, LANES), f32),
                   jax.ShapeDtypeStruct((ki_n, RW_HD, LANES), f32)],
        scratch_shapes=[pltpu.VMEM((ki_n, RW_HD, LANES), f32)],
        compiler_params=_cparams(("arbitrary",)),
    )(rT, wT, kT, bT, nkkT, vT, s0T)


def _rw_post_kernel(y_ref, bonus_ref, g_ref, lw_ref, lb_ref, o_ref):
    hm = _head_matrix()
    y = y_ref[...]
    mean = _head_sum(y, hm) * (1.0 / RW_HD)
    d = y - mean
    var = _head_sum(d * d, hm) * (1.0 / RW_HD)
    yn = d * lax.rsqrt(var + RW_GN_EPS) * lw_ref[...] + lb_ref[...]
    o_ref[...] = ((yn + bonus_ref[...]) * g_ref[...]).astype(o_ref.dtype)


def _rw_post(y, bonus, g, lnw, lnb, layer):
    m = y.shape[0]
    tr = min(512, m)
    spec = pl.BlockSpec((tr, RW_W), lambda i: (i, 0))
    vec = pl.BlockSpec((None, 1, RW_W), lambda i: (layer, 0, 0))
    return pl.pallas_call(
        _rw_post_kernel,
        grid=(m // tr,),
        in_specs=[spec, spec, spec, vec, vec],
        out_specs=spec,
        out_shape=jax.ShapeDtypeStruct((m, RW_W), bf16),
        compiler_params=_cparams(("parallel",)),
    )(y, bonus, g, lnw, lnb)


def _rwkv(p, shift0, s0, wts, layer, *, nb, tp, t_valid, rows):
    H, N = RW_HEADS, RW_HD
    kh = LANES // (nb * H)
    assert kh in (1, 2) and kh * nb * H == LANES
    ki_n = N // kh
    shift_pad = jnp.pad(shift0, ((0, 0), (0, 3 * RW_W + LORA_PAD - RW_COLS)))[:, None, :]
    valid = None if t_valid == tp else t_valid
    r, w, k, v, nkk, bb, bonus, g = _rw_pre(
        p, shift_pad, wts["mu_pad"], wts["w0"], wts["w2p"], wts["a0"], wts["a2p"], wts["g2p"],
        wts["kkw"], wts["kaw"], wts["rkw"], layer, nb=nb, tp=tp, rows=rows, valid=valid)

    def key_lanes(x):
        x = x.reshape(nb, tp, H, kh, ki_n)[:, :t_valid]
        return x.transpose(1, 4, 3, 0, 2).reshape(t_valid, ki_n, LANES)

    vT = jnp.broadcast_to(v.reshape(nb, tp, H, 1, N)[:, :t_valid], (nb, t_valid, H, kh, N))
    vT = vT.transpose(1, 4, 3, 0, 2).reshape(t_valid, N, LANES)
    s0T = s0.reshape(nb, H, N, kh, ki_n).transpose(4, 2, 3, 0, 1).reshape(ki_n, N, LANES)
    yT, sT = _rw_scan(key_lanes(r), key_lanes(w), key_lanes(k), key_lanes(bb), key_lanes(nkk), vT, s0T,
                      tt=64)
    y = yT[:, :, :nb * H].reshape(t_valid, N, nb, H).transpose(2, 0, 3, 1).reshape(nb, t_valid, RW_W)
    if t_valid != tp:
        y = jnp.pad(y, ((0, 0), (0, tp - t_valid), (0, 0)))
    y = y.reshape(nb * tp, RW_W)
    s_out = sT.reshape(ki_n, N, kh, nb, H).transpose(3, 4, 1, 2, 0).reshape(nb, H, N, N)
    o = _rw_post(y, bonus, g, wts["lnw"], wts["lnb"], layer)
    return o, s_out


def _kv_prep_kernel(ak_ref, av_ref, aki_ref, kn_ref, k_o, v_o, ki_o):
    gain = kn_ref[...]
    for n in range(AT_KV_HEADS):
        x = ak_ref[:, n * HEAD_DIM:(n + 1) * HEAD_DIM]
        ms = jnp.mean(x * x, axis=-1, keepdims=True)
        k_o[:, n * HEAD_DIM:(n + 1) * HEAD_DIM] = x * lax.rsqrt(ms + NORM_EPS) * gain
    v_o[...] = av_ref[...]
    ki_o[...] = aki_ref[...]


def _kv_prep(p, k_norm, layer):
    m = p.shape[0]
    tr = min(512, m)
    return pl.pallas_call(
        _kv_prep_kernel,
        grid=(m // tr,),
        in_specs=[pl.BlockSpec((tr, KV_W), lambda i: (i, OFF_AK // KV_W)),
                  pl.BlockSpec((tr, KV_W), lambda i: (i, OFF_AV // KV_W)),
                  pl.BlockSpec((tr, IDX_DIM), lambda i: (i, OFF_AKI // IDX_DIM)),
                  pl.BlockSpec((None, 1, HEAD_DIM), lambda i: (layer, 0, 0))],
        out_specs=[pl.BlockSpec((tr, KV_W), lambda i: (i, 0)),
                   pl.BlockSpec((tr, KV_W), lambda i: (i, 0)),
                   pl.BlockSpec((tr, IDX_DIM), lambda i: (i, 0))],
        out_shape=[jax.ShapeDtypeStruct((m, KV_W), f32), jax.ShapeDtypeStruct((m, KV_W), f32),
                   jax.ShapeDtypeStruct((m, IDX_DIM), f32)],
        compiler_params=_cparams(("parallel",)),
    )(p, p, p, k_norm)


def _index_scores(qi, wi_col, keys):
    rws = qi.shape[0]
    qs = jnp.concatenate([qi[:, h * IDX_DIM:(h + 1) * IDX_DIM] for h in range(IDX_HEADS)], axis=0)
    d = jnp.maximum(_dot_nt(qs, keys, precision=HI), 0.0) * wi_col
    s = d[0:rws]
    for h in range(1, IDX_HEADS):
        s = s + d[h * rws:(h + 1) * rws]
    return s


def _wi_column(awi):
    scale = IDX_HEADS ** -0.5 * IDX_DIM ** -0.5
    return jnp.concatenate([awi[:, h:h + 1] for h in range(IDX_HEADS)], axis=0) * scale


def _sortable(score):
    bits = lax.bitcast_convert_type(score, jnp.int32)
    key = jnp.where(bits < 0, bits ^ jnp.int32(0x7FFFFFFF), bits)
    return jnp.where(score == 0.0, 0, key)


def _select_topk(skey, n_sel, n_keys, axis=1):
    one = tuple(1 if a == axis else s for a, s in enumerate(skey.shape))
    nsel = jnp.float32(n_sel)
    int_min = jnp.int32(-2 ** 31)

    def _count(mask):
        ones = mask.astype(f32)
        if axis == 0 and ones.shape[0] % COUNT_ROWS == 0 and ones.shape[0] > COUNT_ROWS:
            ones = jnp.sum(ones.reshape(-1, COUNT_ROWS, ones.shape[1]), axis=0)
        return jnp.sum(ones, axis=axis, keepdims=True)

    zero = jnp.zeros(one, jnp.int32)
    cand = jnp.where(_count(skey >= zero) >= nsel, zero, zero + int_min)

    def bit_step(it, cand):
        trial = cand + jnp.left_shift(jnp.int32(1), 30 - it)
        return jnp.where(_count(skey >= trial) >= nsel, trial, cand)

    tau = lax.fori_loop(0, 31, bit_step, cand)
    gt = skey > tau
    eq = skey == tau
    need = nsel - _count(gt)
    idx = lax.broadcasted_iota(jnp.int32, skey.shape, axis)
    nbits = int(n_keys).bit_length()

    def idx_step(it, x):
        trial = x + jnp.left_shift(jnp.int32(1), nbits - 1 - it)
        ok = (trial <= n_keys) & (_count(eq & (idx < trial)) < need)
        return jnp.where(ok, trial, x)

    surplus = jnp.max(_count(eq) - need) > 0.0
    x = lax.cond(surplus, lambda: lax.fori_loop(0, nbits, idx_step, zero), lambda: zero + n_keys)
    return gt | (eq & (idx <= x))


def _q_heads(aq, gain, n):
    outs = []
    for g in range(AT_GROUP):
        h = n * AT_GROUP + g
        x = aq[:, h * HEAD_DIM:(h + 1) * HEAD_DIM]
        ms = jnp.mean(x * x, axis=-1, keepdims=True)
        outs.append(x * lax.rsqrt(ms + NORM_EPS) * gain)
    return jnp.concatenate(outs, axis=0)


def _dsa_prompt_kernel(aq_ref, aqi_ref, awi_ref, k_ref, v_ref, ki_ref, qn_ref, o_ref,
                       sel_ref, keep_ref, kb_ref, vt_ref, q_ref, m_ref, l_ref, acc_ref, *, t_len, n_sel):
    i = pl.program_id(1)
    qb = Q_BLOCK
    nkb = t_len // qb
    kc = DSA_KEY_CHUNK if t_len % DSA_KEY_CHUNK == 0 else qb

    @pl.when(i == 0)
    def _():
        kb_ref[...] = k_ref[...].astype(bf16)
        for n in range(AT_KV_HEADS):
            for j in range(nkb):
                tile = v_ref[j * qb:(j + 1) * qb, n * HEAD_DIM:(n + 1) * HEAD_DIM].T.astype(bf16)
                lo = (j * qb) % kc
                vt_ref[n, (j * qb) // kc, :, lo:lo + qb] = tile

    qi = aqi_ref[...]
    qs = jnp.concatenate([qi[:, h * IDX_DIM:(h + 1) * IDX_DIM] for h in range(IDX_HEADS)], axis=0)
    qs_hi, qs_lo = _split_bf16(qs)
    wi_t = awi_ref[...].T
    wi_row = jnp.concatenate([wi_t[h:h + 1, :] for h in range(IDX_HEADS)], axis=1)
    wi_row = wi_row * (IDX_HEADS ** -0.5 * IDX_DIM ** -0.5)
    q_pos = i * qb + lax.broadcasted_iota(jnp.int32, (1, qb), 1)

    for j in range(nkb):
        rows = slice(j * qb, (j + 1) * qb)

        @pl.when(j <= i)
        def _():
            k_hi, k_lo = _split_bf16(ki_ref[rows, :])
            d = _dot_nt(k_hi, qs_hi) + (_dot_nt(k_hi, qs_lo) + _dot_nt(k_lo, qs_hi))
            d = jnp.maximum(d, 0.0) * wi_row
            s = d[:, 0:qb]
            for h in range(1, IDX_HEADS):
                s = s + d[:, h * qb:(h + 1) * qb]
            key_pos = j * qb + lax.broadcasted_iota(jnp.int32, (qb, qb), 0)
            s = jnp.where(key_pos <= q_pos, s, NEG_BIG)
            sel_ref[rows, :] = _sortable(s)

        @pl.when(j > i)
        def _():
            sel_ref[rows, :] = _sortable(jnp.full((qb, qb), NEG_BIG, f32))

    def select(width):
        chosen = _select_topk(sel_ref[:width, :], n_sel, width, axis=0)
        key_pos = lax.broadcasted_iota(jnp.int32, (width, qb), 0)
        keep_ref[:width, :] = jnp.where(chosen & (key_pos <= q_pos), 1.0, 0.0)

    n_tier = DSA_TIERS if nkb % DSA_TIERS == 0 and (nkb // DSA_TIERS * qb) % kc == 0 else 1
    per_tier = nkb // n_tier
    for tier in range(n_tier):
        pl.when(i // per_tier == tier)(functools.partial(select, (tier + 1) * per_tier * qb))

    gain = qn_ref[...]
    for h in range(AT_HEADS):
        x = aq_ref[:, h * HEAD_DIM:(h + 1) * HEAD_DIM]
        ms = jnp.mean(x * x, axis=-1, keepdims=True)
        q_ref[h] = (x * lax.rsqrt(ms + NORM_EPS) * gain).astype(bf16)
    m_ref[...] = jnp.full(m_ref.shape, NEG_BIG, f32)
    l_ref[...] = jnp.zeros(l_ref.shape, f32)
    acc_ref[...] = jnp.zeros(acc_ref.shape, f32)

    def key_chunk(c, carry):
        rows = pl.ds(pl.multiple_of(c * kc, kc), kc)
        keep = keep_ref[rows, :] > 0.5
        for n in range(AT_KV_HEADS):
            kn = kb_ref[rows, n * HEAD_DIM:(n + 1) * HEAD_DIM]
            vnt = vt_ref[n, c]
            for g in range(AT_GROUP):
                h = n * AT_GROUP + g
                s = _dot_nt(kn, q_ref[h]) * (HEAD_DIM ** -0.5 * LOG2_E)
                s = jnp.where(keep, s, NEG_BIG)
                m_old = m_ref[h]
                m_new = jnp.maximum(m_old, jnp.max(s, axis=0, keepdims=True))
                alpha = jnp.exp2(m_old - m_new)
                e = jnp.exp2(s - m_new)
                l_ref[h] = alpha * l_ref[h] + jnp.sum(e, axis=0, keepdims=True)
                acc_ref[h] = alpha * acc_ref[h] + _dot(vnt, e.astype(bf16))
                m_ref[h] = m_new
        return carry

    lax.fori_loop(0, ((i + 1) * qb + kc - 1) // kc, key_chunk, 0)
    for h in range(AT_HEADS):
        o_ref[:, h * HEAD_DIM:(h + 1) * HEAD_DIM] = (acc_ref[h] / l_ref[h]).T.astype(o_ref.dtype)


def _dsa_prompt(p, k, v, ki, q_norm, layer, *, nb, t_len):
    n_sel = min(TOPK_MAX, t_len // 4)
    nq = t_len // Q_BLOCK
    kc = DSA_KEY_CHUNK if t_len % DSA_KEY_CHUNK == 0 else Q_BLOCK
    return pl.pallas_call(
        functools.partial(_dsa_prompt_kernel, t_len=t_len, n_sel=n_sel),
        grid=(nb, nq),
        in_specs=[pl.BlockSpec((Q_BLOCK, AT_W), lambda b, i: (b * nq + i, OFF_AQ // AT_W)),
                  pl.BlockSpec((Q_BLOCK, IDX_HEADS * IDX_DIM),
                               lambda b, i: (b * nq + i, OFF_AQI // (IDX_HEADS * IDX_DIM))),
                  pl.BlockSpec((Q_BLOCK, LANES), lambda b, i: (b * nq + i, OFF_AWI // LANES)),
                  pl.BlockSpec((t_len, KV_W), lambda b, i: (b, 0)),
                  pl.BlockSpec((t_len, KV_W), lambda b, i: (b, 0)),
                  pl.BlockSpec((t_len, IDX_DIM), lambda b, i: (b, 0)),
                  pl.BlockSpec((None, 1, HEAD_DIM), lambda b, i: (layer, 0, 0))],
        out_specs=pl.BlockSpec((Q_BLOCK, AT_W), lambda b, i: (b * nq + i, 0)),
        out_shape=jax.ShapeDtypeStruct((nb * t_len, AT_W), bf16),
        scratch_shapes=[pltpu.VMEM((t_len, Q_BLOCK), jnp.int32),
                        pltpu.VMEM((t_len, Q_BLOCK), f32),
                        pltpu.VMEM((t_len, KV_W), bf16),
                        pltpu.VMEM((AT_KV_HEADS, t_len // kc, HEAD_DIM, kc), bf16),
                        pltpu.VMEM((AT_HEADS, Q_BLOCK, HEAD_DIM), bf16),
                        pltpu.VMEM((AT_HEADS, 1, Q_BLOCK), f32),
                        pltpu.VMEM((AT_HEADS, 1, Q_BLOCK), f32),
                        pltpu.VMEM((AT_HEADS, HEAD_DIM, Q_BLOCK), f32)],
        compiler_params=_cparams(("parallel", "arbitrary")),
    )(p, p, p, k, v, ki, q_norm)


def _dsa_sample_score_kernel(pt_ref, aqi_ref, awi_ref, *refs, n_steps, npg, valid):
    page_refs, knew_ref, o_ref = refs[:npg], refs[npg], refs[npg + 1]
    g = pl.program_id(1)
    rws = SAMPLE_ROWS
    ps = page_refs[0].shape[0]
    qi = aqi_ref[...]
    wi_col = _wi_column(awi_ref[...])

    @pl.when(g < n_steps)
    def _():
        keys = jnp.concatenate([r[...] for r in page_refs], axis=0)
        o_ref[...] = _index_scores(qi, wi_col, keys)

    @pl.when(g == n_steps)
    def _():
        keys = jnp.concatenate([knew_ref[...], jnp.zeros((ps - rws, IDX_DIM), f32)], axis=0)
        s = _index_scores(qi, wi_col, keys)
        key_i = lax.broadcasted_iota(jnp.int32, (rws, ps), 1)
        q_i = lax.broadcasted_iota(jnp.int32, (rws, ps), 0)
        o_ref[:, :ps] = jnp.where((key_i <= q_i) & (key_i < valid), s, NEG_BIG)
        if npg > 1:
            o_ref[:, ps:] = jnp.full((rws, (npg - 1) * ps), NEG_BIG, f32)


def _page_specs(n_pages, npg, page_shape, layer):
    zeros = (0,) * len(page_shape)

    def spec(j):
        return pl.BlockSpec((None, None) + tuple(page_shape),
                            lambda b, g, pt: (layer, pt[b, jnp.minimum(g * npg + j, n_pages - 1)]) + zeros)
    return [spec(j) for j in range(npg)]


def _dsa_sample_scores(p, ki_new, cache_kidx, page_table, layer, *, nb, valid):
    n_pages = page_table.shape[1]
    ps = cache_kidx.shape[2]
    rws = SAMPLE_ROWS
    npg = min(SAMPLE_PAGES, n_pages)
    assert n_pages % npg == 0
    n_steps = n_pages // npg
    grid_spec = pltpu.PrefetchScalarGridSpec(
        num_scalar_prefetch=1,
        grid=(nb, n_steps + 1),
        in_specs=[pl.BlockSpec((rws, IDX_HEADS * IDX_DIM),
                               lambda b, g, pt: (b, OFF_AQI // (IDX_HEADS * IDX_DIM))),
                  pl.BlockSpec((rws, LANES), lambda b, g, pt: (b, OFF_AWI // LANES))]
        + _page_specs(n_pages, npg, (ps, IDX_DIM), layer)
        + [pl.BlockSpec((rws, IDX_DIM), lambda b, g, pt: (b, 0))],
        out_specs=pl.BlockSpec((None, rws, npg * ps), lambda b, g, pt: (b, 0, g)),
    )
    return pl.pallas_call(
        functools.partial(_dsa_sample_score_kernel, n_steps=n_steps, npg=npg, valid=valid),
        grid_spec=grid_spec,
        out_shape=jax.ShapeDtypeStruct((nb, rws, (n_steps + 1) * npg * ps), f32),
        compiler_params=_cparams(("parallel", "arbitrary")),
    )(page_table, p, p, *([cache_kidx] * npg), ki_new)


def _dsa_sample_select_kernel(s_ref, o_ref, *, n_sel, n_keys):
    chosen = _select_topk(_sortable(s_ref[...]), n_sel, n_keys)
    o_ref[...] = jnp.where(chosen & (s_ref[...] > 0.5 * NEG_BIG), 1.0, 0.0)


def _dsa_sample_select(scores, n_sel):
    nb, rws, n_keys = scores.shape
    spec = pl.BlockSpec((None, rws, n_keys), lambda b: (b, 0, 0))
    return pl.pallas_call(
        functools.partial(_dsa_sample_select_kernel, n_sel=n_sel, n_keys=n_keys),
        grid=(nb,),
        in_specs=[spec],
        out_specs=spec,
        out_shape=jax.ShapeDtypeStruct(scores.shape, f32),
        compiler_params=_cparams(("parallel",)),
    )(scores)


def _dsa_sample_attn_kernel(pt_ref, aq_ref, keep_ref, *refs, n_steps, npg):
    kpage_refs, vpage_refs = refs[:npg], refs[npg:2 * npg]
    knew_ref, vnew_ref, qn_ref, o_ref, q_ref, m_ref, l_ref, acc_ref = refs[2 * npg:]
    g = pl.program_id(1)
    rws = SAMPLE_ROWS
    ps = kpage_refs[0].shape[0] // AT_KV_HEADS

    @pl.when(g == 0)
    def _():
        m_ref[...] = jnp.full(m_ref.shape, NEG_BIG, f32)
        l_ref[...] = jnp.zeros(l_ref.shape, f32)
        acc_ref[...] = jnp.zeros(acc_ref.shape, f32)
        for n in range(AT_KV_HEADS):
            q_ref[n] = _q_heads(aq_ref[...], qn_ref[...], n).astype(bf16)

    def step(k_of, v_of, keep):
        keep4 = jnp.concatenate([keep] * AT_GROUP, axis=0) > 0.5
        for n in range(AT_KV_HEADS):
            s = _dot_nt(q_ref[n], k_of(n)) * (HEAD_DIM ** -0.5)
            s = jnp.where(keep4, s, NEG_BIG)
            m_old = m_ref[n]
            m_new = jnp.maximum(m_old, jnp.max(s, axis=1, keepdims=True))
            alpha = jnp.exp(m_old - m_new)
            e = jnp.where(keep4, jnp.exp(s - m_new), 0.0)
            l_ref[n] = alpha * l_ref[n] + jnp.sum(e, axis=1, keepdims=True)
            acc_ref[n] = alpha * acc_ref[n] + _dot(e.astype(bf16), v_of(n))
            m_ref[n] = m_new

    def page_head(pages, n):
        rows = pl.ds(n, ps, stride=AT_KV_HEADS)
        return jnp.concatenate([pg[rows, :].astype(bf16) for pg in pages], axis=0)

    def new_head(block, n):
        return block[:, n * HEAD_DIM:(n + 1) * HEAD_DIM].astype(bf16)

    @pl.when(g < n_steps)
    def _():
        step(functools.partial(page_head, kpage_refs), functools.partial(page_head, vpage_refs), keep_ref[...])

    @pl.when(g == n_steps)
    def _():
        pad = jnp.zeros((ps - rws, KV_W), f32)
        knew = jnp.concatenate([knew_ref[...], pad], axis=0)
        vnew = jnp.concatenate([vnew_ref[...], pad], axis=0)
        step(functools.partial(new_head, knew), functools.partial(new_head, vnew), keep_ref[:, :ps])
        for n in range(AT_KV_HEADS):
            o = acc_ref[n] / l_ref[n]
            for gq in range(AT_GROUP):
                h = n * AT_GROUP + gq
                o_ref[:, h * HEAD_DIM:(h + 1) * HEAD_DIM] = o[gq * rws:(gq + 1) * rws].astype(o_ref.dtype)


def _dsa_sample_attn(p, keep, cache_k, cache_v, k_new, v_new, q_norm, page_table, layer, *, nb):
    n_pages = page_table.shape[1]
    ps = cache_k.shape[2] // AT_KV_HEADS
    rws = SAMPLE_ROWS
    npg = min(SAMPLE_PAGES, n_pages)
    n_steps = n_pages // npg
    new =pl.BlockSpec((rws, KV_W), lambda b, g, pt: (b, 0))
    grid_spec = pltpu.PrefetchScalarGridSpec(
        num_scalar_prefetch=1,
        grid=(nb, n_steps + 1),
        in_specs=[pl.BlockSpec((rws, AT_W), lambda b, g, pt: (b, OFF_AQ // AT_W)),
                  pl.BlockSpec((None, rws, npg * ps), lambda b, g, pt: (b, 0, g))]
        + 2 * _page_specs(n_pages, npg, (ps * AT_KV_HEADS, HEAD_DIM), layer)
        + [new, new, pl.BlockSpec((None, 1, HEAD_DIM), lambda b, g, pt: (layer, 0, 0))],
        out_specs=pl.BlockSpec((rws, AT_W), lambda b, g, pt: (b, 0)),
        scratch_shapes=[pltpu.VMEM((AT_KV_HEADS, AT_GROUP * rws, HEAD_DIM), bf16),
                        pltpu.VMEM((AT_KV_HEADS, AT_GROUP * rws, 1), f32),
                        pltpu.VMEM((AT_KV_HEADS, AT_GROUP * rws, 1), f32),
                        pltpu.VMEM((AT_KV_HEADS, AT_GROUP * rws, HEAD_DIM), f32)],
    )
    return pl.pallas_call(
        functools.partial(_dsa_sample_attn_kernel, n_steps=n_steps, npg=npg),
        grid_spec=grid_spec,
        out_shape=jax.ShapeDtypeStruct((nb * rws, AT_W), bf16),
        compiler_params=_cparams(("parallel", "arbitrary")),
    )(page_table, p, keep, *([cache_k] * npg), *([cache_v] * npg), k_new, v_new, q_norm)


def _prep_weights(w_in, rwkv_mu, rwkv_w2, rwkv_a2, rwkv_g2, w_out, w_gate, w_up, w_down):
    depth, d, _ = w_in.shape
    hg, rw, at = jnp.split(w_in.astype(bf16), [HG_COLS, HG_COLS + RW_COLS], axis=-1)
    aq, ak, av, aqi, aki, awi = jnp.split(at, np.cumsum([AT_W, KV_W, KV_W, IDX_HEADS * IDX_DIM, IDX_DIM])
                                          .tolist(), axis=-1)
    r, k, v, lora = jnp.split(rw, [RW_W, 2 * RW_W, 3 * RW_W], axis=-1)
    z = lambda n: jnp.zeros((depth, d, n), bf16)
    w_in_p = jnp.concatenate(
        [hg, aq, aqi, ak, av, aki, awi, z(OFF_LORA - OFF_AWI - IDX_HEADS), lora, z(LORA_PAD - RW_LORA), r, k, v],
        axis=-1)
    assert w_in_p.shape[-1] == IN_PAD
    mu_r, mu_l = rwkv_mu[:, :3 * RW_W], rwkv_mu[:, 3 * RW_W:]
    mu_pad = jnp.concatenate([mu_r, mu_l, jnp.zeros((depth, LORA_PAD - RW_LORA), f32)], axis=-1)[:, None, :]
    zl = lambda n: jnp.zeros((depth, n, RW_W), bf16)
    w2p = jnp.concatenate([rwkv_w2.astype(bf16), zl(LORA_PAD - RW_DECAY_LORA)], axis=1)
    a2p = jnp.concatenate([zl(RW_DECAY_LORA), rwkv_a2.astype(bf16),
                           zl(LORA_PAD - RW_DECAY_LORA - RW_AAA_LORA)], axis=1)
    g2p = jnp.concatenate([zl(RW_DECAY_LORA + RW_AAA_LORA), rwkv_g2.astype(bf16), zl(LORA_PAD - RW_LORA)], axis=1)
    return dict(w_in=w_in_p, mu_pad=mu_pad, w2p=w2p, a2p=a2p, g2p=g2p,
                w_out=w_out.astype(bf16), w_gate=w_gate.astype(bf16), w_up=w_up.astype(bf16),
                w_down=w_down.astype(bf16))


def _shift_row(p_row):
    return jnp.concatenate([p_row[..., OFF_R:OFF_R + 3 * RW_W], p_row[..., OFF_LORA:OFF_LORA + RW_LORA]], axis=-1)


def _k_tile(f):
    half = f // 2
    return half if f % 2 == 0 and half % LANES == 0 else f


def _layer(x, layer, wts, *, nb, tp, t_valid, hg_s0, rw_s0, shift0, attend):
    h = _rmsnorm(x, wts["ln1"], layer)
    p = _matmul(h, wts["w_in"], layer, tm=1024, tn=512, tk=h.shape[1])
    valid = None if t_valid == tp else t_valid
    rows = min(256, tp)
    o_hg, hg_s = _gla(p, wts["lbs"], wts["hgrn_norm"], hg_s0, layer, nb=nb, tp=tp, rows=rows, valid=valid)
    o_rw, rw_s = _rwkv(p, shift0, rw_s0, wts, layer, nb=nb, tp=tp, t_valid=t_valid, rows=rows)
    k, v, ki = _kv_prep(p, wts["k_norm"], layer)
    o_at = attend(p, k, v, ki)
    mix = jnp.concatenate([o_hg, o_rw, o_at], axis=-1)
    x = _matmul(mix, wts["w_out"], layer, tm=1024, tn=512, tk=mix.shape[1], res=x)
    h2 = _rmsnorm(x, wts["ln2"], layer)
    act = _swiglu(h2, wts["w_gate"], wts["w_up"], layer, tm=1024, tn=256)
    x = _matmul(act, wts["w_down"], layer, tm=1024, tn=512, tk=_k_tile(act.shape[1]), res=x)
    shift = _shift_row(p.reshape(nb, tp, IN_PAD)[:, t_valid - 1])
    return x, (k, v, ki, hg_s, rw_s, shift)


def kernel(x_prompt, x_sample, cache_k, cache_v, cache_kidx, state_hgrn, state_rwkv, state_shift, page_table,
           ln1, w_in, hgrn_lb, hgrn_norm, rwkv_mu, rwkv_w0, rwkv_w2, rwkv_a0, rwkv_a2, rwkv_g2, rwkv_kk,
           rwkv_ka, rwkv_rk, rwkv_lnx_w, rwkv_lnx_b, q_norm, k_norm, w_out, ln2, w_gate, w_up, w_down):
    depth = w_in.shape[0]
    B, T, D = x_prompt.shape
    DB, DS, _ = x_sample.shape
    n_pool, page_size = cache_k.shape[1], cache_k.shape[2]
    ck = cache_k.reshape(depth, n_pool, page_size * AT_KV_HEADS, HEAD_DIM)
    cv = cache_v.reshape(depth, n_pool, page_size * AT_KV_HEADS, HEAD_DIM)
    past = page_table.shape[1] * page_size

    wts = _prep_weights(w_in, rwkv_mu, rwkv_w2, rwkv_a2, rwkv_g2, w_out, w_gate, w_up, w_down)
    lb_p = jax.nn.softmax(hgrn_lb.astype(f32), axis=0)
    row = lambda a: a.astype(f32).reshape(depth, 1, -1)
    wts.update(lbs=(jnp.cumsum(lb_p, axis=0) - lb_p[0:1])[:, None, :], hgrn_norm=row(hgrn_norm),
               ln1=row(ln1), ln2=row(ln2), w0=row(rwkv_w0), a0=row(rwkv_a0), kkw=row(rwkv_kk),
               kaw=row(rwkv_ka), rkw=row(rwkv_rk), lnw=row(rwkv_lnx_w), lnb=row(rwkv_lnx_b),
               q_norm=row(q_norm), k_norm=row(k_norm))

    xp = x_prompt.reshape(B * T, D)
    outs_p = []
    zeros_hg = jnp.zeros((B, HG_HEADS, HG_DK, HG_DV), f32)
    zeros_rw = jnp.zeros((B, RW_HEADS, RW_HD, RW_HD), f32)
    zeros_sh = jnp.zeros((B, RW_COLS), f32)
    for l in range(depth):
        attend = lambda p, k, v, ki, l=l: _dsa_prompt(p, k, v, ki, wts["q_norm"], l, nb=B, t_len=T)
        xp, st = _layer(xp, l, wts, nb=B, tp=T, t_valid=T, hg_s0=zeros_hg, rw_s0=zeros_rw, shift0=zeros_sh,
                        attend=attend)
        outs_p.append(st)

    TP = SAMPLE_ROWS
    xs = jnp.pad(x_sample, ((0, 0), (0, TP - DS), (0, 0))).reshape(DB * TP, D)
    n_sel_s = min(TOPK_MAX, (past + DS) // 4)
    outs_s = []
    for l in range(depth):
        def attend(p, k, v, ki, l=l):
            scores = _dsa_sample_scores(p, ki, cache_kidx, page_table, l, nb=DB, valid=DS)
            keep = _dsa_sample_select(scores, n_sel_s)
            return _dsa_sample_attn(p, keep, ck, cv, k, v, wts["q_norm"], page_table, l, nb=DB)

        xs, st = _layer(xs, l, wts, nb=DB, tp=TP, t_valid=DS, hg_s0=state_hgrn[l], rw_s0=state_rwkv[l],
                        shift0=state_shift[l], attend=attend)
        outs_s.append(st)

    def stack(outs, i):
        return jnp.stack([o[i] for o in outs])

    k_p = stack(outs_p, 0).reshape(depth, B, T, AT_KV_HEADS, HEAD_DIM)
    v_p = stack(outs_p, 1).reshape(depth, B, T, AT_KV_HEADS, HEAD_DIM)
    ki_p = stack(outs_p, 2).reshape(depth, B, T, IDX_DIM)
    cut = lambda a, w: a.reshape(depth, DB, TP, *w)[:, :, :DS]
    k_s = cut(stack(outs_s, 0), (AT_KV_HEADS, HEAD_DIM))
    v_s = cut(stack(outs_s, 1), (AT_KV_HEADS, HEAD_DIM))
    ki_s = cut(stack(outs_s, 2), (IDX_DIM,))
    y_p = xp.reshape(B, T, D)
    y_s = xs.reshape(DB, TP, D)[:, :DS]
    return (y_p, y_s, k_p, v_p, ki_p, stack(outs_p, 3), stack(outs_p, 4), stack(outs_p, 5),
            k_s, v_s, ki_s, stack(outs_s, 3), stack(outs_s, 4), stack(outs_s, 5))
```

```python
import functools

import jax
import jax.numpy as jnp
import numpy as np
from jax import lax
from jax.experimental import pallas as pl
from jax.experimental.pallas import tpu as pltpu

f32 = jnp.float32
bf16 = jnp.bfloat16

HG_HEADS, HG_DK, HG_DV = 8, 128, 128
HG_W = HG_HEADS * HG_DV
EXP_CLIP = 60.0
RW_HEADS, RW_HD = 16, 64
RW_W = RW_HEADS * RW_HD
RW_DECAY_LORA, RW_AAA_LORA, RW_GATE_LORA = 64, 64, 160
RW_LORA = RW_DECAY_LORA + RW_AAA_LORA + RW_GATE_LORA
RW_GN_EPS = 64e-5
RW_COLS = 3 * RW_W + RW_LORA
AT_HEADS, AT_KV_HEADS, HEAD_DIM = 16, 4, 128
AT_GROUP = AT_HEADS // AT_KV_HEADS
AT_W = AT_HEADS * HEAD_DIM
KV_W = AT_KV_HEADS * HEAD_DIM
IDX_HEADS, IDX_DIM = 8, 128
TOPK_MAX = 256
Q_BLOCK = 128
NEG_BIG = -1e30
LOG2_E = 1.4426950408889634
NORM_EPS = 1e-6
HG_COLS = 2 * HG_HEADS * HG_DK + 2 * HG_W
AT_COLS = AT_W + 2 * KV_W + IDX_HEADS * IDX_DIM + IDX_DIM + IDX_HEADS

LANES = 128
VMEM_LIMIT = 56 * 1024 * 1024
SAMPLE_ROWS = 16
GLA_CHUNK = 128
GLA_HEADS = 4
GLA_SUB = 16
GLA_SPAN = 60.0
RW_UNROLL = 32
DSA_TIERS = 4
COUNT_ROWS = 64
DSA_KEY_CHUNK = 256
SAMPLE_PAGES = 16

LORA_PAD = 512
OFF_HG = 0
OFF_AQ = HG_COLS
OFF_AQI = OFF_AQ + AT_W
OFF_AK = OFF_AQI + IDX_HEADS * IDX_DIM
OFF_AV = OFF_AK + KV_W
OFF_AKI = OFF_AV + KV_W
OFF_AWI = OFF_AKI + IDX_DIM
OFF_LORA = 8704
OFF_R = OFF_LORA + LORA_PAD
OFF_K = OFF_R + RW_W
OFF_V = OFF_K + RW_W
IN_PAD = OFF_V + RW_W


def _cparams(sem):
    return pltpu.CompilerParams(dimension_semantics=sem, vmem_limit_bytes=VMEM_LIMIT)


def _sigmoid(x):
    return jax.nn.sigmoid(x)


def _dot(a, b, precision=None):
    return jnp.dot(a, b, preferred_element_type=f32, precision=precision)


def _dot_nt(a, b, precision=None):
    return lax.dot_general(a, b, (((1,), (1,)), ((), ())), preferred_element_type=f32, precision=precision)


def _dot_tn(a, b, precision=None):
    return lax.dot_general(a, b, (((0,), (0,)), ((), ())), preferred_element_type=f32, precision=precision)


HI = lax.Precision.HIGHEST


def _split_bf16(x):
    hi = x.astype(bf16)
    return hi, (x - hi.astype(f32)).astype(bf16)


def _rmsnorm_kernel(x_ref, g_ref, o_ref):
    x = x_ref[...]
    ms = jnp.mean(x * x, axis=-1, keepdims=True)
    o_ref[...] = (x * lax.rsqrt(ms + NORM_EPS) * g_ref[...]).astype(o_ref.dtype)


def _rmsnorm(x, g, layer):
    m, d = x.shape
    tr = min(256, m)
    return pl.pallas_call(
        _rmsnorm_kernel,
        grid=(m // tr,),
        in_specs=[pl.BlockSpec((tr, d), lambda i: (i, 0)),
                  pl.BlockSpec((None, 1, d), lambda i: (layer, 0, 0))],
        out_specs=pl.BlockSpec((tr, d), lambda i: (i, 0)),
        out_shape=jax.ShapeDtypeStruct((m, d), bf16),
        compiler_params=_cparams(("parallel",)),
    )(x, g)


def _mm_kernel(*refs, nk, has_res):
    if has_res:
        a_ref, b_ref, r_ref, o_ref = refs[:4]
    else:
        a_ref, b_ref, o_ref = refs[:3]
        r_ref = None
    part = _dot(a_ref[...], b_ref[...])
    if nk == 1:
        o_ref[...] = (part + r_ref[...]) if has_res else part
        return
    acc_ref = refs[-1]
    k = pl.program_id(2)

    @pl.when(k == 0)
    def _():
        acc_ref[...] = part

    @pl.when(k > 0)
    def _():
        acc_ref[...] += part

    @pl.when(k == nk - 1)
    def _():
        o_ref[...] = (acc_ref[...] + r_ref[...]) if has_res else acc_ref[...]


def _matmul(a, w, layer, *, tm, tn, tk, res=None):
    m, kdim = a.shape
    n = w.shape[-1]
    tm = min(tm, m)
    tn = min(tn, n)
    nk = kdim // tk
    assert m % tm == 0 and n % tn == 0 and kdim % tk == 0
    in_specs = [pl.BlockSpec((tm, tk), lambda i, j, k: (i, k)),
                pl.BlockSpec((None, tk, tn), lambda i, j, k: (layer, k, j))]
    args = [a, w]
    if res is not None:
        in_specs.append(pl.BlockSpec((tm, tn), lambda i, j, k: (i, j)))
        args.append(res)
    scratch = [pltpu.VMEM((tm, tn), f32)] if nk > 1 else []
    return pl.pallas_call(
        functools.partial(_mm_kernel, nk=nk, has_res=res is not None),
        grid=(m // tm, n // tn, nk),
        in_specs=in_specs,
        out_specs=pl.BlockSpec((tm, tn), lambda i, j, k: (i, j)),
        out_shape=jax.ShapeDtypeStruct((m, n), f32),
        scratch_shapes=scratch,
        compiler_params=_cparams(("parallel", "parallel", "arbitrary")),
    )(*args)


def _swiglu_kernel(a_ref, wg_ref, wu_ref, o_ref):
    a = a_ref[...]
    g = _dot(a, wg_ref[...])
    u = _dot(a, wu_ref[...])
    o_ref[...] = (g * _sigmoid(g) * u).astype(o_ref.dtype)


def _swiglu(a, wg, wu, layer, *, tm, tn):
    m, d = a.shape
    n = wg.shape[-1]
    tm = min(tm, m)
    assert m % tm == 0 and n % tn == 0
    wspec = pl.BlockSpec((None, d, tn), lambda i, j: (layer, 0, j))
    return pl.pallas_call(
        _swiglu_kernel,
        grid=(m // tm, n // tn),
        in_specs=[pl.BlockSpec((tm, d), lambda i, j: (i, 0)), wspec, wspec],
        out_specs=pl.BlockSpec((tm, tn), lambda i, j: (i, j)),
        out_shape=jax.ShapeDtypeStruct((m, n), bf16),
        compiler_params=_cparams(("parallel", "parallel")),
    )(a, wg, wu)


def _gla_kernel(pq_ref, pf_ref, pi_ref, pg_ref, lb_ref, g_ref, s0_ref, o_ref, sout_ref, st_ref,
                *, rows, valid):
    i = pl.program_id(2)
    chunk, sub = GLA_CHUNK, GLA_SUB
    live_rows = min(rows, chunk)
    heads = range(GLA_HEADS)
    lanes = [slice(hh * LANES, (hh + 1) * LANES) for hh in heads]

    @pl.when(i == 0)
    def _():
        for hh in heads:
            st_ref[hh] = s0_ref[hh].T

    n_live = live_rows if valid is None else min(valid, live_rows)
    nsb = -(-n_live // sub)
    crow = lax.broadcasted_iota(jnp.int32, (chunk, 1), 0)
    tri = (lax.broadcasted_iota(jnp.int32, (chunk, chunk), 0)
           >= lax.broadcasted_iota(jnp.int32, (chunk, chunk), 1)).astype(f32)
    sub_row = lax.broadcasted_iota(jnp.int32, (sub, 1), 0)

    def padded(x):
        if live_rows == chunk:
            return x
        return jnp.concatenate([x, jnp.zeros((chunk - live_rows, x.shape[1]), x.dtype)], axis=0)

    tail = chunk - nsb * sub
    zero_row = jnp.zeros((1, HG_DK), f32)

    def one_chunk(c, carry):
        rs = pl.ds(pl.multiple_of(c * live_rows, live_rows), live_rows)

        def prelude(hh):
            pq = padded(pq_ref[rs, lanes[hh]])
            fr = padded(pf_ref[rs, lanes[hh]])
            v = padded(pi_ref[rs, lanes[hh]])
            lb = lb_ref[:, lanes[hh]]
            q = pq * _sigmoid(pq) * (HG_DK ** -0.5)
            log_sig = jnp.minimum(fr, 0.0) - jnp.log1p(jnp.exp(-jnp.abs(fr)))
            log_f = log_sig + jnp.log1p(lb * jnp.exp(jnp.minimum(-fr, EXP_CLIP)))
            k = (1.0 - lb) * _sigmoid(-fr)
            if valid is not None or live_rows != chunk:
                live = crow < live_rows
                if valid is not None:
                    live = live & (i * rows + c * live_rows + crow < valid)
                log_f = jnp.where(live, log_f, 0.0)
                k = jnp.where(live, k, 0.0)
            b = _dot(tri, log_f, precision=HI)
            ref_rows = [zero_row if ib == 0 else b[ib * sub - 1:ib * sub] for ib in range(nsb)]
            return q, k, v, b, ref_rows

        pre = [prelude(hh) for hh in heads]

        def scores(hh, with_diagonal):
            q, k, v, b, ref_rows = pre[hh]
            a_rows = []
            for ib in range(nsb):
                lo = ib * sub
                hi = lo + sub if with_diagonal else lo
                if hi == 0:
                    a_rows.append(jnp.zeros((sub, chunk), f32))
                    continue
                qt = (q[lo:lo + sub] * jnp.exp(b[lo:lo + sub] - ref_rows[ib])).astype(bf16)
                grow = jnp.minimum(ref_rows[ib] - b, GLA_SPAN if with_diagonal else 0.0)
                kt = jnp.where(crow < hi, k * jnp.exp(grow), 0.0).astype(bf16)
                a = _dot_nt(qt, kt)
                if with_diagonal:
                    key = lax.broadcasted_iota(jnp.int32, (sub, chunk), 1)
                    a = jnp.where(key <= lo + sub_row, a, 0.0)
                a_rows.append(a)
            if tail:
                a_rows.append(jnp.zeros((tail, chunk), f32))
            return jnp.concatenate(a_rows, axis=0)

        def diagonal_terms(hh):
            q, k, v, b, _ = pre[hh]
            d_rows = []
            for ib in range(nsb):
                lo = ib * sub
                q_i, b_i = q[lo:lo + sub], b[lo:lo + sub]
                d_i = jnp.zeros((sub, HG_DV), f32)
                for s in range(sub):
                    gs = lo + s
                    term = q_i * jnp.exp(jnp.minimum(b_i - b[gs:gs + 1], 0.0)) * k[gs:gs + 1]
                    a_col = jnp.sum(term, axis=1, keepdims=True)
                    d_i = d_i + jnp.where(sub_row >= s, a_col, 0.0) * v[gs:gs + 1]
                d_rows.append(d_i)
            if tail:
                d_rows.append(jnp.zeros((tail, HG_DV), f32))
            return jnp.concatenate(d_rows, axis=0)

        def factored():
            return tuple(x for hh in heads for x in (scores(hh, True), jnp.zeros((chunk, HG_DV), f32)))

        def term_by_term():
            return tuple(x for hh in heads for x in (scores(hh, False), diagonal_terms(hh)))

        span = zero_row
        for q, k, v, b, ref_rows in pre:
            for ib in range(nsb):
                span = jnp.maximum(span, ref_rows[ib] - b[(ib + 1) * sub - 1:(ib + 1) * sub])
        mats = lax.cond(jnp.max(span) <= GLA_SPAN, factored, term_by_term)

        for hh in heads:
            q, k, v, b, _ = pre[hh]
            st = st_ref[hh]
            v16 = v.astype(bf16)
            o = _dot_nt((q * jnp.exp(b)).astype(bf16), st.astype(bf16))
            o = o + _dot(mats[2 * hh].astype(bf16), v16) + mats[2 * hh + 1]
            b_last = b[chunk - 1:chunk]
            kd = (k * jnp.exp(b_last - b)).astype(bf16)
            st_ref[hh] = st * jnp.exp(b_last) + _dot_tn(v16, kd)
            o = o[:live_rows]
            ms = jnp.mean(o * o, axis=-1, keepdims=True)
            y = o * lax.rsqrt(ms + NORM_EPS) * g_ref[:, lanes[hh]] * _sigmoid(pg_ref[rs, lanes[hh]])
            o_ref[rs, lanes[hh]] = y.astype(o_ref.dtype)
        return carry

    lax.fori_loop(0, rows // live_rows, one_chunk, 0)

    @pl.when(i == pl.num_programs(2) - 1)
    def _():
        for hh in heads:
            sout_ref[hh] = st_ref[hh].T


def _gla(p, lb, gain, s0, layer, *, nb, tp, rows, valid):
    nblk = tp // rows
    H = HG_HEADS
    G = GLA_HEADS
    assert H % G == 0
    ng = H // G
    wide = G * LANES

    def pspec(part):
        return pl.BlockSpec((rows, wide), lambda b, h, i: (b * nblk + i, part * ng + h))

    return pl.pallas_call(
        functools.partial(_gla_kernel, rows=rows, valid=valid),
        grid=(nb, ng, nblk),
        in_specs=[pspec(0), pspec(1), pspec(2), pspec(3),
                  pl.BlockSpec((None, 1, wide), lambda b, h, i: (layer, 0, h)),
                  pl.BlockSpec((None, 1, wide), lambda b, h, i: (layer, 0, h)),
                  pl.BlockSpec((None, G, HG_DK, HG_DV), lambda b, h, i: (b, h, 0, 0))],
        out_specs=[pl.BlockSpec((rows, wide), lambda b, h, i: (b * nblk + i, h)),
                   pl.BlockSpec((None, G, HG_DK, HG_DV), lambda b, h, i: (b, h, 0, 0))],
        out_shape=[jax.ShapeDtypeStruct((nb * tp, HG_W), bf16),
                   jax.ShapeDtypeStruct((nb, H, HG_DK, HG_DV), f32)],
        scratch_shapes=[pltpu.VMEM((G, HG_DV, HG_DK), f32)],
        compiler_params=_cparams(("parallel", "parallel", "arbitrary")),
    )(p, p, p, p, lb, gain, s0)


def _head_sum(x, hm):
    cols = [_dot(x[:, c * LANES:(c + 1) * LANES], hm, precision=HI) for c in range(RW_W // LANES)]
    return jnp.concatenate(cols, axis=1)


def _head_matrix():
    r = lax.broadcasted_iota(jnp.int32, (LANES, LANES), 0) // RW_HD
    c = lax.broadcasted_iota(jnp.int32, (LANES, LANES), 1) // RW_HD
    return (r == c).astype(f32)


def _rw_pre_kernel(pr_ref, pk_ref, pv_ref, pl_ref, sr_ref, sk_ref, sv_ref, sl_ref,
                   mr_ref, mk_ref, mv_ref, ml_ref, w0_ref, w2_ref, a0_ref, a2_ref, g2_ref,
                   kkw_ref, kaw_ref, rkw_ref,
                   r_o, w_o, k_o, v_o, nkk_o, b_o, bonus_o, g_o,
                   cr_ref, ck_ref, cv_ref, cl_ref, *, rows, valid):
    i = pl.program_id(1)

    @pl.when(i == 0)
    def _():
        cr_ref[...] = sr_ref[...]
        ck_ref[...] = sk_ref[...]
        cv_ref[...] = sv_ref[...]
        cl_ref[...] = sl_ref[...]

    def mixed(p_ref, c_ref, m_ref):
        cur = p_ref[...]
        rolled = pltpu.roll(cur, 1, axis=0)
        rowid = lax.broadcasted_iota(jnp.int32, cur.shape, 0)
        prev = jnp.where(rowid == 0, c_ref[...], rolled)
        c_ref[...] = cur[rows - 1:rows, :]
        return cur + (prev - cur) * m_ref[...]

    r = mixed(pr_ref, cr_ref, mr_ref)
    k = mixed(pk_ref, ck_ref, mk_ref)
    v = mixed(pv_ref, cv_ref, mv_ref)
    xl = mixed(pl_ref, cl_ref, ml_ref)

    zw = w0_ref[...] + _dot(jnp.tanh(xl).astype(bf16), w2_ref[...])
    w = jnp.minimum(zw, 0.0) - jnp.log1p(jnp.exp(-jnp.abs(zw))) - 0.5
    decay = jnp.exp(-jnp.exp(w))
    a = _sigmoid(a0_ref[...] + _dot(xl.astype(bf16), a2_ref[...]))
    g = _dot(_sigmoid(xl).astype(bf16), g2_ref[...])

    hm = _head_matrix()
    kk = k * kkw_ref[...]
    kk = kk * lax.rsqrt(jnp.maximum(_head_sum(kk * kk, hm), 1e-24))
    k2 = k * (1.0 + (a - 1.0) * kaw_ref[...])
    bonus = _head_sum(r * k2 * rkw_ref[...], hm) * v
    nkk = -kk
    bb = kk * a
    if valid is not None:
        tok = i * rows + lax.broadcasted_iota(jnp.int32, (rows, 1), 0)
        live = tok < valid
        decay = jnp.where(live, decay, 1.0)
        k2 = jnp.where(live, k2, 0.0)
        v = jnp.where(live, v, 0.0)
        nkk = jnp.where(live, nkk, 0.0)
        bb = jnp.where(live, bb, 0.0)
    r_o[...] = r
    w_o[...] = decay
    k_o[...] = k2
    v_o[...] = v
    nkk_o[...] = nkk
    b_o[...] = bb
    bonus_o[...] = bonus
    g_o[...] = g


def _rw_pre(p, shift_pad, mu_pad, w0, w2p, a0, a2p, g2p, kkw, kaw, rkw, layer, *, nb, tp, rows, valid):
    nblk = tp // rows
    W = RW_W

    def pspec(off, width):
        return pl.BlockSpec((rows, width), lambda b, i: (b * nblk + i, off // width))

    def sspec(off, width):
        return pl.BlockSpec((None, 1, width), lambda b, i: (b, 0, off // width))

    def mspec(off, width):
        return pl.BlockSpec((None, 1, width), lambda b, i: (layer, 0, off // width))

    def vec():
        return pl.BlockSpec((None, 1, W), lambda b, i: (layer, 0, 0))

    def lora():
        return pl.BlockSpec((None, LORA_PAD, W), lambda b, i: (layer, 0, 0))

    out_spec = pl.BlockSpec((rows, W), lambda b, i: (b * nblk + i, 0))
    out_shape = jax.ShapeDtypeStruct((nb * tp, W), f32)
    return pl.pallas_call(
        functools.partial(_rw_pre_kernel, rows=rows, valid=valid),
        grid=(nb, nblk),
        in_specs=[pspec(OFF_R, W), pspec(OFF_K, W), pspec(OFF_V, W), pspec(OFF_LORA, LORA_PAD),
                  sspec(0, W), sspec(W, W), sspec(2 * W, W), sspec(3 * W, LORA_PAD),
                  mspec(0, W), mspec(W, W), mspec(2 * W, W), mspec(3 * W, LORA_PAD),
                  vec(), lora(), vec(), lora(), lora(), vec(), vec(), vec()],
        out_specs=[out_spec] * 8,
        out_shape=[out_shape] * 8,
        scratch_shapes=[pltpu.VMEM((1, W), f32), pltpu.VMEM((1, W), f32), pltpu.VMEM((1, W), f32),
                        pltpu.VMEM((1, LORA_PAD), f32)],
        compiler_params=_cparams(("parallel", "arbitrary")),
    )(p, p, p, p, shift_pad, shift_pad, shift_pad, shift_pad, mu_pad, mu_pad, mu_pad, mu_pad,
      w0, w2p, a0, a2p, g2p, kkw, kaw, rkw)


def _rw_scan_kernel(r_ref, w_ref, k_ref, b_ref, nkk_ref, v_ref, s0_ref, y_ref, sout_ref, s_ref,
                    *, tt, ki_n, fold):
    i = pl.program_id(0)

    @pl.when(i == 0)
    def _():
        s_ref[...] = s0_ref[...]

    def lane_total(x):
        return x + pltpu.roll(x, LANES // 2, axis=1) if fold else x

    half = RW_HD // 2
    halves = (slice(0, half), slice(half, RW_HD))
    zero = jnp.zeros((half, LANES), f32)

    def row(ref, t, ki):
        return ref[t, pl.ds(ki, 1), :]

    def first_sa(vr):
        acc = [zero, zero]
        for ki in range(ki_n):
            acc[ki % 2] = acc[ki % 2] + s_ref[ki, vr, :] * row(nkk_ref, 0, ki)
        return lane_total(acc[0] + acc[1])

    def token(t, sa_pair):
        nxt = jnp.minimum(t + 1, tt - 1)
        sa_next = []
        for vr, sa in zip(halves, sa_pair):
            vt = v_ref[t, vr, :]

            def k_block(kb, carry):
                y0, y1, a0, a1 = carry
                for u in range(RW_UNROLL):
                    ki = kb * RW_UNROLL + u
                    s_new = (s_ref[ki, vr, :] * row(w_ref, t, ki) + sa * row(b_ref, t, ki)
                             + vt * row(k_ref, t, ki))
                    s_ref[ki, vr, :] = s_new
                    if u % 2 == 0:
                        y0 = y0 + s_new * row(r_ref, t, ki)
                        a0 = a0 + s_new * row(nkk_ref, nxt, ki)
                    else:
                        y1 = y1 + s_new * row(r_ref, t, ki)
                        a1 = a1 + s_new * row(nkk_ref, nxt, ki)
                return y0, y1, a0, a1

            n_blocks = ki_n // RW_UNROLL
            if n_blocks == 1:
                y0, y1, a0, a1 = k_block(0, (zero, zero, zero, zero))
            else:
                y0, y1, a0, a1 = lax.fori_loop(0, n_blocks, k_block, (zero, zero, zero, zero))
            y_ref[t, vr, :] = lane_total(y0 + y1)
            sa_next.append(lane_total(a0 + a1))
        return tuple(sa_next)

    lax.fori_loop(0, tt, token, tuple(first_sa(vr) for vr in halves))

    @pl.when(i == pl.num_programs(0) - 1)
    def _():
        sout_ref[...] = s_ref[...]


def _rw_scan(rT, wT, kT, bT, nkkT, vT, s0T, *, tt):
    t_len, ki_n, _ = rT.shape
    fold = ki_n * 2 == RW_HD
    assert fold or ki_n == RW_HD
    tt = min(tt, t_len)
    assert t_len % tt == 0 and ki_n % RW_UNROLL == 0
    op = pl.BlockSpec((tt, ki_n, LANES), lambda i: (i, 0, 0))
    vs = pl.BlockSpec((tt, RW_HD, LANES), lambda i: (i, 0, 0))
    ss = pl.BlockSpec((ki_n, RW_HD, LANES), lambda i: (0, 0, 0))
    return pl.pallas_call(
        functools.partial(_rw_scan_kernel, tt=tt, ki_n=ki_n, fold=fold),
        grid=(t_len // tt,),
        in_specs=[op, op, op, op, op, vs, ss],
        out_specs=[vs, ss],
        out_shape=[jax.ShapeDtypeStruct((t_len, RW_HD, LANES), f32),
                   jax.ShapeDtypeStruct((ki_n, RW_HD, LANES), f32)],
        scratch_shapes=[pltpu.VMEM((ki_n, RW_HD, LANES), f32)],
        compiler_params=_cparams(("arbitrary",)),
    )(rT, wT, kT, bT, nkkT, vT, s0T)


def _rw_post_kernel(y_ref, bonus_ref, g_ref, lw_ref, lb_ref, o_ref):
    hm = _head_matrix()
    y = y_ref[...]
    mean = _head_sum(y, hm) * (1.0 / RW_HD)
    d = y - mean
    var = _head_sum(d * d, hm) * (1.0 / RW_HD)
    yn = d * lax.rsqrt(var + RW_GN_EPS) * lw_ref[...] + lb_ref[...]
    o_ref[...] = ((yn + bonus_ref[...]) * g_ref[...]).astype(o_ref.dtype)


def _rw_post(y, bonus, g, lnw, lnb, layer):
    m = y.shape[0]
    tr = min(512, m)
    spec = pl.BlockSpec((tr, RW_W), lambda i: (i, 0))
    vec = pl.BlockSpec((None, 1, RW_W), lambda i: (layer, 0, 0))
    return pl.pallas_call(
        _rw_post_kernel,
        grid=(m // tr,),
        in_specs=[spec, spec, spec, vec, vec],
        out_specs=spec,
        out_shape=jax.ShapeDtypeStruct((m, RW_W), bf16),
        compiler_params=_cparams(("parallel",)),
    )(y, bonus, g, lnw, lnb)


def _rwkv(p, shift0, s0, wts, layer, *, nb, tp, t_valid, rows):
    H, N = RW_HEADS, RW_HD
    kh = LANES // (nb * H)
    assert kh in (1, 2) and kh * nb * H == LANES
    ki_n = N // kh
    shift_pad = jnp.pad(shift0, ((0, 0), (0, 3 * RW_W + LORA_PAD - RW_COLS)))[:, None, :]
    valid = None if t_valid == tp else t_valid
    r, w, k, v, nkk, bb, bonus, g = _rw_pre(
        p, shift_pad, wts["mu_pad"], wts["w0"], wts["w2p"], wts["a0"], wts["a2p"], wts["g2p"],
        wts["kkw"], wts["kaw"], wts["rkw"], layer, nb=nb, tp=tp, rows=rows, valid=valid)

    def key_lanes(x):
        x = x.reshape(nb, tp, H, kh, ki_n)[:, :t_valid]
        return x.transpose(1, 4, 3, 0, 2).reshape(t_valid, ki_n, LANES)

    vT = jnp.broadcast_to(v.reshape(nb, tp, H, 1, N)[:, :t_valid], (nb, t_valid, H, kh, N))
    vT = vT.transpose(1, 4, 3, 0, 2).reshape(t_valid, N, LANES)
    s0T = s0.reshape(nb, H, N, kh, ki_n).transpose(4, 2, 3, 0, 1).reshape(ki_n, N, LANES)
    yT, sT = _rw_scan(key_lanes(r), key_lanes(w), key_lanes(k), key_lanes(bb), key_lanes(nkk), vT, s0T,
                      tt=64)
    y = yT[:, :, :nb * H].reshape(t_valid, N, nb, H).transpose(2, 0, 3, 1).reshape(nb, t_valid, RW_W)
    if t_valid != tp:
        y = jnp.pad(y, ((0, 0), (0, tp - t_valid), (0, 0)))
    y = y.reshape(nb * tp, RW_W)
    s_out = sT.reshape(ki_n, N, kh, nb, H).transpose(3, 4, 1, 2, 0).reshape(nb, H, N, N)
    o = _rw_post(y, bonus, g, wts["lnw"], wts["lnb"], layer)
    return o, s_out


def _kv_prep_kernel(ak_ref, av_ref, aki_ref, kn_ref, k_o, v_o, ki_o):
    gain = kn_ref[...]
    for n in range(AT_KV_HEADS):
        x = ak_ref[:, n * HEAD_DIM:(n + 1) * HEAD_DIM]
        ms = jnp.mean(x * x, axis=-1, keepdims=True)
        k_o[:, n * HEAD_DIM:(n + 1) * HEAD_DIM] = x * lax.rsqrt(ms + NORM_EPS) * gain
    v_o[...] = av_ref[...]
    ki_o[...] = aki_ref[...]


def _kv_prep(p, k_norm, layer):
    m = p.shape[0]
    tr = min(512, m)
    return pl.pallas_call(
        _kv_prep_kernel,
        grid=(m // tr,),
        in_specs=[pl.BlockSpec((tr, KV_W), lambda i: (i, OFF_AK // KV_W)),
                  pl.BlockSpec((tr, KV_W), lambda i: (i, OFF_AV // KV_W)),
                  pl.BlockSpec((tr, IDX_DIM), lambda i: (i, OFF_AKI // IDX_DIM)),
                  pl.BlockSpec((None, 1, HEAD_DIM), lambda i: (layer, 0, 0))],
        out_specs=[pl.BlockSpec((tr, KV_W), lambda i: (i, 0)),
                   pl.BlockSpec((tr, KV_W), lambda i: (i, 0)),
                   pl.BlockSpec((tr, IDX_DIM), lambda i: (i, 0))],
        out_shape=[jax.ShapeDtypeStruct((m, KV_W), f32), jax.ShapeDtypeStruct((m, KV_W), f32),
                   jax.ShapeDtypeStruct((m, IDX_DIM), f32)],
        compiler_params=_cparams(("parallel",)),
    )(p, p, p, k_norm)


def _index_scores(qi, wi_col, keys):
    rws = qi.shape[0]
    qs = jnp.concatenate([qi[:, h * IDX_DIM:(h + 1) * IDX_DIM] for h in range(IDX_HEADS)], axis=0)
    d = jnp.maximum(_dot_nt(qs, keys, precision=HI), 0.0) * wi_col
    s = d[0:rws]
    for h in range(1, IDX_HEADS):
        s = s + d[h * rws:(h + 1) * rws]
    return s


def _wi_column(awi):
    scale = IDX_HEADS ** -0.5 * IDX_DIM ** -0.5
    return jnp.concatenate([awi[:, h:h + 1] for h in range(IDX_HEADS)], axis=0) * scale


def _sortable(score):
    bits = lax.bitcast_convert_type(score, jnp.int32)
    key = jnp.where(bits < 0, bits ^ jnp.int32(0x7FFFFFFF), bits)
    return jnp.where(score == 0.0, 0, key)


def _select_topk(skey, n_sel, n_keys, axis=1):
    one = tuple(1 if a == axis else s for a, s in enumerate(skey.shape))
    nsel = jnp.float32(n_sel)
    int_min = jnp.int32(-2 ** 31)

    def _count(mask):
        ones = mask.astype(f32)
        if axis == 0 and ones.shape[0] % COUNT_ROWS == 0 and ones.shape[0] > COUNT_ROWS:
            ones = jnp.sum(ones.reshape(-1, COUNT_ROWS, ones.shape[1]), axis=0)
        return jnp.sum(ones, axis=axis, keepdims=True)

    zero = jnp.zeros(one, jnp.int32)
    cand = jnp.where(_count(skey >= zero) >= nsel, zero, zero + int_min)

    def bit_step(it, cand):
        trial = cand + jnp.left_shift(jnp.int32(1), 30 - it)
        return jnp.where(_count(skey >= trial) >= nsel, trial, cand)

    tau = lax.fori_loop(0, 31, bit_step, cand)
    gt = skey > tau
    eq = skey == tau
    need = nsel - _count(gt)
    idx = lax.broadcasted_iota(jnp.int32, skey.shape, axis)
    nbits = int(n_keys).bit_length()

    def idx_step(it, x):
        trial = x + jnp.left_shift(jnp.int32(1), nbits - 1 - it)
        ok = (trial <= n_keys) & (_count(eq & (idx < trial)) < need)
        return jnp.where(ok, trial, x)

    surplus = jnp.max(_count(eq) - need) > 0.0
    x = lax.cond(surplus, lambda: lax.fori_loop(0, nbits, idx_step, zero), lambda: zero + n_keys)
    return gt | (eq & (idx <= x))


def _q_heads(aq, gain, n):
    outs = []
    for g in range(AT_GROUP):
        h = n * AT_GROUP + g
        x = aq[:, h * HEAD_DIM:(h + 1) * HEAD_DIM]
        ms = jnp.mean(x * x, axis=-1, keepdims=True)
        outs.append(x * lax.rsqrt(ms + NORM_EPS) * gain)
    return jnp.concatenate(outs, axis=0)


def _dsa_prompt_kernel(aq_ref, aqi_ref, awi_ref, k_ref, v_ref, ki_ref, qn_ref, o_ref,
                       sel_ref, keep_ref, kb_ref, vt_ref, q_ref, m_ref, l_ref, acc_ref, *, t_len, n_sel):
    i = pl.program_id(1)
    qb = Q_BLOCK
    nkb = t_len // qb
    kc = DSA_KEY_CHUNK if t_len % DSA_KEY_CHUNK == 0 else qb

    @pl.when(i == 0)
    def _():
        kb_ref[...] = k_ref[...].astype(bf16)
        for n in range(AT_KV_HEADS):
            for j in range(nkb):
                tile = v_ref[j * qb:(j + 1) * qb, n * HEAD_DIM:(n + 1) * HEAD_DIM].T.astype(bf16)
                lo = (j * qb) % kc
                vt_ref[n, (j * qb) // kc, :, lo:lo + qb] = tile

    qi = aqi_ref[...]
    qs = jnp.concatenate([qi[:, h * IDX_DIM:(h + 1) * IDX_DIM] for h in range(IDX_HEADS)], axis=0)
    qs_hi, qs_lo = _split_bf16(qs)
    wi_t = awi_ref[...].T
    wi_row = jnp.concatenate([wi_t[h:h + 1, :] for h in range(IDX_HEADS)], axis=1)
    wi_row = wi_row * (IDX_HEADS ** -0.5 * IDX_DIM ** -0.5)
    q_pos = i * qb + lax.broadcasted_iota(jnp.int32, (1, qb), 1)

    for j in range(nkb):
        rows = slice(j * qb, (j + 1) * qb)

        @pl.when(j <= i)
        def _():
            k_hi, k_lo = _split_bf16(ki_ref[rows, :])
            d = _dot_nt(k_hi, qs_hi) + (_dot_nt(k_hi, qs_lo) + _dot_nt(k_lo, qs_hi))
            d = jnp.maximum(d, 0.0) * wi_row
            s = d[:, 0:qb]
            for h in range(1, IDX_HEADS):
                s = s + d[:, h * qb:(h + 1) * qb]
            key_pos = j * qb + lax.broadcasted_iota(jnp.int32, (qb, qb), 0)
            s = jnp.where(key_pos <= q_pos, s, NEG_BIG)
            sel_ref[rows, :] = _sortable(s)

        @pl.when(j > i)
        def _():
            sel_ref[rows, :] = _sortable(jnp.full((qb, qb), NEG_BIG, f32))

    def select(width):
        chosen = _select_topk(sel_ref[:width, :], n_sel, width, axis=0)
        key_pos = lax.broadcasted_iota(jnp.int32, (width, qb), 0)
        keep_ref[:width, :] = jnp.where(chosen & (key_pos <= q_pos), 1.0, 0.0)

    n_tier = DSA_TIERS if nkb % DSA_TIERS == 0 and (nkb // DSA_TIERS * qb) % kc == 0 else 1
    per_tier = nkb // n_tier
    for tier in range(n_tier):
        pl.when(i // per_tier == tier)(functools.partial(select, (tier + 1) * per_tier * qb))

    gain = qn_ref[...]
    for h in range(AT_HEADS):
        x = aq_ref[:, h * HEAD_DIM:(h + 1) * HEAD_DIM]
        ms = jnp.mean(x * x, axis=-1, keepdims=True)
        q_ref[h] = (x * lax.rsqrt(ms + NORM_EPS) * gain).astype(bf16)
    m_ref[...] = jnp.full(m_ref.shape, NEG_BIG, f32)
    l_ref[...] = jnp.zeros(l_ref.shape, f32)
    acc_ref[...] = jnp.zeros(acc_ref.shape, f32)

    def key_chunk(c, carry):
        rows = pl.ds(pl.multiple_of(c * kc, kc), kc)
        keep = keep_ref[rows, :] > 0.5
        for n in range(AT_KV_HEADS):
            kn = kb_ref[rows, n * HEAD_DIM:(n + 1) * HEAD_DIM]
            vnt = vt_ref[n, c]
            for g in range(AT_GROUP):
                h = n * AT_GROUP + g
                s = _dot_nt(kn, q_ref[h]) * (HEAD_DIM ** -0.5 * LOG2_E)
                s = jnp.where(keep, s, NEG_BIG)
                m_old = m_ref[h]
                m_new = jnp.maximum(m_old, jnp.max(s, axis=0, keepdims=True))
                alpha = jnp.exp2(m_old - m_new)
                e = jnp.exp2(s - m_new)
                l_ref[h] = alpha * l_ref[h] + jnp.sum(e, axis=0, keepdims=True)
                acc_ref[h] = alpha * acc_ref[h] + _dot(vnt, e.astype(bf16))
                m_ref[h] = m_new
        return carry

    lax.fori_loop(0, ((i + 1) * qb + kc - 1) // kc, key_chunk, 0)
    for h in range(AT_HEADS):
        o_ref[:, h * HEAD_DIM:(h + 1) * HEAD_DIM] = (acc_ref[h] / l_ref[h]).T.astype(o_ref.dtype)


def _dsa_prompt(p, k, v, ki, q_norm, layer, *, nb, t_len):
    n_sel = min(TOPK_MAX, t_len // 4)
    nq = t_len // Q_BLOCK
    kc = DSA_KEY_CHUNK if t_len % DSA_KEY_CHUNK == 0 else Q_BLOCK
    return pl.pallas_call(
        functools.partial(_dsa_prompt_kernel, t_len=t_len, n_sel=n_sel),
        grid=(nb, nq),
        in_specs=[pl.BlockSpec((Q_BLOCK, AT_W), lambda b, i: (b * nq + i, OFF_AQ // AT_W)),
                  pl.BlockSpec((Q_BLOCK, IDX_HEADS * IDX_DIM),
                               lambda b, i: (b * nq + i, OFF_AQI // (IDX_HEADS * IDX_DIM))),
                  pl.BlockSpec((Q_BLOCK, LANES), lambda b, i: (b * nq + i, OFF_AWI // LANES)),
                  pl.BlockSpec((t_len, KV_W), lambda b, i: (b, 0)),
                  pl.BlockSpec((t_len, KV_W), lambda b, i: (b, 0)),
                  pl.BlockSpec((t_len, IDX_DIM), lambda b, i: (b, 0)),
                  pl.BlockSpec((None, 1, HEAD_DIM), lambda b, i: (layer, 0, 0))],
        out_specs=pl.BlockSpec((Q_BLOCK, AT_W), lambda b, i: (b * nq + i, 0)),
        out_shape=jax.ShapeDtypeStruct((nb * t_len, AT_W), bf16),
        scratch_shapes=[pltpu.VMEM((t_len, Q_BLOCK), jnp.int32),
                        pltpu.VMEM((t_len, Q_BLOCK), f32),
                        pltpu.VMEM((t_len, KV_W), bf16),
                        pltpu.VMEM((AT_KV_HEADS, t_len // kc, HEAD_DIM, kc), bf16),
                        pltpu.VMEM((AT_HEADS, Q_BLOCK, HEAD_DIM), bf16),
                        pltpu.VMEM((AT_HEADS, 1, Q_BLOCK), f32),
                        pltpu.VMEM((AT_HEADS, 1, Q_BLOCK), f32),
                        pltpu.VMEM((AT_HEADS, HEAD_DIM, Q_BLOCK), f32)],
        compiler_params=_cparams(("parallel", "arbitrary")),
    )(p, p, p, k, v, ki, q_norm)


def _dsa_sample_score_kernel(pt_ref, aqi_ref, awi_ref, *refs, n_steps, npg, valid):
    page_refs, knew_ref, o_ref = refs[:npg], refs[npg], refs[npg + 1]
    g = pl.program_id(1)
    rws = SAMPLE_ROWS
    ps = page_refs[0].shape[0]
    qi = aqi_ref[...]
    wi_col = _wi_column(awi_ref[...])

    @pl.when(g < n_steps)
    def _():
        keys = jnp.concatenate([r[...] for r in page_refs], axis=0)
        o_ref[...] = _index_scores(qi, wi_col, keys)

    @pl.when(g == n_steps)
    def _():
        keys = jnp.concatenate([knew_ref[...], jnp.zeros((ps - rws, IDX_DIM), f32)], axis=0)
        s = _index_scores(qi, wi_col, keys)
        key_i = lax.broadcasted_iota(jnp.int32, (rws, ps), 1)
        q_i = lax.broadcasted_iota(jnp.int32, (rws, ps), 0)
        o_ref[:, :ps] = jnp.where((key_i <= q_i) & (key_i < valid), s, NEG_BIG)
        if npg > 1:
            o_ref[:, ps:] = jnp.full((rws, (npg - 1) * ps), NEG_BIG, f32)


def _page_specs(n_pages, npg, page_shape, layer):
    zeros = (0,) * len(page_shape)

    def spec(j):
        return pl.BlockSpec((None, None) + tuple(page_shape),
                            lambda b, g, pt: (layer, pt[b, jnp.minimum(g * npg + j, n_pages - 1)]) + zeros)
    return [spec(j) for j in range(npg)]


def _dsa_sample_scores(p, ki_new, cache_kidx, page_table, layer, *, nb, valid):
    n_pages = page_table.shape[1]
    ps = cache_kidx.shape[2]
    rws = SAMPLE_ROWS
    npg = min(SAMPLE_PAGES, n_pages)
    assert n_pages % npg == 0
    n_steps = n_pages // npg
    grid_spec = pltpu.PrefetchScalarGridSpec(
        num_scalar_prefetch=1,
        grid=(nb, n_steps + 1),
        in_specs=[pl.BlockSpec((rws, IDX_HEADS * IDX_DIM),
                               lambda b, g, pt: (b, OFF_AQI // (IDX_HEADS * IDX_DIM))),
                  pl.BlockSpec((rws, LANES), lambda b, g, pt: (b, OFF_AWI // LANES))]
        + _page_specs(n_pages, npg, (ps, IDX_DIM), layer)
        + [pl.BlockSpec((rws, IDX_DIM), lambda b, g, pt: (b, 0))],
        out_specs=pl.BlockSpec((None, rws, npg * ps), lambda b, g, pt: (b, 0, g)),
    )
    return pl.pallas_call(
        functools.partial(_dsa_sample_score_kernel, n_steps=n_steps, npg=npg, valid=valid),
        grid_spec=grid_spec,
        out_shape=jax.ShapeDtypeStruct((nb, rws, (n_steps + 1) * npg * ps), f32),
        compiler_params=_cparams(("parallel", "arbitrary")),
    )(page_table, p, p, *([cache_kidx] * npg), ki_new)


def _dsa_sample_select_kernel(s_ref, o_ref, *, n_sel, n_keys):
    chosen = _select_topk(_sortable(s_ref[...]), n_sel, n_keys)
    o_ref[...] = jnp.where(chosen & (s_ref[...] > 0.5 * NEG_BIG), 1.0, 0.0)


def _dsa_sample_select(scores, n_sel):
    nb, rws, n_keys = scores.shape
    spec = pl.BlockSpec((None, rws, n_keys), lambda b: (b, 0, 0))
    return pl.pallas_call(
        functools.partial(_dsa_sample_select_kernel, n_sel=n_sel, n_keys=n_keys),
        grid=(nb,),
        in_specs=[spec],
        out_specs=spec,
        out_shape=jax.ShapeDtypeStruct(scores.shape, f32),
        compiler_params=_cparams(("parallel",)),
    )(scores)


def _dsa_sample_attn_kernel(pt_ref, aq_ref, keep_ref, *refs, n_steps, npg):
    kpage_refs, vpage_refs = refs[:npg], refs[npg:2 * npg]
    knew_ref, vnew_ref, qn_ref, o_ref, q_ref, m_ref, l_ref, acc_ref = refs[2 * npg:]
    g = pl.program_id(1)
    rws = SAMPLE_ROWS
    ps = kpage_refs[0].shape[0] // AT_KV_HEADS

    @pl.when(g == 0)
    def _():
        m_ref[...] = jnp.full(m_ref.shape, NEG_BIG, f32)
        l_ref[...] = jnp.zeros(l_ref.shape, f32)
        acc_ref[...] = jnp.zeros(acc_ref.shape, f32)
        for n in range(AT_KV_HEADS):
            q_ref[n] = _q_heads(aq_ref[...], qn_ref[...], n).astype(bf16)

    def step(k_of, v_of, keep):
        keep4 = jnp.concatenate([keep] * AT_GROUP, axis=0) > 0.5
        for n in range(AT_KV_HEADS):
            s = _dot_nt(q_ref[n], k_of(n)) * (HEAD_DIM ** -0.5)
            s = jnp.where(keep4, s, NEG_BIG)
            m_old = m_ref[n]
            m_new = jnp.maximum(m_old, jnp.max(s, axis=1, keepdims=True))
            alpha = jnp.exp(m_old - m_new)
            e = jnp.where(keep4, jnp.exp(s - m_new), 0.0)
            l_ref[n] = alpha * l_ref[n] + jnp.sum(e, axis=1, keepdims=True)
            acc_ref[n] = alpha * acc_ref[n] + _dot(e.astype(bf16), v_of(n))
            m_ref[n] = m_new

    def page_head(pages, n):
        rows = pl.ds(n, ps, stride=AT_KV_HEADS)
        return jnp.concatenate([pg[rows, :].astype(bf16) for pg in pages], axis=0)

    def new_head(block, n):
        return block[:, n * HEAD_DIM:(n + 1) * HEAD_DIM].astype(bf16)

    @pl.when(g < n_steps)
    def _():
        step(functools.partial(page_head, kpage_refs), functools.partial(page_head, vpage_refs), keep_ref[...])

    @pl.when(g == n_steps)
    def _():
        pad = jnp.zeros((ps - rws, KV_W), f32)
        knew = jnp.concatenate([knew_ref[...], pad], axis=0)
        vnew = jnp.concatenate([vnew_ref[...], pad], axis=0)
        step(functools.partial(new_head, knew), functools.partial(new_head, vnew), keep_ref[:, :ps])
        for n in range(AT_KV_HEADS):
            o = acc_ref[n] / l_ref[n]
            for gq in range(AT_GROUP):
                h = n * AT_GROUP + gq
                o_ref[:, h * HEAD_DIM:(h + 1) * HEAD_DIM] = o[gq * rws:(gq + 1) * rws].astype(o_ref.dtype)


def _dsa_sample_attn(p, keep, cache_k, cache_v, k_new, v_new, q_norm, page_table, layer, *, nb):
    n_pages = page_table.shape[1]
    ps = cache_k.shape[2] // AT_KV_HEADS
    rws = SAMPLE_ROWS
    npg = min(SAMPLE_PAGES, n_pages)
    n_steps = n_pages // npg
    new =pl.BlockSpec((rws, KV_W), lambda b, g, pt: (b, 0))
    grid_spec = pltpu.PrefetchScalarGridSpec(
        num_scalar_prefetch=1,
        grid=(nb, n_steps + 1),
        in_specs=[pl.BlockSpec((rws, AT_W), lambda b, g, pt: (b, OFF_AQ // AT_W)),
                  pl.BlockSpec((None, rws, npg * ps), lambda b, g, pt: (b, 0, g))]
        + 2 * _page_specs(n_pages, npg, (ps * AT_KV_HEADS, HEAD_DIM), layer)
        + [new, new, pl.BlockSpec((None, 1, HEAD_DIM), lambda b, g, pt: (layer, 0, 0))],
        out_specs=pl.BlockSpec((rws, AT_W), lambda b, g, pt: (b, 0)),
        scratch_shapes=[pltpu.VMEM((AT_KV_HEADS, AT_GROUP * rws, HEAD_DIM), bf16),
                        pltpu.VMEM((AT_KV_HEADS, AT_GROUP * rws, 1), f32),
                        pltpu.VMEM((AT_KV_HEADS, AT_GROUP * rws, 1), f32),
                        pltpu.VMEM((AT_KV_HEADS, AT_GROUP * rws, HEAD_DIM), f32)],
    )
    return pl.pallas_call(
        functools.partial(_dsa_sample_attn_kernel, n_steps=n_steps, npg=npg),
        grid_spec=grid_spec,
        out_shape=jax.ShapeDtypeStruct((nb * rws, AT_W), bf16),
        compiler_params=_cparams(("parallel", "arbitrary")),
    )(page_table, p, keep, *([cache_k] * npg), *([cache_v] * npg), k_new, v_new, q_norm)


def _prep_weights(w_in, rwkv_mu, rwkv_w2, rwkv_a2, rwkv_g2, w_out, w_gate, w_up, w_down):
    depth, d, _ = w_in.shape
    hg, rw, at = jnp.split(w_in.astype(bf16), [HG_COLS, HG_COLS + RW_COLS], axis=-1)
    aq, ak, av, aqi, aki, awi = jnp.split(at, np.cumsum([AT_W, KV_W, KV_W, IDX_HEADS * IDX_DIM, IDX_DIM])
                                          .tolist(), axis=-1)
    r, k, v, lora = jnp.split(rw, [RW_W, 2 * RW_W, 3 * RW_W], axis=-1)
    z = lambda n: jnp.zeros((depth, d, n), bf16)
    w_in_p = jnp.concatenate(
        [hg, aq, aqi, ak, av, aki, awi, z(OFF_LORA - OFF_AWI - IDX_HEADS), lora, z(LORA_PAD - RW_LORA), r, k, v],
        axis=-1)
    assert w_in_p.shape[-1] == IN_PAD
    mu_r, mu_l = rwkv_mu[:, :3 * RW_W], rwkv_mu[:, 3 * RW_W:]
    mu_pad = jnp.concatenate([mu_r, mu_l, jnp.zeros((depth, LORA_PAD - RW_LORA), f32)], axis=-1)[:, None, :]
    zl = lambda n: jnp.zeros((depth, n, RW_W), bf16)
    w2p = jnp.concatenate([rwkv_w2.astype(bf16), zl(LORA_PAD - RW_DECAY_LORA)], axis=1)
    a2p = jnp.concatenate([zl(RW_DECAY_LORA), rwkv_a2.astype(bf16),
                           zl(LORA_PAD - RW_DECAY_LORA - RW_AAA_LORA)], axis=1)
    g2p = jnp.concatenate([zl(RW_DECAY_LORA + RW_AAA_LORA), rwkv_g2.astype(bf16), zl(LORA_PAD - RW_LORA)], axis=1)
    return dict(w_in=w_in_p, mu_pad=mu_pad, w2p=w2p, a2p=a2p, g2p=g2p,
                w_out=w_out.astype(bf16), w_gate=w_gate.astype(bf16), w_up=w_up.astype(bf16),
                w_down=w_down.astype(bf16))


def _shift_row(p_row):
    return jnp.concatenate([p_row[..., OFF_R:OFF_R + 3 * RW_W], p_row[..., OFF_LORA:OFF_LORA + RW_LORA]], axis=-1)


def _k_tile(f):
    half = f // 2
    return half if f % 2 == 0 and half % LANES == 0 else f


def _layer(x, layer, wts, *, nb, tp, t_valid, hg_s0, rw_s0, shift0, attend):
    h = _rmsnorm(x, wts["ln1"], layer)
    p = _matmul(h, wts["w_in"], layer, tm=1024, tn=512, tk=h.shape[1])
    valid = None if t_valid == tp else t_valid
    rows = min(256, tp)
    o_hg, hg_s = _gla(p, wts["lbs"], wts["hgrn_norm"], hg_s0, layer, nb=nb, tp=tp, rows=rows, valid=valid)
    o_rw, rw_s = _rwkv(p, shift0, rw_s0, wts, layer, nb=nb, tp=tp, t_valid=t_valid, rows=rows)
    k, v, ki = _kv_prep(p, wts["k_norm"], layer)
    o_at = attend(p, k, v, ki)
    mix = jnp.concatenate([o_hg, o_rw, o_at], axis=-1)
    x = _matmul(mix, wts["w_out"], layer, tm=1024, tn=512, tk=mix.shape[1], res=x)
    h2 = _rmsnorm(x, wts["ln2"], layer)
    act = _swiglu(h2, wts["w_gate"], wts["w_up"], layer, tm=1024, tn=256)
    x = _matmul(act, wts["w_down"], layer, tm=1024, tn=512, tk=_k_tile(act.shape[1]), res=x)
    shift = _shift_row(p.reshape(nb, tp, IN_PAD)[:, t_valid - 1])
    return x, (k, v, ki, hg_s, rw_s, shift)


def kernel(x_prompt, x_sample, cache_k, cache_v, cache_kidx, state_hgrn, state_rwkv, state_shift, page_table,
           ln1, w_in, hgrn_lb, hgrn_norm, rwkv_mu, rwkv_w0, rwkv_w2, rwkv_a0, rwkv_a2, rwkv_g2, rwkv_kk,
           rwkv_ka, rwkv_rk, rwkv_lnx_w, rwkv_lnx_b, q_norm, k_norm, w_out, ln2, w_gate, w_up, w_down):
    depth = w_in.shape[0]
    B, T, D = x_prompt.shape
    DB, DS, _ = x_sample.shape
    n_pool, page_size = cache_k.shape[1], cache_k.shape[2]
    ck = cache_k.reshape(depth, n_pool, page_size * AT_KV_HEADS, HEAD_DIM)
    cv = cache_v.reshape(depth, n_pool, page_size * AT_KV_HEADS, HEAD_DIM)
    past = page_table.shape[1] * page_size

    wts = _prep_weights(w_in, rwkv_mu, rwkv_w2, rwkv_a2, rwkv_g2, w_out, w_gate, w_up, w_down)
    lb_p = jax.nn.softmax(hgrn_lb.astype(f32), axis=0)
    row = lambda a: a.astype(f32).reshape(depth, 1, -1)
    wts.update(lbs=(jnp.cumsum(lb_p, axis=0) - lb_p[0:1])[:, None, :], hgrn_norm=row(hgrn_norm),
               ln1=row(ln1), ln2=row(ln2), w0=row(rwkv_w0), a0=row(rwkv_a0), kkw=row(rwkv_kk),
               kaw=row(rwkv_ka), rkw=row(rwkv_rk), lnw=row(rwkv_lnx_w), lnb=row(rwkv_lnx_b),
               q_norm=row(q_norm), k_norm=row(k_norm))

    xp = x_prompt.reshape(B * T, D)
    outs_p = []
    zeros_hg = jnp.zeros((B, HG_HEADS, HG_DK, HG_DV), f32)
    zeros_rw = jnp.zeros((B, RW_HEADS, RW_HD, RW_HD), f32)
    zeros_sh = jnp.zeros((B, RW_COLS), f32)
    for l in range(depth):
        attend = lambda p, k, v, ki, l=l: _dsa_prompt(p, k, v, ki, wts["q_norm"], l, nb=B, t_len=T)
        xp, st = _layer(xp, l, wts, nb=B, tp=T, t_valid=T, hg_s0=zeros_hg, rw_s0=zeros_rw, shift0=zeros_sh,
                        attend=attend)
        outs_p.append(st)

    TP = SAMPLE_ROWS
    xs = jnp.pad(x_sample, ((0, 0), (0, TP - DS), (0, 0))).reshape(DB * TP, D)
    n_sel_s = min(TOPK_MAX, (past + DS) // 4)
    outs_s = []
    for l in range(depth):
        def attend(p, k, v, ki, l=l):
            scores = _dsa_sample_scores(p, ki, cache_kidx, page_table, l, nb=DB, valid=DS)
            keep = _dsa_sample_select(scores, n_sel_s)
            return _dsa_sample_attn(p, keep, ck, cv, k, v, wts["q_norm"], page_table, l, nb=DB)

        xs, st = _layer(xs, l, wts, nb=DB, tp=TP, t_valid=DS, hg_s0=state_hgrn[l], rw_s0=state_rwkv[l],
                        shift0=state_shift[l], attend=attend)
        outs_s.append(st)

    def stack(outs, i):
        return jnp.stack([o[i] for o in outs])

    k_p = stack(outs_p, 0).reshape(depth, B, T, AT_KV_HEADS, HEAD_DIM)
    v_p = stack(outs_p, 1).reshape(depth, B, T, AT_KV_HEADS, HEAD_DIM)
    ki_p = stack(outs_p, 2).reshape(depth, B, T, IDX_DIM)
    cut = lambda a, w: a.reshape(depth, DB, TP, *w)[:, :, :DS]
    k_s = cut(stack(outs_s, 0), (AT_KV_HEADS, HEAD_DIM))
    v_s = cut(stack(outs_s, 1), (AT_KV_HEADS, HEAD_DIM))
    ki_s = cut(stack(outs_s, 2), (IDX_DIM,))
    y_p = xp.reshape(B, T, D)
    y_s = xs.reshape(DB, TP, D)[:, :DS]
    return (y_p, y_s, k_p, v_p, ki_p, stack(outs_p, 3), stack(outs_p, 4), stack(outs_p, 5),
            k_s, v_s, ki_s, stack(outs_s, 3), stack(outs_s, 4), stack(outs_s, 5))
```

```python
import functools

import jax
import jax.numpy as jnp
import numpy as np
from jax import lax
from jax.experimental import pallas as pl
from jax.experimental.pallas import tpu as pltpu

f32 = jnp.float32
bf16 = jnp.bfloat16

HG_HEADS, HG_DK, HG_DV = 8, 128, 128
HG_W = HG_HEADS * HG_DV
EXP_CLIP = 60.0
RW_HEADS, RW_HD = 16, 64
RW_W = RW_HEADS * RW_HD
RW_DECAY_LORA, RW_AAA_LORA, RW_GATE_LORA = 64, 64, 160
RW_LORA = RW_DECAY_LORA + RW_AAA_LORA + RW_GATE_LORA
RW_GN_EPS = 64e-5
RW_COLS = 3 * RW_W + RW_LORA
AT_HEADS, AT_KV_HEADS, HEAD_DIM = 16, 4, 128
AT_GROUP = AT_HEADS // AT_KV_HEADS
AT_W = AT_HEADS * HEAD_DIM
KV_W = AT_KV_HEADS * HEAD_DIM
IDX_HEADS, IDX_DIM = 8, 128
TOPK_MAX = 256
Q_BLOCK = 128
NEG_BIG = -1e30
LOG2_E = 1.4426950408889634
NORM_EPS = 1e-6
HG_COLS = 2 * HG_HEADS * HG_DK + 2 * HG_W
AT_COLS = AT_W + 2 * KV_W + IDX_HEADS * IDX_DIM + IDX_DIM + IDX_HEADS

LANES = 128
VMEM_LIMIT = 56 * 1024 * 1024
SAMPLE_ROWS = 16
GLA_CHUNK = 128
GLA_HEADS = 4
GLA_SUB = 16
GLA_SPAN = 60.0
RW_UNROLL = 32
DSA_TIERS = 4
COUNT_ROWS = 64
DSA_KEY_CHUNK = 256
SAMPLE_PAGES = 16

LORA_PAD = 512
OFF_HG = 0
OFF_AQ = HG_COLS
OFF_AQI = OFF_AQ + AT_W
OFF_AK = OFF_AQI + IDX_HEADS * IDX_DIM
OFF_AV = OFF_AK + KV_W
OFF_AKI = OFF_AV + KV_W
OFF_AWI = OFF_AKI + IDX_DIM
OFF_LORA = 8704
OFF_R = OFF_LORA + LORA_PAD
OFF_K = OFF_R + RW_W
OFF_V = OFF_K + RW_W
IN_PAD = OFF_V + RW_W


def _cparams(sem):
    return pltpu.CompilerParams(dimension_semantics=sem, vmem_limit_bytes=VMEM_LIMIT)


def _sigmoid(x):
    return jax.nn.sigmoid(x)


def _dot(a, b, precision=None):
    return jnp.dot(a, b, preferred_element_type=f32, precision=precision)


def _dot_nt(a, b, precision=None):
    return lax.dot_general(a, b, (((1,), (1,)), ((), ())), preferred_element_type=f32, precision=precision)


def _dot_tn(a, b, precision=None):
    return lax.dot_general(a, b, (((0,), (0,)), ((), ())), preferred_element_type=f32, precision=precision)


HI = lax.Precision.HIGHEST


def _split_bf16(x):
    hi = x.astype(bf16)
    return hi, (x - hi.astype(f32)).astype(bf16)


def _rmsnorm_kernel(x_ref, g_ref, o_ref):
    x = x_ref[...]
    ms = jnp.mean(x * x, axis=-1, keepdims=True)
    o_ref[...] = (x * lax.rsqrt(ms + NORM_EPS) * g_ref[...]).astype(o_ref.dtype)


def _rmsnorm(x, g, layer):
    m, d = x.shape
    tr = min(256, m)
    return pl.pallas_call(
        _rmsnorm_kernel,
        grid=(m // tr,),
        in_specs=[pl.BlockSpec((tr, d), lambda i: (i, 0)),
                  pl.BlockSpec((None, 1, d), lambda i: (layer, 0, 0))],
        out_specs=pl.BlockSpec((tr, d), lambda i: (i, 0)),
        out_shape=jax.ShapeDtypeStruct((m, d), bf16),
        compiler_params=_cparams(("parallel",)),
    )(x, g)


def _mm_kernel(*refs, nk, has_res):
    if has_res:
        a_ref, b_ref, r_ref, o_ref = refs[:4]
    else:
        a_ref, b_ref, o_ref = refs[:3]
        r_ref = None
    part = _dot(a_ref[...], b_ref[...].astype(bf16))
    if nk == 1:
        o_ref[...] = (part + r_ref[...]) if has_res else part
        return
    acc_ref = refs[-1]
    k = pl.program_id(2)

    @pl.when(k == 0)
    def _():
        acc_ref[...] = part

    @pl.when(k > 0)
    def _():
        acc_ref[...] += part

    @pl.when(k == nk - 1)
    def _():
        o_ref[...] = (acc_ref[...] + r_ref[...]) if has_res else acc_ref[...]


def _matmul(a, w, layer, *, tm, tn, tk, res=None):
    m, kdim = a.shape
    n = w.shape[-1]
    tm = min(tm, m)
    tn = min(tn, n)
    nk = kdim // tk
    assert m % tm == 0 and n % tn == 0 and kdim % tk == 0
    a_map = lambda i, j, k: (i, k)
    in_specs = [_resident_rows_spec(tm, tk, a_map) if nk == 1 else pl.BlockSpec((tm, tk), a_map),
                pl.BlockSpec((None, tk, tn), lambda i, j, k: (layer, k, j))]
    args = [a, w]
    if res is not None:
        in_specs.append(pl.BlockSpec((tm, tn), lambda i, j, k: (i, j)))
        args.append(res)
    scratch = [pltpu.VMEM((tm, tn), f32)] if nk > 1 else []
    return pl.pallas_call(
        functools.partial(_mm_kernel, nk=nk, has_res=res is not None),
        grid=(m // tm, n // tn, nk),
        in_specs=in_specs,
        out_specs=pl.BlockSpec((tm, tn), lambda i, j, k: (i, j)),
        out_shape=jax.ShapeDtypeStruct((m, n), f32),
        scratch_shapes=scratch,
        compiler_params=_cparams(("parallel", "parallel", "arbitrary")),
    )(*args)


def _swiglu_kernel(a_ref, wg_ref, wu_ref, o_ref):
    a = a_ref[...]
    g = _dot(a, wg_ref[...].astype(bf16))
    u = _dot(a, wu_ref[...].astype(bf16))
    o_ref[...] = (g * _sigmoid(g) * u).astype(o_ref.dtype)


def _resident_rows_spec(tm, width, index_map):
    return pl.BlockSpec((tm, width), index_map, pipeline_mode=pl.Buffered(1))


def _swiglu(a, wg, wu, layer, *, tm, tn):
    m, d = a.shape
    n = wg.shape[-1]
    tm = min(tm, m)
    assert m % tm == 0 and n % tn == 0
    wspec = pl.BlockSpec((None, d, tn), lambda i, j: (layer, 0, j))
    return pl.pallas_call(
        _swiglu_kernel,
        grid=(m // tm, n // tn),
        in_specs=[_resident_rows_spec(tm, d, lambda i, j: (i, 0)), wspec, wspec],
        out_specs=pl.BlockSpec((tm, tn), lambda i, j: (i, j)),
        out_shape=jax.ShapeDtypeStruct((m, n), bf16),
        compiler_params=_cparams(("parallel", "parallel")),
    )(a, wg, wu)


def _gla_kernel(pq_ref, pf_ref, pi_ref, pg_ref, lb_ref, g_ref, s0_ref, o_ref, sout_ref, st_ref,
                *, rows, valid):
    i = pl.program_id(2)
    chunk, sub = GLA_CHUNK, GLA_SUB
    live_rows = min(rows, chunk)
    heads = range(GLA_HEADS)
    lanes = [slice(hh * LANES, (hh + 1) * LANES) for hh in heads]

    @pl.when(i == 0)
    def _():
        for hh in heads:
            st_ref[hh] = s0_ref[hh].T

    n_live = live_rows if valid is None else min(valid, live_rows)
    nsb = -(-n_live // sub)
    crow = lax.broadcasted_iota(jnp.int32, (chunk, 1), 0)
    tri = (lax.broadcasted_iota(jnp.int32, (chunk, chunk), 0)
           >= lax.broadcasted_iota(jnp.int32, (chunk, chunk), 1)).astype(f32)
    sub_row = lax.broadcasted_iota(jnp.int32, (sub, 1), 0)

    def padded(x):
        if live_rows == chunk:
            return x
        return jnp.concatenate([x, jnp.zeros((chunk - live_rows, x.shape[1]), x.dtype)], axis=0)

    tail = chunk - nsb * sub
    zero_row = jnp.zeros((1, HG_DK), f32)

    def one_chunk(c, carry):
        rs = pl.ds(pl.multiple_of(c * live_rows, live_rows), live_rows)

        def prelude(hh):
            pq = padded(pq_ref[rs, lanes[hh]])
            fr = padded(pf_ref[rs, lanes[hh]])
            v = padded(pi_ref[rs, lanes[hh]])
            lb = lb_ref[:, lanes[hh]]
            q = pq * _sigmoid(pq) * (HG_DK ** -0.5)
            log_sig = jnp.minimum(fr, 0.0) - jnp.log1p(jnp.exp(-jnp.abs(fr)))
            log_f = log_sig + jnp.log1p(lb * jnp.exp(jnp.minimum(-fr, EXP_CLIP)))
            k = (1.0 - lb) * _sigmoid(-fr)
            if valid is not None or live_rows != chunk:
                live = crow < live_rows
                if valid is not None:
                    live = live & (i * rows + c * live_rows + crow < valid)
                log_f = jnp.where(live, log_f, 0.0)
                k = jnp.where(live, k, 0.0)
            b = _dot(tri, log_f, precision=HI)
            ref_rows = [zero_row if ib == 0 else b[ib * sub - 1:ib * sub] for ib in range(nsb)]
            return q, k, v, b, ref_rows

        pre = [prelude(hh) for hh in heads]

        def scores(hh, with_diagonal):
            q, k, v, b, ref_rows = pre[hh]
            a_rows = []
            for ib in range(nsb):
                lo = ib * sub
                hi = lo + sub if with_diagonal else lo
                if hi == 0:
                    a_rows.append(jnp.zeros((sub, chunk), f32))
                    continue
                qt = (q[lo:lo + sub] * jnp.exp(b[lo:lo + sub] - ref_rows[ib])).astype(bf16)
                grow = jnp.minimum(ref_rows[ib] - b, GLA_SPAN if with_diagonal else 0.0)
                kt = jnp.where(crow < hi, k * jnp.exp(grow), 0.0).astype(bf16)
                a = _dot_nt(qt, kt)
                if with_diagonal:
                    key = lax.broadcasted_iota(jnp.int32, (sub, chunk), 1)
                    a = jnp.where(key <= lo + sub_row, a, 0.0)
                a_rows.append(a)
            if tail:
                a_rows.append(jnp.zeros((tail, chunk), f32))
            return jnp.concatenate(a_rows, axis=0)

        def diagonal_terms(hh):
            q, k, v, b, _ = pre[hh]
            d_rows = []
            for ib in range(nsb):
                lo = ib * sub
                q_i, b_i = q[lo:lo + sub], b[lo:lo + sub]
                d_i = jnp.zeros((sub, HG_DV), f32)
                for s in range(sub):
                    gs = lo + s
                    term = q_i * jnp.exp(jnp.minimum(b_i - b[gs:gs + 1], 0.0)) * k[gs:gs + 1]
                    a_col = jnp.sum(term, axis=1, keepdims=True)
                    d_i = d_i + jnp.where(sub_row >= s, a_col, 0.0) * v[gs:gs + 1]
                d_rows.append(d_i)
            if tail:
                d_rows.append(jnp.zeros((tail, HG_DV), f32))
            return jnp.concatenate(d_rows, axis=0)

        def factored():
            return tuple(x for hh in heads for x in (scores(hh, True), jnp.zeros((chunk, HG_DV), f32)))

        def term_by_term():
            return tuple(x for hh in heads for x in (scores(hh, False), diagonal_terms(hh)))

        span = zero_row
        for q, k, v, b, ref_rows in pre:
            for ib in range(nsb):
                span = jnp.maximum(span, ref_rows[ib] - b[(ib + 1) * sub - 1:(ib + 1) * sub])
        mats = lax.cond(jnp.max(span) <= GLA_SPAN, factored, term_by_term)

        for hh in heads:
            q, k, v, b, _ = pre[hh]
            st = st_ref[hh]
            v16 = v.astype(bf16)
            o = _dot_nt((q * jnp.exp(b)).astype(bf16), st.astype(bf16))
            o = o + _dot(mats[2 * hh].astype(bf16), v16) + mats[2 * hh + 1]
            b_last = b[chunk - 1:chunk]
            kd = (k * jnp.exp(b_last - b)).astype(bf16)
            st_ref[hh] = st * jnp.exp(b_last) + _dot_tn(v16, kd)
            o = o[:live_rows]
            ms = jnp.mean(o * o, axis=-1, keepdims=True)
            y = o * lax.rsqrt(ms + NORM_EPS) * g_ref[:, lanes[hh]] * _sigmoid(pg_ref[rs, lanes[hh]])
            o_ref[rs, lanes[hh]] = y.astype(o_ref.dtype)
        return carry

    lax.fori_loop(0, rows // live_rows, one_chunk, 0)

    @pl.when(i == pl.num_programs(2) - 1)
    def _():
        for hh in heads:
            sout_ref[hh] = st_ref[hh].T


def _gla(p, lb, gain, s0, layer, *, nb, tp, rows, valid):
    nblk = tp // rows
    H = HG_HEADS
    G = GLA_HEADS
    assert H % G == 0
    ng = H // G
    wide = G * LANES

    def pspec(part):
        return pl.BlockSpec((rows, wide), lambda b, h, i: (b * nblk + i, part * ng + h))

    return pl.pallas_call(
        functools.partial(_gla_kernel, rows=rows, valid=valid),
        grid=(nb, ng, nblk),
        in_specs=[pspec(0), pspec(1), pspec(2), pspec(3),
                  pl.BlockSpec((None, 1, wide), lambda b, h, i: (layer, 0, h)),
                  pl.BlockSpec((None, 1, wide), lambda b, h, i: (layer, 0, h)),
                  pl.BlockSpec((None, G, HG_DK, HG_DV), lambda b, h, i: (b, h, 0, 0))],
        out_specs=[pl.BlockSpec((rows, wide), lambda b, h, i: (b * nblk + i, h)),
                   pl.BlockSpec((None, G, HG_DK, HG_DV), lambda b, h, i: (b, h, 0, 0))],
        out_shape=[jax.ShapeDtypeStruct((nb * tp, HG_W), bf16),
                   jax.ShapeDtypeStruct((nb, H, HG_DK, HG_DV), f32)],
        scratch_shapes=[pltpu.VMEM((G, HG_DV, HG_DK), f32)],
        compiler_params=_cparams(("parallel", "parallel", "arbitrary")),
    )(p, p, p, p, lb, gain, s0)


def _head_sum(x, hm):
    cols = [_dot(x[:, c * LANES:(c + 1) * LANES], hm, precision=HI) for c in range(RW_W // LANES)]
    return jnp.concatenate(cols, axis=1)


def _head_matrix():
    r = lax.broadcasted_iota(jnp.int32, (LANES, LANES), 0) // RW_HD
    c = lax.broadcasted_iota(jnp.int32, (LANES, LANES), 1) // RW_HD
    return (r == c).astype(f32)


def _rw_pre_kernel(pr_ref, pk_ref, pv_ref, pl_ref, sr_ref, sk_ref, sv_ref, sl_ref,
                   mr_ref, mk_ref, mv_ref, ml_ref, w0_ref, w2_ref, a0_ref, a2_ref, g2_ref,
                   kkw_ref, kaw_ref, rkw_ref,
                   r_o, w_o, k_o, v_o, nkk_o, b_o, bonus_o, g_o,
                   cr_ref, ck_ref, cv_ref, cl_ref, *, rows, valid):
    i = pl.program_id(1)

    @pl.when(i == 0)
    def _():
        cr_ref[...] = sr_ref[...]
        ck_ref[...] = sk_ref[...]
        cv_ref[...] = sv_ref[...]
        cl_ref[...] = sl_ref[...]

    def mixed(p_ref, c_ref, m_ref):
        cur = p_ref[...]
        rolled = pltpu.roll(cur, 1, axis=0)
        rowid = lax.broadcasted_iota(jnp.int32, cur.shape, 0)
        prev = jnp.where(rowid == 0, c_ref[...], rolled)
        c_ref[...] = cur[rows - 1:rows, :]
        return cur + (prev - cur) * m_ref[...]

    r = mixed(pr_ref, cr_ref, mr_ref)
    k = mixed(pk_ref, ck_ref, mk_ref)
    v = mixed(pv_ref, cv_ref, mv_ref)
    xl = mixed(pl_ref, cl_ref, ml_ref)

    zw = w0_ref[...] + _dot(jnp.tanh(xl).astype(bf16), w2_ref[...])
    w = jnp.minimum(zw, 0.0) - jnp.log1p(jnp.exp(-jnp.abs(zw))) - 0.5
    decay = jnp.exp(-jnp.exp(w))
    a = _sigmoid(a0_ref[...] + _dot(xl.astype(bf16), a2_ref[...]))
    g = _dot(_sigmoid(xl).astype(bf16), g2_ref[...])

    hm = _head_matrix()
    kk = k * kkw_ref[...]
    kk = kk * lax.rsqrt(jnp.maximum(_head_sum(kk * kk, hm), 1e-24))
    k2 = k * (1.0 + (a - 1.0) * kaw_ref[...])
    bonus = _head_sum(r * k2 * rkw_ref[...], hm) * v
    nkk = -kk
    bb = kk * a
    if valid is not None:
        tok = i * rows + lax.broadcasted_iota(jnp.int32, (rows, 1), 0)
        live = tok < valid
        decay = jnp.where(live, decay, 1.0)
        k2 = jnp.where(live, k2, 0.0)
        v = jnp.where(live, v, 0.0)
        nkk = jnp.where(live, nkk, 0.0)
        bb = jnp.where(live, bb, 0.0)
    r_o[...] = r
    w_o[...] = decay
    k_o[...] = k2
    v_o[...] = v
    nkk_o[...] = nkk
    b_o[...] = bb
    bonus_o[...] = bonus
    g_o[...] = g


def _rw_pre(p, shift_pad, mu_pad, w0, w2p, a0, a2p, g2p, kkw, kaw, rkw, layer, *, nb, tp, rows, valid):
    nblk = tp // rows
    W = RW_W

    def pspec(off, width):
        return pl.BlockSpec((rows, width), lambda b, i: (b * nblk + i, off // width))

    def sspec(off, width):
        return pl.BlockSpec((None, 1, width), lambda b, i: (b, 0, off // width))

    def mspec(off, width):
        return pl.BlockSpec((None, 1, width), lambda b, i: (layer, 0, off // width))

    def vec():
        return pl.BlockSpec((None, 1, W), lambda b, i: (layer, 0, 0))

    def lora():
        return pl.BlockSpec((None, LORA_PAD, W), lambda b, i: (layer, 0, 0))

    out_spec = pl.BlockSpec((rows, W), lambda b, i: (b * nblk + i, 0))
    out_shape = jax.ShapeDtypeStruct((nb * tp, W), f32)
    return pl.pallas_call(
        functools.partial(_rw_pre_kernel, rows=rows, valid=valid),
        grid=(nb, nblk),
        in_specs=[pspec(OFF_R, W), pspec(OFF_K, W), pspec(OFF_V, W), pspec(OFF_LORA, LORA_PAD),
                  sspec(0, W), sspec(W, W), sspec(2 * W, W), sspec(3 * W, LORA_PAD),
                  mspec(0, W), mspec(W, W), mspec(2 * W, W), mspec(3 * W, LORA_PAD),
                  vec(), lora(), vec(), lora(), lora(), vec(), vec(), vec()],
        out_specs=[out_spec] * 8,
        out_shape=[out_shape] * 8,
        scratch_shapes=[pltpu.VMEM((1, W), f32), pltpu.VMEM((1, W), f32), pltpu.VMEM((1, W), f32),
                        pltpu.VMEM((1, LORA_PAD), f32)],
        compiler_params=_cparams(("parallel", "arbitrary")),
    )(p, p, p, p, shift_pad, shift_pad, shift_pad, shift_pad, mu_pad, mu_pad, mu_pad, mu_pad,
      w0, w2p, a0, a2p, g2p, kkw, kaw, rkw)


def _rw_scan_kernel(r_ref, w_ref, k_ref, b_ref, nkk_ref, v_ref, s0_ref, y_ref, sout_ref, s_ref,
                    *, tt, ki_n, fold):
    i = pl.program_id(0)

    @pl.when(i == 0)
    def _():
        s_ref[...] = s0_ref[...]

    def lane_total(x):
        return x + pltpu.roll(x, LANES // 2, axis=1) if fold else x

    half = RW_HD // 2
    halves = (slice(0, half), slice(half, RW_HD))
    zero = jnp.zeros((half, LANES), f32)

    def row(ref, t, ki):
        return ref[t, pl.ds(ki, 1), :]

    def first_sa(vr):
        acc = [zero, zero]
        for ki in range(ki_n):
            acc[ki % 2] = acc[ki % 2] + s_ref[ki, vr, :] * row(nkk_ref, 0, ki)
        return lane_total(acc[0] + acc[1])

    def token(t, sa_pair):
        nxt = jnp.minimum(t + 1, tt - 1)
        sa_next = []
        for vr, sa in zip(halves, sa_pair):
            vt = v_ref[t, vr, :]

            def k_block(kb, carry):
                y0, y1, a0, a1 = carry
                for u in range(RW_UNROLL):
                    ki = kb * RW_UNROLL + u
                    s_new = (s_ref[ki, vr, :] * row(w_ref, t, ki) + sa * row(b_ref, t, ki)
                             + vt * row(k_ref, t, ki))
                    s_ref[ki, vr, :] = s_new
                    if u % 2 == 0:
                        y0 = y0 + s_new * row(r_ref, t, ki)
                        a0 = a0 + s_new * row(nkk_ref, nxt, ki)
                    else:
                        y1 = y1 + s_new * row(r_ref, t, ki)
                        a1 = a1 + s_new * row(nkk_ref, nxt, ki)
                return y0, y1, a0, a1

            n_blocks = ki_n // RW_UNROLL
            if n_blocks == 1:
                y0, y1, a0, a1 = k_block(0, (zero, zero, zero, zero))
            else:
                y0, y1, a0, a1 = lax.fori_loop(0, n_blocks, k_block, (zero, zero, zero, zero))
            y_ref[t, vr, :] = lane_total(y0 + y1)
            sa_next.append(lane_total(a0 + a1))
        return tuple(sa_next)

    lax.fori_loop(0, tt, token, tuple(first_sa(vr) for vr in halves))

    @pl.when(i == pl.num_programs(0) - 1)
    def _():
        sout_ref[...] = s_ref[...]


def _rw_scan(rT, wT, kT, bT, nkkT, vT, s0T, *, tt):
    t_len, ki_n, _ = rT.shape
    fold = ki_n * 2 == RW_HD
    assert fold or ki_n == RW_HD
    tt = min(tt, t_len)
    assert t_len % tt == 0 and ki_n % RW_UNROLL == 0
    op = pl.BlockSpec((tt, ki_n, LANES), lambda i: (i, 0, 0))
    vs = pl.BlockSpec((tt, RW_HD, LANES), lambda i: (i, 0, 0))
    ss = pl.BlockSpec((ki_n, RW_HD, LANES), lambda i: (0, 0, 0))
    return pl.pallas_call(
        functools.partial(_rw_scan_kernel, tt=tt, ki_n=ki_n, fold=fold),
        grid=(t_len // tt,),
        in_specs=[op, op, op, op, op, vs, ss],
        out_specs=[vs, ss],
        out_shape=[jax.ShapeDtypeStruct((t_len, RW_HD, LANES), f32),
                   jax.ShapeDtypeStruct((ki_n, RW_HD, LANES), f32)],
        scratch_shapes=[pltpu.VMEM((ki_n, RW_HD, LANES), f32)],
        compiler_params=_cparams(("arbitrary",)),
    )(rT, wT, kT, bT, nkkT, vT, s0T)


def _rw_post_kernel(y_ref, bonus_ref, g_ref, lw_ref, lb_ref, o_ref):
    hm = _head_matrix()
    y = y_ref[...]
    mean = _head_sum(y, hm) * (1.0 / RW_HD)
    d = y - mean
    var = _head_sum(d * d, hm) * (1.0 / RW_HD)
    yn = d * lax.rsqrt(var + RW_GN_EPS) * lw_ref[...] + lb_ref[...]
    o_ref[...] = ((yn + bonus_ref[...]) * g_ref[...]).astype(o_ref.dtype)


def _rw_post(y, bonus, g, lnw, lnb, layer):
    m = y.shape[0]
    tr = min(512, m)
    spec = pl.BlockSpec((tr, RW_W), lambda i: (i, 0))
    vec = pl.BlockSpec((None, 1, RW_W), lambda i: (layer, 0, 0))
    return pl.pallas_call(
        _rw_post_kernel,
        grid=(m // tr,),
        in_specs=[spec, spec, spec, vec, vec],
        out_specs=spec,
        out_shape=jax.ShapeDtypeStruct((m, RW_W), bf16),
        compiler_params=_cparams(("parallel",)),
    )(y, bonus, g, lnw, lnb)


def _rwkv(p, shift0, s0, wts, layer, *, nb, tp, t_valid, rows):
    H, N = RW_HEADS, RW_HD
    kh = LANES // (nb * H)
    assert kh in (1, 2) and kh * nb * H == LANES
    ki_n = N // kh
    shift_pad = jnp.pad(shift0, ((0, 0), (0, 3 * RW_W + LORA_PAD - RW_COLS)))[:, None, :]
    valid = None if t_valid == tp else t_valid
    r, w, k, v, nkk, bb, bonus, g = _rw_pre(
        p, shift_pad, wts["mu_pad"], wts["w0"], wts["w2p"], wts["a0"], wts["a2p"], wts["g2p"],
        wts["kkw"], wts["kaw"], wts["rkw"], layer, nb=nb, tp=tp, rows=rows, valid=valid)

    def key_lanes(x):
        x = x.reshape(nb, tp, H, kh, ki_n)[:, :t_valid]
        return x.transpose(1, 4, 3, 0, 2).reshape(t_valid, ki_n, LANES)

    vT = jnp.broadcast_to(v.reshape(nb, tp, H, 1, N)[:, :t_valid], (nb, t_valid, H, kh, N))
    vT = vT.transpose(1, 4, 3, 0, 2).reshape(t_valid, N, LANES)
    s0T = s0.reshape(nb, H, N, kh, ki_n).transpose(4, 2, 3, 0, 1).reshape(ki_n, N, LANES)
    yT, sT = _rw_scan(key_lanes(r), key_lanes(w), key_lanes(k), key_lanes(bb), key_lanes(nkk), vT, s0T,
                      tt=64)
    y = yT[:, :, :nb * H].reshape(t_valid, N, nb, H).transpose(2, 0, 3, 1).reshape(nb, t_valid, RW_W)
    if t_valid != tp:
        y = jnp.pad(y, ((0, 0), (0, tp - t_valid), (0, 0)))
    y = y.reshape(nb * tp, RW_W)
    s_out = sT.reshape(ki_n, N, kh, nb, H).transpose(3, 4, 1, 2, 0).reshape(nb, H, N, N)
    o = _rw_post(y, bonus, g, wts["lnw"], wts["lnb"], layer)
    return o, s_out


def _kv_prep_kernel(ak_ref, av_ref, aki_ref, kn_ref, k_o, v_o, ki_o):
    gain = kn_ref[...]
    for n in range(AT_KV_HEADS):
        x = ak_ref[:, n * HEAD_DIM:(n + 1) * HEAD_DIM]
        ms = jnp.mean(x * x, axis=-1, keepdims=True)
        k_o[:, n * HEAD_DIM:(n + 1) * HEAD_DIM] = x * lax.rsqrt(ms + NORM_EPS) * gain
    v_o[...] = av_ref[...]
    ki_o[...] = aki_ref[...]


def _kv_prep(p, k_norm, layer):
    m = p.shape[0]
    tr = min(512, m)
    return pl.pallas_call(
        _kv_prep_kernel,
        grid=(m // tr,),
        in_specs=[pl.BlockSpec((tr, KV_W), lambda i: (i, OFF_AK // KV_W)),
                  pl.BlockSpec((tr, KV_W), lambda i: (i, OFF_AV // KV_W)),
                  pl.BlockSpec((tr, IDX_DIM), lambda i: (i, OFF_AKI // IDX_DIM)),
                  pl.BlockSpec((None, 1, HEAD_DIM), lambda i: (layer, 0, 0))],
        out_specs=[pl.BlockSpec((tr, KV_W), lambda i: (i, 0)),
                   pl.BlockSpec((tr, KV_W), lambda i: (i, 0)),
                   pl.BlockSpec((tr, IDX_DIM), lambda i: (i, 0))],
        out_shape=[jax.ShapeDtypeStruct((m, KV_W), f32), jax.ShapeDtypeStruct((m, KV_W), f32),
                   jax.ShapeDtypeStruct((m, IDX_DIM), f32)],
        compiler_params=_cparams(("parallel",)),
    )(p, p, p, k_norm)


def _index_scores(qi, wi_col, keys):
    rws = qi.shape[0]
    qs = jnp.concatenate([qi[:, h * IDX_DIM:(h + 1) * IDX_DIM] for h in range(IDX_HEADS)], axis=0)
    d = jnp.maximum(_dot_nt(qs, keys, precision=HI), 0.0) * wi_col
    s = d[0:rws]
    for h in range(1, IDX_HEADS):
        s = s + d[h * rws:(h + 1) * rws]
    return s


def _wi_column(awi):
    scale = IDX_HEADS ** -0.5 * IDX_DIM ** -0.5
    return jnp.concatenate([awi[:, h:h + 1] for h in range(IDX_HEADS)], axis=0) * scale


def _sortable(score):
    bits = lax.bitcast_convert_type(score, jnp.int32)
    key = jnp.where(bits < 0, bits ^ jnp.int32(0x7FFFFFFF), bits)
    return jnp.where(score == 0.0, 0, key)


def _select_topk(skey, n_sel, n_keys, axis=1):
    one = tuple(1 if a == axis else s for a, s in enumerate(skey.shape))
    nsel = jnp.float32(n_sel)
    int_min = jnp.int32(-2 ** 31)

    def _count(mask):
        ones = mask.astype(f32)
        if axis == 0 and ones.shape[0] % COUNT_ROWS == 0 and ones.shape[0] > COUNT_ROWS:
            ones = jnp.sum(ones.reshape(-1, COUNT_ROWS, ones.shape[1]), axis=0)
        return jnp.sum(ones, axis=axis, keepdims=True)

    zero = jnp.zeros(one, jnp.int32)
    cand = jnp.where(_count(skey >= zero) >= nsel, zero, zero + int_min)

    def bit_step(it, cand):
        trial = cand + jnp.left_shift(jnp.int32(1), 30 - it)
        return jnp.where(_count(skey >= trial) >= nsel, trial, cand)

    tau = lax.fori_loop(0, 31, bit_step, cand)
    gt = skey > tau
    eq = skey == tau
    need = nsel - _count(gt)
    idx = lax.broadcasted_iota(jnp.int32, skey.shape, axis)
    nbits = int(n_keys).bit_length()

    def idx_step(it, x):
        trial = x + jnp.left_shift(jnp.int32(1), nbits - 1 - it)
        ok = (trial <= n_keys) & (_count(eq & (idx < trial)) < need)
        return jnp.where(ok, trial, x)

    surplus = jnp.max(_count(eq) - need) > 0.0
    x = lax.cond(surplus, lambda: lax.fori_loop(0, nbits, idx_step, zero), lambda: zero + n_keys)
    return gt | (eq & (idx <= x))


def _q_heads(aq, gain, n):
    outs = []
    for g in range(AT_GROUP):
        h = n * AT_GROUP + g
        x = aq[:, h * HEAD_DIM:(h + 1) * HEAD_DIM]
        ms = jnp.mean(x * x, axis=-1, keepdims=True)
        outs.append(x * lax.rsqrt(ms + NORM_EPS) * gain)
    return jnp.concatenate(outs, axis=0)


def _dsa_prompt_kernel(aq_ref, aqi_ref, awi_ref, k_ref, v_ref, ki_ref, qn_ref, o_ref,
                       sel_ref, keep_ref, kb_ref, vt_ref, q_ref, m_ref, l_ref, acc_ref, *, t_len, n_sel):
    i = pl.program_id(1)
    qb = Q_BLOCK
    nkb = t_len // qb
    kc = DSA_KEY_CHUNK if t_len % DSA_KEY_CHUNK == 0 else qb

    @pl.when(i == 0)
    def _():
        kb_ref[...] = k_ref[...].astype(bf16)
        for n in range(AT_KV_HEADS):
            for j in range(nkb):
                tile = v_ref[j * qb:(j + 1) * qb, n * HEAD_DIM:(n + 1) * HEAD_DIM].T.astype(bf16)
                lo = (j * qb) % kc
                vt_ref[n, (j * qb) // kc, :, lo:lo + qb] = tile

    qi = aqi_ref[...]
    qs = jnp.concatenate([qi[:, h * IDX_DIM:(h + 1) * IDX_DIM] for h in range(IDX_HEADS)], axis=0)
    qs_hi, qs_lo = _split_bf16(qs)
    wi_t = awi_ref[...].T
    wi_row = jnp.concatenate([wi_t[h:h + 1, :] for h in range(IDX_HEADS)], axis=1)
    wi_row = wi_row * (IDX_HEADS ** -0.5 * IDX_DIM ** -0.5)
    q_pos = i * qb + lax.broadcasted_iota(jnp.int32, (1, qb), 1)

    for j in range(nkb):
        rows = slice(j * qb, (j + 1) * qb)

        @pl.when(j <= i)
        def _():
            k_hi, k_lo = _split_bf16(ki_ref[rows, :])
            d = _dot_nt(k_hi, qs_hi) + (_dot_nt(k_hi, qs_lo) + _dot_nt(k_lo, qs_hi))
            d = jnp.maximum(d, 0.0) * wi_row
            s = d[:, 0:qb]
            for h in range(1, IDX_HEADS):
                s = s + d[:, h * qb:(h + 1) * qb]
            key_pos = j * qb + lax.broadcasted_iota(jnp.int32, (qb, qb), 0)
            s = jnp.where(key_pos <= q_pos, s, NEG_BIG)
            sel_ref[rows, :] = _sortable(s)

        @pl.when(j > i)
        def _():
            sel_ref[rows, :] = _sortable(jnp.full((qb, qb), NEG_BIG, f32))

    def select(width):
        chosen = _select_topk(sel_ref[:width, :], n_sel, width, axis=0)
        key_pos = lax.broadcasted_iota(jnp.int32, (width, qb), 0)
        keep_ref[:width, :] = jnp.where(chosen & (key_pos <= q_pos), 1.0, 0.0)

    n_tier = DSA_TIERS if nkb % DSA_TIERS == 0 and (nkb // DSA_TIERS * qb) % kc == 0 else 1
    per_tier = nkb // n_tier
    for tier in range(n_tier):
        pl.when(i // per_tier == tier)(functools.partial(select, (tier + 1) * per_tier * qb))

    gain = qn_ref[...]
    for h in range(AT_HEADS):
        x = aq_ref[:, h * HEAD_DIM:(h + 1) * HEAD_DIM]
        ms = jnp.mean(x * x, axis=-1, keepdims=True)
        q_ref[h] = (x * lax.rsqrt(ms + NORM_EPS) * gain).astype(bf16)
    m_ref[...] = jnp.full(m_ref.shape, NEG_BIG, f32)
    l_ref[...] = jnp.zeros(l_ref.shape, f32)
    acc_ref[...] = jnp.zeros(acc_ref.shape, f32)

    def key_chunk(c, carry):
        rows = pl.ds(pl.multiple_of(c * kc, kc), kc)
        keep = keep_ref[rows, :] > 0.5
        for n in range(AT_KV_HEADS):
            kn = kb_ref[rows, n * HEAD_DIM:(n + 1) * HEAD_DIM]
            vnt = vt_ref[n, c]
            for g in range(AT_GROUP):
                h = n * AT_GROUP + g
                s = _dot_nt(kn, q_ref[h]) * (HEAD_DIM ** -0.5 * LOG2_E)
                s = jnp.where(keep, s, NEG_BIG)
                m_old = m_ref[h]
                m_new = jnp.maximum(m_old, jnp.max(s, axis=0, keepdims=True))
                alpha = jnp.exp2(m_old - m_new)
                e = jnp.exp2(s - m_new)
                l_ref[h] = alpha * l_ref[h] + jnp.sum(e, axis=0, keepdims=True)
                acc_ref[h] = alpha * acc_ref[h] + _dot(vnt, e.astype(bf16))
                m_ref[h] = m_new
        return carry

    lax.fori_loop(0, ((i + 1) * qb + kc - 1) // kc, key_chunk, 0)
    for h in range(AT_HEADS):
        o_ref[:, h * HEAD_DIM:(h + 1) * HEAD_DIM] = (acc_ref[h] / l_ref[h]).T.astype(o_ref.dtype)


def _dsa_prompt(p, k, v, ki, q_norm, layer, *, nb, t_len):
    n_sel = min(TOPK_MAX, t_len // 4)
    nq = t_len // Q_BLOCK
    kc = DSA_KEY_CHUNK if t_len % DSA_KEY_CHUNK == 0 else Q_BLOCK
    return pl.pallas_call(
        functools.partial(_dsa_prompt_kernel, t_len=t_len, n_sel=n_sel),
        grid=(nb, nq),
        in_specs=[pl.BlockSpec((Q_BLOCK, AT_W), lambda b, i: (b * nq + i, OFF_AQ // AT_W)),
                  pl.BlockSpec((Q_BLOCK, IDX_HEADS * IDX_DIM),
                               lambda b, i: (b * nq + i, OFF_AQI // (IDX_HEADS * IDX_DIM))),
                  pl.BlockSpec((Q_BLOCK, LANES), lambda b, i: (b * nq + i, OFF_AWI // LANES)),
                  pl.BlockSpec((t_len, KV_W), lambda b, i: (b, 0)),
                  pl.BlockSpec((t_len, KV_W), lambda b, i: (b, 0)),
                  pl.BlockSpec((t_len, IDX_DIM), lambda b, i: (b, 0)),
                  pl.BlockSpec((None, 1, HEAD_DIM), lambda b, i: (layer, 0, 0))],
        out_specs=pl.BlockSpec((Q_BLOCK, AT_W), lambda b, i: (b * nq + i, 0)),
        out_shape=jax.ShapeDtypeStruct((nb * t_len, AT_W), bf16),
        scratch_shapes=[pltpu.VMEM((t_len, Q_BLOCK), jnp.int32),
                        pltpu.VMEM((t_len, Q_BLOCK), f32),
                        pltpu.VMEM((t_len, KV_W), bf16),
                        pltpu.VMEM((AT_KV_HEADS, t_len // kc, HEAD_DIM, kc), bf16),
                        pltpu.VMEM((AT_HEADS, Q_BLOCK, HEAD_DIM), bf16),
                        pltpu.VMEM((AT_HEADS, 1, Q_BLOCK), f32),
                        pltpu.VMEM((AT_HEADS, 1, Q_BLOCK), f32),
                        pltpu.VMEM((AT_HEADS, HEAD_DIM, Q_BLOCK), f32)],
        compiler_params=_cparams(("parallel", "arbitrary")),
    )(p, p, p, k, v, ki, q_norm)


def _dsa_sample_score_kernel(pt_ref, aqi_ref, awi_ref, *refs, n_steps, npg, valid):
    page_refs, knew_ref, o_ref = refs[:npg], refs[npg], refs[npg + 1]
    g = pl.program_id(1)
    rws = SAMPLE_ROWS
    ps = page_refs[0].shape[0]
    qi = aqi_ref[...]
    wi_col = _wi_column(awi_ref[...])

    @pl.when(g < n_steps)
    def _():
        keys = jnp.concatenate([r[...] for r in page_refs], axis=0)
        o_ref[...] = _index_scores(qi, wi_col, keys)

    @pl.when(g == n_steps)
    def _():
        keys = jnp.concatenate([knew_ref[...], jnp.zeros((ps - rws, IDX_DIM), f32)], axis=0)
        s = _index_scores(qi, wi_col, keys)
        key_i = lax.broadcasted_iota(jnp.int32, (rws, ps), 1)
        q_i = lax.broadcasted_iota(jnp.int32, (rws, ps), 0)
        o_ref[:, :ps] = jnp.where((key_i <= q_i) & (key_i < valid), s, NEG_BIG)
        if npg > 1:
            o_ref[:, ps:] = jnp.full((rws, (npg - 1) * ps), NEG_BIG, f32)


def _page_specs(n_pages, npg, page_shape, layer):
    zeros = (0,) * len(page_shape)

    def spec(j):
        return pl.BlockSpec((None, None) + tuple(page_shape),
                            lambda b, g, pt: (layer, pt[b, jnp.minimum(g * npg + j, n_pages - 1)]) + zeros)
    return [spec(j) for j in range(npg)]


def _dsa_sample_scores(p, ki_new, cache_kidx, page_table, layer, *, nb, valid):
    n_pages = page_table.shape[1]
    ps = cache_kidx.shape[2]
    rws = SAMPLE_ROWS
    npg = min(SAMPLE_PAGES, n_pages)
    assert n_pages % npg == 0
    n_steps = n_pages // npg
    grid_spec = pltpu.PrefetchScalarGridSpec(
        num_scalar_prefetch=1,
        grid=(nb, n_steps + 1),
        in_specs=[pl.BlockSpec((rws, IDX_HEADS * IDX_DIM),
                               lambda b, g, pt: (b, OFF_AQI // (IDX_HEADS * IDX_DIM))),
                  pl.BlockSpec((rws, LANES), lambda b, g, pt: (b, OFF_AWI // LANES))]
        + _page_specs(n_pages, npg, (ps, IDX_DIM), layer)
        + [pl.BlockSpec((rws, IDX_DIM), lambda b, g, pt: (b, 0))],
        out_specs=pl.BlockSpec((None, rws, npg * ps), lambda b, g, pt: (b, 0, g)),
    )
    return pl.pallas_call(
        functools.partial(_dsa_sample_score_kernel, n_steps=n_steps, npg=npg, valid=valid),
        grid_spec=grid_spec,
        out_shape=jax.ShapeDtypeStruct((nb, rws, (n_steps + 1) * npg * ps), f32),
        compiler_params=_cparams(("parallel", "arbitrary")),
    )(page_table, p, p, *([cache_kidx] * npg), ki_new)


def _dsa_sample_select_kernel(s_ref, o_ref, *, n_sel, n_keys):
    chosen = _select_topk(_sortable(s_ref[...]), n_sel, n_keys)
    o_ref[...] = jnp.where(chosen & (s_ref[...] > 0.5 * NEG_BIG), 1.0, 0.0)


def _dsa_sample_select(scores, n_sel):
    nb, rws, n_keys = scores.shape
    spec = pl.BlockSpec((None, rws, n_keys), lambda b: (b, 0, 0))
    return pl.pallas_call(
        functools.partial(_dsa_sample_select_kernel, n_sel=n_sel, n_keys=n_keys),
        grid=(nb,),
        in_specs=[spec],
        out_specs=spec,
        out_shape=jax.ShapeDtypeStruct(scores.shape, f32),
        compiler_params=_cparams(("parallel",)),
    )(scores)


def _dsa_sample_attn_kernel(pt_ref, aq_ref, keep_ref, *refs, n_steps, npg):
    kpage_refs, vpage_refs = refs[:npg], refs[npg:2 * npg]
    knew_ref, vnew_ref, qn_ref, o_ref, q_ref, m_ref, l_ref, acc_ref = refs[2 * npg:]
    g = pl.program_id(1)
    rws = SAMPLE_ROWS
    ps = kpage_refs[0].shape[0] // AT_KV_HEADS

    @pl.when(g == 0)
    def _():
        m_ref[...] = jnp.full(m_ref.shape, NEG_BIG, f32)
        l_ref[...] = jnp.zeros(l_ref.shape, f32)
        acc_ref[...] = jnp.zeros(acc_ref.shape, f32)
        for n in range(AT_KV_HEADS):
            q_ref[n] = _q_heads(aq_ref[...], qn_ref[...], n).astype(bf16)

    def step(k_of, v_of, keep):
        keep4 = jnp.concatenate([keep] * AT_GROUP, axis=0) > 0.5
        for n in range(AT_KV_HEADS):
            s = _dot_nt(q_ref[n], k_of(n)) * (HEAD_DIM ** -0.5)
            s = jnp.where(keep4, s, NEG_BIG)
            m_old = m_ref[n]
            m_new = jnp.maximum(m_old, jnp.max(s, axis=1, keepdims=True))
            alpha = jnp.exp(m_old - m_new)
            e = jnp.where(keep4, jnp.exp(s - m_new), 0.0)
            l_ref[n] = alpha * l_ref[n] + jnp.sum(e, axis=1, keepdims=True)
            acc_ref[n] = alpha * acc_ref[n] + _dot(e.astype(bf16), v_of(n))
            m_ref[n] = m_new

    def page_head(pages, n):
        rows = pl.ds(n, ps, stride=AT_KV_HEADS)
        return jnp.concatenate([pg[rows, :].astype(bf16) for pg in pages], axis=0)

    def new_head(block, n):
        return block[:, n * HEAD_DIM:(n + 1) * HEAD_DIM].astype(bf16)

    @pl.when(g < n_steps)
    def _():
        step(functools.partial(page_head, kpage_refs), functools.partial(page_head, vpage_refs), keep_ref[...])

    @pl.when(g == n_steps)
    def _():
        pad = jnp.zeros((ps - rws, KV_W), f32)
        knew = jnp.concatenate([knew_ref[...], pad], axis=0)
        vnew = jnp.concatenate([vnew_ref[...], pad], axis=0)
        step(functools.partial(new_head, knew), functools.partial(new_head, vnew), keep_ref[:, :ps])
        for n in range(AT_KV_HEADS):
            o = acc_ref[n] / l_ref[n]
            for gq in range(AT_GROUP):
                h = n * AT_GROUP + gq
                o_ref[:, h * HEAD_DIM:(h + 1) * HEAD_DIM] = o[gq * rws:(gq + 1) * rws].astype(o_ref.dtype)


def _dsa_sample_attn(p, keep, cache_k, cache_v, k_new, v_new, q_norm, page_table, layer, *, nb):
    n_pages = page_table.shape[1]
    ps = cache_k.shape[2] // AT_KV_HEADS
    rws = SAMPLE_ROWS
    npg = min(SAMPLE_PAGES, n_pages)
    n_steps = n_pages // npg
    new =pl.BlockSpec((rws, KV_W), lambda b, g, pt: (b, 0))
    grid_spec = pltpu.PrefetchScalarGridSpec(
        num_scalar_prefetch=1,
        grid=(nb, n_steps + 1),
        in_specs=[pl.BlockSpec((rws, AT_W), lambda b, g, pt: (b, OFF_AQ // AT_W)),
                  pl.BlockSpec((None, rws, npg * ps), lambda b, g, pt: (b, 0, g))]
        + 2 * _page_specs(n_pages, npg, (ps * AT_KV_HEADS, HEAD_DIM), layer)
        + [new, new, pl.BlockSpec((None, 1, HEAD_DIM), lambda b, g, pt: (layer, 0, 0))],
        out_specs=pl.BlockSpec((rws, AT_W), lambda b, g, pt: (b, 0)),
        scratch_shapes=[pltpu.VMEM((AT_KV_HEADS, AT_GROUP * rws, HEAD_DIM), bf16),
                        pltpu.VMEM((AT_KV_HEADS, AT_GROUP * rws, 1), f32),
                        pltpu.VMEM((AT_KV_HEADS, AT_GROUP * rws, 1), f32),
                        pltpu.VMEM((AT_KV_HEADS, AT_GROUP * rws, HEAD_DIM), f32)],
    )
    return pl.pallas_call(
        functools.partial(_dsa_sample_attn_kernel, n_steps=n_steps, npg=npg),
        grid_spec=grid_spec,
        out_shape=jax.ShapeDtypeStruct((nb * rws, AT_W), bf16),
        compiler_params=_cparams(("parallel", "arbitrary")),
    )(page_table, p, keep, *([cache_k] * npg), *([cache_v] * npg), k_new, v_new, q_norm)


def _prep_weights(w_in, rwkv_mu, rwkv_w2, rwkv_a2, rwkv_g2, w_out, w_gate, w_up, w_down):
    depth, d, _ = w_in.shape
    hg, rw, at = jnp.split(w_in.astype(bf16), [HG_COLS, HG_COLS + RW_COLS], axis=-1)
    aq, ak, av, aqi, aki, awi = jnp.split(at, np.cumsum([AT_W, KV_W, KV_W, IDX_HEADS * IDX_DIM, IDX_DIM])
                                          .tolist(), axis=-1)
    r, k, v, lora = jnp.split(rw, [RW_W, 2 * RW_W, 3 * RW_W], axis=-1)
    z = lambda n: jnp.zeros((depth, d, n), bf16)
    w_in_p = jnp.concatenate(
        [hg, aq, aqi, ak, av, aki, awi, z(OFF_LORA - OFF_AWI - IDX_HEADS), lora, z(LORA_PAD - RW_LORA), r, k, v],
        axis=-1)
    assert w_in_p.shape[-1] == IN_PAD
    mu_r, mu_l = rwkv_mu[:, :3 * RW_W], rwkv_mu[:, 3 * RW_W:]
    mu_pad = jnp.concatenate([mu_r, mu_l, jnp.zeros((depth, LORA_PAD - RW_LORA), f32)], axis=-1)[:, None, :]
    zl = lambda n: jnp.zeros((depth, n, RW_W), bf16)
    w2p = jnp.concatenate([rwkv_w2.astype(bf16), zl(LORA_PAD - RW_DECAY_LORA)], axis=1)
    a2p = jnp.concatenate([zl(RW_DECAY_LORA), rwkv_a2.astype(bf16),
                           zl(LORA_PAD - RW_DECAY_LORA - RW_AAA_LORA)], axis=1)
    g2p = jnp.concatenate([zl(RW_DECAY_LORA + RW_AAA_LORA), rwkv_g2.astype(bf16), zl(LORA_PAD - RW_LORA)], axis=1)
    return dict(w_in=w_in_p, mu_pad=mu_pad, w2p=w2p, a2p=a2p, g2p=g2p,
                w_out=w_out, w_gate=w_gate, w_up=w_up, w_down=w_down.astype(bf16))


def _shift_row(p_row):
    return jnp.concatenate([p_row[..., OFF_R:OFF_R + 3 * RW_W], p_row[..., OFF_LORA:OFF_LORA + RW_LORA]], axis=-1)


def _k_tile(f):
    half = f // 2
    return half if f % 2 == 0 and half % LANES == 0 else f


def _layer(x, layer, wts, *, nb, tp, t_valid, hg_s0, rw_s0, shift0, attend):
    h = _rmsnorm(x, wts["ln1"], layer)
    p = _matmul(h, wts["w_in"], layer, tm=1024, tn=512, tk=h.shape[1])
    valid = None if t_valid == tp else t_valid
    rows = min(256, tp)
    o_hg, hg_s = _gla(p, wts["lbs"], wts["hgrn_norm"], hg_s0, layer, nb=nb, tp=tp, rows=rows, valid=valid)
    o_rw, rw_s = _rwkv(p, shift0, rw_s0, wts, layer, nb=nb, tp=tp, t_valid=t_valid, rows=rows)
    k, v, ki = _kv_prep(p, wts["k_norm"], layer)
    o_at = attend(p, k, v, ki)
    mix = jnp.concatenate([o_hg, o_rw, o_at], axis=-1)
    x = _matmul(mix, wts["w_out"], layer, tm=2048, tn=256, tk=mix.shape[1], res=x)
    h2 = _rmsnorm(x, wts["ln2"], layer)
    act = _swiglu(h2, wts["w_gate"], wts["w_up"], layer, tm=2048, tn=256)
    x = _matmul(act, wts["w_down"], layer, tm=1024, tn=512, tk=_k_tile(act.shape[1]), res=x)
    shift = _shift_row(p.reshape(nb, tp, IN_PAD)[:, t_valid - 1])
    return x, (k, v, ki, hg_s, rw_s, shift)


def kernel(x_prompt, x_sample, cache_k, cache_v, cache_kidx, state_hgrn, state_rwkv, state_shift, page_table,
           ln1, w_in, hgrn_lb, hgrn_norm, rwkv_mu, rwkv_w0, rwkv_w2, rwkv_a0, rwkv_a2, rwkv_g2, rwkv_kk,
           rwkv_ka, rwkv_rk, rwkv_lnx_w, rwkv_lnx_b, q_norm, k_norm, w_out, ln2, w_gate, w_up, w_down):
    depth = w_in.shape[0]
    B, T, D = x_prompt.shape
    DB, DS, _ = x_sample.shape
    n_pool, page_size = cache_k.shape[1], cache_k.shape[2]
    ck = cache_k.reshape(depth, n_pool, page_size * AT_KV_HEADS, HEAD_DIM)
    cv = cache_v.reshape(depth, n_pool, page_size * AT_KV_HEADS, HEAD_DIM)
    past = page_table.shape[1] * page_size

    wts = _prep_weights(w_in, rwkv_mu, rwkv_w2, rwkv_a2, rwkv_g2, w_out, w_gate, w_up, w_down)
    lb_p = jax.nn.softmax(hgrn_lb.astype(f32), axis=0)
    row = lambda a: a.astype(f32).reshape(depth, 1, -1)
    wts.update(lbs=(jnp.cumsum(lb_p, axis=0) - lb_p[0:1])[:, None, :], hgrn_norm=row(hgrn_norm),
               ln1=row(ln1), ln2=row(ln2), w0=row(rwkv_w0), a0=row(rwkv_a0), kkw=row(rwkv_kk),
               kaw=row(rwkv_ka), rkw=row(rwkv_rk), lnw=row(rwkv_lnx_w), lnb=row(rwkv_lnx_b),
               q_norm=row(q_norm), k_norm=row(k_norm))

    xp = x_prompt.reshape(B * T, D)
    outs_p = []
    zeros_hg = jnp.zeros((B, HG_HEADS, HG_DK, HG_DV), f32)
    zeros_rw = jnp.zeros((B, RW_HEADS, RW_HD, RW_HD), f32)
    zeros_sh = jnp.zeros((B, RW_COLS), f32)
    for l in range(depth):
        attend = lambda p, k, v, ki, l=l: _dsa_prompt(p, k, v, ki, wts["q_norm"], l, nb=B, t_len=T)
        xp, st = _layer(xp, l, wts, nb=B, tp=T, t_valid=T, hg_s0=zeros_hg, rw_s0=zeros_rw, shift0=zeros_sh,
                        attend=attend)
        outs_p.append(st)

    TP = SAMPLE_ROWS
    xs = jnp.pad(x_sample, ((0, 0), (0, TP - DS), (0, 0))).reshape(DB * TP, D)
    n_sel_s = min(TOPK_MAX, (past + DS) // 4)
    outs_s = []
    for l in range(depth):
        def attend(p, k, v, ki, l=l):
            scores = _dsa_sample_scores(p, ki, cache_kidx, page_table, l, nb=DB, valid=DS)
            keep = _dsa_sample_select(scores, n_sel_s)
            return _dsa_sample_attn(p, keep, ck, cv, k, v, wts["q_norm"], page_table, l, nb=DB)

        xs, st = _layer(xs, l, wts, nb=DB, tp=TP, t_valid=DS, hg_s0=state_hgrn[l], rw_s0=state_rwkv[l],
                        shift0=state_shift[l], attend=attend)
        outs_s.append(st)

    def stack(outs, i):
        return jnp.stack([o[i] for o in outs])

    k_p = stack(outs_p, 0).reshape(depth, B, T, AT_KV_HEADS, HEAD_DIM)
    v_p = stack(outs_p, 1).reshape(depth, B, T, AT_KV_HEADS, HEAD_DIM)
    ki_p = stack(outs_p, 2).reshape(depth, B, T, IDX_DIM)
    cut = lambda a, w: a.reshape(depth, DB, TP, *w)[:, :, :DS]
    k_s = cut(stack(outs_s, 0), (AT_KV_HEADS, HEAD_DIM))
    v_s = cut(stack(outs_s, 1), (AT_KV_HEADS, HEAD_DIM))
    ki_s = cut(stack(outs_s, 2), (IDX_DIM,))
    y_p = xp.reshape(B, T, D)
    y_s = xs.reshape(DB, TP, D)[:, :DS]
    return (y_p, y_s, k_p, v_p, ki_p, stack(outs_p, 3), stack(outs_p, 4), stack(outs_p, 5),
            k_s, v_s, ki_s, stack(outs_s, 3), stack(outs_s, 4), stack(outs_s, 5))
```

```python
import functools

import jax
import jax.numpy as jnp
import numpy as np
from jax import lax
from jax.experimental import pallas as pl
from jax.experimental.pallas import tpu as pltpu

f32 = jnp.float32
bf16 = jnp.bfloat16

HG_HEADS, HG_DK, HG_DV = 8, 128, 128
HG_W = HG_HEADS * HG_DV
EXP_CLIP = 60.0
RW_HEADS, RW_HD = 16, 64
RW_W = RW_HEADS * RW_HD
RW_DECAY_LORA, RW_AAA_LORA, RW_GATE_LORA = 64, 64, 160
RW_LORA = RW_DECAY_LORA + RW_AAA_LORA + RW_GATE_LORA
RW_GN_EPS = 64e-5
RW_COLS = 3 * RW_W + RW_LORA
AT_HEADS, AT_KV_HEADS, HEAD_DIM = 16, 4, 128
AT_GROUP = AT_HEADS // AT_KV_HEADS
AT_W = AT_HEADS * HEAD_DIM
KV_W = AT_KV_HEADS * HEAD_DIM
IDX_HEADS, IDX_DIM = 8, 128
TOPK_MAX = 256
Q_BLOCK = 128
NEG_BIG = -1e30
LOG2_E = 1.4426950408889634
NORM_EPS = 1e-6
HG_COLS = 2 * HG_HEADS * HG_DK + 2 * HG_W
AT_COLS = AT_W + 2 * KV_W + IDX_HEADS * IDX_DIM + IDX_DIM + IDX_HEADS

LANES = 128
VMEM_LIMIT = 56 * 1024 * 1024
SAMPLE_ROWS = 16
GLA_CHUNK = 128
GLA_HEADS = 8
GLA_SUB = 16
GLA_SPAN = 60.0
RW_UNROLL = 32
DSA_TIERS = 4
COUNT_ROWS = 64
DSA_KEY_CHUNK = 256
SAMPLE_PAGES = 16

LORA_PAD = 512
OFF_HG = 0
OFF_AQ = HG_COLS
OFF_AQI = OFF_AQ + AT_W
OFF_AK = OFF_AQI + IDX_HEADS * IDX_DIM
OFF_AV = OFF_AK + KV_W
OFF_AKI = OFF_AV + KV_W
OFF_AWI = OFF_AKI + IDX_DIM
OFF_LORA = 8704
OFF_R = OFF_LORA + LORA_PAD
OFF_K = OFF_R + RW_W
OFF_V = OFF_K + RW_W
IN_PAD = OFF_V + RW_W


def _cparams(sem):
    return pltpu.CompilerParams(dimension_semantics=sem, vmem_limit_bytes=VMEM_LIMIT)


def _sigmoid(x):
    return jax.nn.sigmoid(x)


def _dot(a, b, precision=None):
    return jnp.dot(a, b, preferred_element_type=f32, precision=precision)


def _dot_nt(a, b, precision=None):
    return lax.dot_general(a, b, (((1,), (1,)), ((), ())), preferred_element_type=f32, precision=precision)


def _dot_tn(a, b, precision=None):
    return lax.dot_general(a, b, (((0,), (0,)), ((), ())), preferred_element_type=f32, precision=precision)


HI = lax.Precision.HIGHEST


def _split_bf16(x):
    hi = x.astype(bf16)
    return hi, (x - hi.astype(f32)).astype(bf16)


def _rmsnorm_kernel(x_ref, g_ref, o_ref):
    x = x_ref[...]
    ms = jnp.mean(x * x, axis=-1, keepdims=True)
    o_ref[...] = (x * lax.rsqrt(ms + NORM_EPS) * g_ref[...]).astype(o_ref.dtype)


def _rmsnorm(x, g, layer):
    m, d = x.shape
    tr = min(256, m)
    return pl.pallas_call(
        _rmsnorm_kernel,
        grid=(m // tr,),
        in_specs=[pl.BlockSpec((tr, d), lambda i: (i, 0)),
                  pl.BlockSpec((None, 1, d), lambda i: (layer, 0, 0))],
        out_specs=pl.BlockSpec((tr, d), lambda i: (i, 0)),
        out_shape=jax.ShapeDtypeStruct((m, d), bf16),
        compiler_params=_cparams(("parallel",)),
    )(x, g)


def _mm_kernel(*refs, nk, has_res):
    if has_res:
        a_ref, b_ref, r_ref, o_ref = refs[:4]
    else:
        a_ref, b_ref, o_ref = refs[:3]
        r_ref = None
    part = _dot(a_ref[...], b_ref[...].astype(bf16))
    if nk == 1:
        o_ref[...] = (part + r_ref[...]) if has_res else part
        return
    acc_ref = refs[-1]
    k = pl.program_id(2)

    @pl.when(k == 0)
    def _():
        acc_ref[...] = part

    @pl.when(k > 0)
    def _():
        acc_ref[...] += part

    @pl.when(k == nk - 1)
    def _():
        o_ref[...] = (acc_ref[...] + r_ref[...]) if has_res else acc_ref[...]


def _matmul(a, w, layer, *, tm, tn, tk, res=None):
    m, kdim = a.shape
    n = w.shape[-1]
    tm = min(tm, m)
    tn = min(tn, n)
    nk = kdim // tk
    assert m % tm == 0 and n % tn == 0 and kdim % tk == 0
    in_specs = [pl.BlockSpec((tm, tk), lambda i, j, k: (i, k)),
                pl.BlockSpec((None, tk, tn), lambda i, j, k: (layer, k, j))]
    args = [a, w]
    if res is not None:
        in_specs.append(pl.BlockSpec((tm, tn), lambda i, j, k: (i, j)))
        args.append(res)
    scratch = [pltpu.VMEM((tm, tn), f32)] if nk > 1 else []
    return pl.pallas_call(
        functools.partial(_mm_kernel, nk=nk, has_res=res is not None),
        grid=(m // tm, n // tn, nk),
        in_specs=in_specs,
        out_specs=pl.BlockSpec((tm, tn), lambda i, j, k: (i, j)),
        out_shape=jax.ShapeDtypeStruct((m, n), f32),
        scratch_shapes=scratch,
        compiler_params=_cparams(("parallel", "parallel", "arbitrary")),
    )(*args)


def _swiglu_kernel(a_ref, wg_ref, wu_ref, o_ref):
    a = a_ref[...]
    g = _dot(a, wg_ref[...].astype(bf16))
    u = _dot(a, wu_ref[...].astype(bf16))
    o_ref[...] = (g * _sigmoid(g) * u).astype(o_ref.dtype)


def _resident_rows_spec(tm, width, index_map):
    return pl.BlockSpec((tm, width), index_map, pipeline_mode=pl.Buffered(1))


def _swiglu(a, wg, wu, layer, *, tm, tn):
    m, d = a.shape
    n = wg.shape[-1]
    tm = min(tm, m)
    assert m % tm == 0 and n % tn == 0
    wspec = pl.BlockSpec((None, d, tn), lambda i, j: (layer, 0, j))
    return pl.pallas_call(
        _swiglu_kernel,
        grid=(m // tm, n // tn),
        in_specs=[_resident_rows_spec(tm, d, lambda i, j: (i, 0)), wspec, wspec],
        out_specs=pl.BlockSpec((tm, tn), lambda i, j: (i, j)),
        out_shape=jax.ShapeDtypeStruct((m, n), bf16),
        compiler_params=_cparams(("parallel", "parallel")),
    )(a, wg, wu)


def _gla_kernel(pq_ref, pf_ref, pi_ref, pg_ref, lb_ref, g_ref, s0_ref, o_ref, sout_ref, st_ref,
                *, rows, valid):
    i = pl.program_id(2)
    chunk, sub = GLA_CHUNK, GLA_SUB
    live_rows = min(rows, chunk)
    heads = range(GLA_HEADS)
    lanes = [slice(hh * LANES, (hh + 1) * LANES) for hh in heads]

    @pl.when(i == 0)
    def _():
        for hh in heads:
            st_ref[hh] = s0_ref[hh].T

    n_live = live_rows if valid is None else min(valid, live_rows)
    nsb = -(-n_live // sub)
    crow = lax.broadcasted_iota(jnp.int32, (chunk, 1), 0)
    tri = (lax.broadcasted_iota(jnp.int32, (chunk, chunk), 0)
           >= lax.broadcasted_iota(jnp.int32, (chunk, chunk), 1)).astype(f32)
    sub_row = lax.broadcasted_iota(jnp.int32, (sub, 1), 0)

    def padded(x):
        if live_rows == chunk:
            return x
        return jnp.concatenate([x, jnp.zeros((chunk - live_rows, x.shape[1]), x.dtype)], axis=0)

    tail = chunk - nsb * sub
    zero_row = jnp.zeros((1, HG_DK), f32)

    def one_chunk(c, carry):
        rs = pl.ds(pl.multiple_of(c * live_rows, live_rows), live_rows)

        def prelude(hh):
            pq = padded(pq_ref[rs, lanes[hh]])
            fr = padded(pf_ref[rs, lanes[hh]])
            v = padded(pi_ref[rs, lanes[hh]])
            lb = lb_ref[:, lanes[hh]]
            q = pq * _sigmoid(pq) * (HG_DK ** -0.5)
            log_sig = jnp.minimum(fr, 0.0) - jnp.log1p(jnp.exp(-jnp.abs(fr)))
            log_f = log_sig + jnp.log1p(lb * jnp.exp(jnp.minimum(-fr, EXP_CLIP)))
            k = (1.0 - lb) * _sigmoid(-fr)
            if valid is not None or live_rows != chunk:
                live = crow < live_rows
                if valid is not None:
                    live = live & (i * rows + c * live_rows + crow < valid)
                log_f = jnp.where(live, log_f, 0.0)
                k = jnp.where(live, k, 0.0)
            b = _dot(tri, log_f, precision=HI)
            ref_rows = [zero_row if ib == 0 else b[ib * sub - 1:ib * sub] for ib in range(nsb)]
            return q, k, v, b, ref_rows

        pre = [prelude(hh) for hh in heads]

        def scores(hh, with_diagonal):
            q, k, v, b, ref_rows = pre[hh]
            a_rows = []
            for ib in range(nsb):
                lo = ib * sub
                hi = lo + sub if with_diagonal else lo
                if hi == 0:
                    a_rows.append(jnp.zeros((sub, chunk), f32))
                    continue
                qt = (q[lo:lo + sub] * jnp.exp(b[lo:lo + sub] - ref_rows[ib])).astype(bf16)
                grow = jnp.minimum(ref_rows[ib] - b, GLA_SPAN if with_diagonal else 0.0)
                kt = jnp.where(crow < hi, k * jnp.exp(grow), 0.0).astype(bf16)
                a = _dot_nt(qt, kt)
                if with_diagonal:
                    key = lax.broadcasted_iota(jnp.int32, (sub, chunk), 1)
                    a = jnp.where(key <= lo + sub_row, a, 0.0)
                a_rows.append(a)
            if tail:
                a_rows.append(jnp.zeros((tail, chunk), f32))
            return jnp.concatenate(a_rows, axis=0)

        def diagonal_terms(hh):
            q, k, v, b, _ = pre[hh]
            d_rows = []
            for ib in range(nsb):
                lo = ib * sub
                q_i, b_i = q[lo:lo + sub], b[lo:lo + sub]
                d_i = jnp.zeros((sub, HG_DV), f32)
                for s in range(sub):
                    gs = lo + s
                    term = q_i * jnp.exp(jnp.minimum(b_i - b[gs:gs + 1], 0.0)) * k[gs:gs + 1]
                    a_col = jnp.sum(term, axis=1, keepdims=True)
                    d_i = d_i + jnp.where(sub_row >= s, a_col, 0.0) * v[gs:gs + 1]
                d_rows.append(d_i)
            if tail:
                d_rows.append(jnp.zeros((tail, HG_DV), f32))
            return jnp.concatenate(d_rows, axis=0)

        def factored():
            return tuple(x for hh in heads for x in (scores(hh, True), jnp.zeros((chunk, HG_DV), f32)))

        def term_by_term():
            return tuple(x for hh in heads for x in (scores(hh, False), diagonal_terms(hh)))

        span = zero_row
        for q, k, v, b, ref_rows in pre:
            for ib in range(nsb):
                span = jnp.maximum(span, ref_rows[ib] - b[(ib + 1) * sub - 1:(ib + 1) * sub])
        mats = lax.cond(jnp.max(span) <= GLA_SPAN, factored, term_by_term)

        for hh in heads:
            q, k, v, b, _ = pre[hh]
            st = st_ref[hh]
            v16 = v.astype(bf16)
            o = _dot_nt((q * jnp.exp(b)).astype(bf16), st.astype(bf16))
            o = o + _dot(mats[2 * hh].astype(bf16), v16) + mats[2 * hh + 1]
            b_last = b[chunk - 1:chunk]
            kd = (k * jnp.exp(b_last - b)).astype(bf16)
            st_ref[hh] = st * jnp.exp(b_last) + _dot_tn(v16, kd)
            o = o[:live_rows]
            ms = jnp.mean(o * o, axis=-1, keepdims=True)
            y = o * lax.rsqrt(ms + NORM_EPS) * g_ref[:, lanes[hh]] * _sigmoid(pg_ref[rs, lanes[hh]])
            o_ref[rs, lanes[hh]] = y.astype(o_ref.dtype)
        return carry

    lax.fori_loop(0, rows // live_rows, one_chunk, 0)

    @pl.when(i == pl.num_programs(2) - 1)
    def _():
        for hh in heads:
            sout_ref[hh] = st_ref[hh].T


def _gla(p, lb, gain, s0, layer, *, nb, tp, rows, valid):
    nblk = tp // rows
    H = HG_HEADS
    G = GLA_HEADS
    assert H % G == 0
    ng = H // G
    wide = G * LANES

    def pspec(part):
        return pl.BlockSpec((rows, wide), lambda b, h, i: (b * nblk + i, part * ng + h))

    return pl.pallas_call(
        functools.partial(_gla_kernel, rows=rows, valid=valid),
        grid=(nb, ng, nblk),
        in_specs=[pspec(0), pspec(1), pspec(2), pspec(3),
                  pl.BlockSpec((None, 1, wide), lambda b, h, i: (layer, 0, h)),
                  pl.BlockSpec((None, 1, wide), lambda b, h, i: (layer, 0, h)),
                  pl.BlockSpec((None, G, HG_DK, HG_DV), lambda b, h, i: (b, h, 0, 0))],
        out_specs=[pl.BlockSpec((rows, wide), lambda b, h, i: (b * nblk + i, h)),
                   pl.BlockSpec((None, G, HG_DK, HG_DV), lambda b, h, i: (b, h, 0, 0))],
        out_shape=[jax.ShapeDtypeStruct((nb * tp, HG_W), bf16),
                   jax.ShapeDtypeStruct((nb, H, HG_DK, HG_DV), f32)],
        scratch_shapes=[pltpu.VMEM((G, HG_DV, HG_DK), f32)],
        compiler_params=_cparams(("parallel", "parallel", "arbitrary")),
    )(p, p, p, p, lb, gain, s0)


def _head_sum(x, hm):
    cols = [_dot(x[:, c * LANES:(c + 1) * LANES], hm, precision=HI) for c in range(RW_W // LANES)]
    return jnp.concatenate(cols, axis=1)


def _head_matrix():
    r = lax.broadcasted_iota(jnp.int32, (LANES, LANES), 0) // RW_HD
    c = lax.broadcasted_iota(jnp.int32, (LANES, LANES), 1) // RW_HD
    return (r == c).astype(f32)


def _rw_pre_kernel(pr_ref, pk_ref, pv_ref, pl_ref, sr_ref, sk_ref, sv_ref, sl_ref,
                   mr_ref, mk_ref, mv_ref, ml_ref, w0_ref, w2_ref, a0_ref, a2_ref, g2_ref,
                   kkw_ref, kaw_ref, rkw_ref,
                   r_o, w_o, k_o, v_o, nkk_o, b_o, bonus_o, g_o,
                   cr_ref, ck_ref, cv_ref, cl_ref, *, rows, valid):
    i = pl.program_id(1)

    @pl.when(i == 0)
    def _():
        cr_ref[...] = sr_ref[...]
        ck_ref[...] = sk_ref[...]
        cv_ref[...] = sv_ref[...]
        cl_ref[...] = sl_ref[...]

    def mixed(p_ref, c_ref, m_ref):
        cur = p_ref[...]
        rolled = pltpu.roll(cur, 1, axis=0)
        rowid = lax.broadcasted_iota(jnp.int32, cur.shape, 0)
        prev = jnp.where(rowid == 0, c_ref[...], rolled)
        c_ref[...] = cur[rows - 1:rows, :]
        return cur + (prev - cur) * m_ref[...]

    r = mixed(pr_ref, cr_ref, mr_ref)
    k = mixed(pk_ref, ck_ref, mk_ref)
    v = mixed(pv_ref, cv_ref, mv_ref)
    xl = mixed(pl_ref, cl_ref, ml_ref)

    zw = w0_ref[...] + _dot(jnp.tanh(xl).astype(bf16), w2_ref[...])
    w = jnp.minimum(zw, 0.0) - jnp.log1p(jnp.exp(-jnp.abs(zw))) - 0.5
    decay = jnp.exp(-jnp.exp(w))
    a = _sigmoid(a0_ref[...] + _dot(xl.astype(bf16), a2_ref[...]))
    g = _dot(_sigmoid(xl).astype(bf16), g2_ref[...])

    hm = _head_matrix()
    kk = k * kkw_ref[...]
    kk = kk * lax.rsqrt(jnp.maximum(_head_sum(kk * kk, hm), 1e-24))
    k2 = k * (1.0 + (a - 1.0) * kaw_ref[...])
    bonus = _head_sum(r * k2 * rkw_ref[...], hm) * v
    nkk = -kk
    bb = kk * a
    if valid is not None:
        tok = i * rows + lax.broadcasted_iota(jnp.int32, (rows, 1), 0)
        live = tok < valid
        decay = jnp.where(live, decay, 1.0)
        k2 = jnp.where(live, k2, 0.0)
        v = jnp.where(live, v, 0.0)
        nkk = jnp.where(live, nkk, 0.0)
        bb = jnp.where(live, bb, 0.0)
    r_o[...] = r
    w_o[...] = decay
    k_o[...] = k2
    v_o[...] = v
    nkk_o[...] = nkk
    b_o[...] = bb
    bonus_o[...] = bonus
    g_o[...] = g


def _rw_pre(p, shift_pad, mu_pad, w0, w2p, a0, a2p, g2p, kkw, kaw, rkw, layer, *, nb, tp, rows, valid):
    nblk = tp // rows
    W = RW_W

    def pspec(off, width):
        return pl.BlockSpec((rows, width), lambda b, i: (b * nblk + i, off // width))

    def sspec(off, width):
        return pl.BlockSpec((None, 1, width), lambda b, i: (b, 0, off // width))

    def mspec(off, width):
        return pl.BlockSpec((None, 1, width), lambda b, i: (layer, 0, off // width))

    def vec():
        return pl.BlockSpec((None, 1, W), lambda b, i: (layer, 0, 0))

    def lora():
        return pl.BlockSpec((None, LORA_PAD, W), lambda b, i: (layer, 0, 0))

    out_spec = pl.BlockSpec((rows, W), lambda b, i: (b * nblk + i, 0))
    out_shape = jax.ShapeDtypeStruct((nb * tp, W), f32)
    return pl.pallas_call(
        functools.partial(_rw_pre_kernel, rows=rows, valid=valid),
        grid=(nb, nblk),
        in_specs=[pspec(OFF_R, W), pspec(OFF_K, W), pspec(OFF_V, W), pspec(OFF_LORA, LORA_PAD),
                  sspec(0, W), sspec(W, W), sspec(2 * W, W), sspec(3 * W, LORA_PAD),
                  mspec(0, W), mspec(W, W), mspec(2 * W, W), mspec(3 * W, LORA_PAD),
                  vec(), lora(), vec(), lora(), lora(), vec(), vec(), vec()],
        out_specs=[out_spec] * 8,
        out_shape=[out_shape] * 8,
        scratch_shapes=[pltpu.VMEM((1, W), f32), pltpu.VMEM((1, W), f32), pltpu.VMEM((1, W), f32),
                        pltpu.VMEM((1, LORA_PAD), f32)],
        compiler_params=_cparams(("parallel", "arbitrary")),
    )(p, p, p, p, shift_pad, shift_pad, shift_pad, shift_pad, mu_pad, mu_pad, mu_pad, mu_pad,
      w0, w2p, a0, a2p, g2p, kkw, kaw, rkw)


def _rw_scan_kernel(r_ref, w_ref, k_ref, b_ref, nkk_ref, v_ref, s0_ref, y_ref, sout_ref, s_ref,
                    *, tt, ki_n, fold):
    i = pl.program_id(0)

    @pl.when(i == 0)
    def _():
        s_ref[...] = s0_ref[...]

    def lane_total(x):
        return x + pltpu.roll(x, LANES // 2, axis=1) if fold else x

    half = RW_HD // 2
    halves = (slice(0, half), slice(half, RW_HD))
    zero = jnp.zeros((half, LANES), f32)

    def row(ref, t, ki):
        return ref[t, pl.ds(ki, 1), :]

    def first_sa(vr):
        acc = [zero, zero]
        for ki in range(ki_n):
            acc[ki % 2] = acc[ki % 2] + s_ref[ki, vr, :] * row(nkk_ref, 0, ki)
        return lane_total(acc[0] + acc[1])

    def token(t, sa_pair):
        nxt = jnp.minimum(t + 1, tt - 1)
        sa_next = []
        for vr, sa in zip(halves, sa_pair):
            vt = v_ref[t, vr, :]

            def k_block(kb, carry):
                y0, y1, a0, a1 = carry
                for u in range(RW_UNROLL):
                    ki = kb * RW_UNROLL + u
                    s_new = (s_ref[ki, vr, :] * row(w_ref, t, ki) + sa * row(b_ref, t, ki)
                             + vt * row(k_ref, t, ki))
                    s_ref[ki, vr, :] = s_new
                    if u % 2 == 0:
                        y0 = y0 + s_new * row(r_ref, t, ki)
                        a0 = a0 + s_new * row(nkk_ref, nxt, ki)
                    else:
                        y1 = y1 + s_new * row(r_ref, t, ki)
                        a1 = a1 + s_new * row(nkk_ref, nxt, ki)
                return y0, y1, a0, a1

            n_blocks = ki_n // RW_UNROLL
            if n_blocks == 1:
                y0, y1, a0, a1 = k_block(0, (zero, zero, zero, zero))
            else:
                y0, y1, a0, a1 = lax.fori_loop(0, n_blocks, k_block, (zero, zero, zero, zero))
            y_ref[t, vr, :] = lane_total(y0 + y1)
            sa_next.append(lane_total(a0 + a1))
        return tuple(sa_next)

    lax.fori_loop(0, tt, token, tuple(first_sa(vr) for vr in halves))

    @pl.when(i == pl.num_programs(0) - 1)
    def _():
        sout_ref[...] = s_ref[...]


def _rw_scan(rT, wT, kT, bT, nkkT, vT, s0T, *, tt):
    t_len, ki_n, _ = rT.shape
    fold = ki_n * 2 == RW_HD
    assert fold or ki_n == RW_HD
    tt = min(tt, t_len)
    assert t_len % tt == 0 and ki_n % RW_UNROLL == 0
    op = pl.BlockSpec((tt, ki_n, LANES), lambda i: (i, 0, 0))
    vs = pl.BlockSpec((tt, RW_HD, LANES), lambda i: (i, 0, 0))
    ss = pl.BlockSpec((ki_n, RW_HD, LANES), lambda i: (0, 0, 0))
    return pl.pallas_call(
        functools.partial(_rw_scan_kernel, tt=tt, ki_n=ki_n, fold=fold),
        grid=(t_len // tt,),
        in_specs=[op, op, op, op, op, vs, ss],
        out_specs=[vs, ss],
        out_shape=[jax.ShapeDtypeStruct((t_len, RW_HD, LANES), f32),
                   jax.ShapeDtypeStruct((ki_n, RW_HD, LANES), f32)],
        scratch_shapes=[pltpu.VMEM((ki_n, RW_HD, LANES), f32)],
        compiler_params=_cparams(("arbitrary",)),
    )(rT, wT, kT, bT, nkkT, vT, s0T)


def _rw_post_kernel(y_ref, bonus_ref, g_ref, lw_ref, lb_ref, o_ref):
    hm = _head_matrix()
    y = y_ref[...]
    mean = _head_sum(y, hm) * (1.0 / RW_HD)
    d = y - mean
    var = _head_sum(d * d, hm) * (1.0 / RW_HD)
    yn = d * lax.rsqrt(var + RW_GN_EPS) * lw_ref[...] + lb_ref[...]
    o_ref[...] = ((yn + bonus_ref[...]) * g_ref[...]).astype(o_ref.dtype)


def _rw_post(y, bonus, g, lnw, lnb, layer):
    m = y.shape[0]
    tr = min(512, m)
    spec = pl.BlockSpec((tr, RW_W), lambda i: (i, 0))
    vec = pl.BlockSpec((None, 1, RW_W), lambda i: (layer, 0, 0))
    return pl.pallas_call(
        _rw_post_kernel,
        grid=(m // tr,),
        in_specs=[spec, spec, spec, vec, vec],
        out_specs=spec,
        out_shape=jax.ShapeDtypeStruct((m, RW_W), bf16),
        compiler_params=_cparams(("parallel",)),
    )(y, bonus, g, lnw, lnb)


def _rwkv(p, shift0, s0, wts, layer, *, nb, tp, t_valid, rows):
    H, N = RW_HEADS, RW_HD
    kh = LANES // (nb * H)
    assert kh in (1, 2) and kh * nb * H == LANES
    ki_n = N // kh
    shift_pad = jnp.pad(shift0, ((0, 0), (0, 3 * RW_W + LORA_PAD - RW_COLS)))[:, None, :]
    valid = None if t_valid == tp else t_valid
    r, w, k, v, nkk, bb, bonus, g = _rw_pre(
        p, shift_pad, wts["mu_pad"], wts["w0"], wts["w2p"], wts["a0"], wts["a2p"], wts["g2p"],
        wts["kkw"], wts["kaw"], wts["rkw"], layer, nb=nb, tp=tp, rows=rows, valid=valid)

    def key_lanes(x):
        x = x.reshape(nb, tp, H, kh, ki_n)[:, :t_valid]
        return x.transpose(1, 4, 3, 0, 2).reshape(t_valid, ki_n, LANES)

    vT = jnp.broadcast_to(v.reshape(nb, tp, H, 1, N)[:, :t_valid], (nb, t_valid, H, kh, N))
    vT = vT.transpose(1, 4, 3, 0, 2).reshape(t_valid, N, LANES)
    s0T = s0.reshape(nb, H, N, kh, ki_n).transpose(4, 2, 3, 0, 1).reshape(ki_n, N, LANES)
    yT, sT = _rw_scan(key_lanes(r), key_lanes(w), key_lanes(k), key_lanes(bb), key_lanes(nkk), vT, s0T,
                      tt=64)
    y = yT[:, :, :nb * H].reshape(t_valid, N, nb, H).transpose(2, 0, 3, 1).reshape(nb, t_valid, RW_W)
    if t_valid != tp:
        y = jnp.pad(y, ((0, 0), (0, tp - t_valid), (0, 0)))
    y = y.reshape(nb * tp, RW_W)
    s_out = sT.reshape(ki_n, N, kh, nb, H).transpose(3, 4, 1, 2, 0).reshape(nb, H, N, N)
    o = _rw_post(y, bonus, g, wts["lnw"], wts["lnb"], layer)
    return o, s_out


def _kv_prep_kernel(ak_ref, av_ref, aki_ref, kn_ref, k_o, v_o, ki_o):
    gain = kn_ref[...]
    for n in range(AT_KV_HEADS):
        x = ak_ref[:, n * HEAD_DIM:(n + 1) * HEAD_DIM]
        ms = jnp.mean(x * x, axis=-1, keepdims=True)
        k_o[:, n * HEAD_DIM:(n + 1) * HEAD_DIM] = x * lax.rsqrt(ms + NORM_EPS) * gain
    v_o[...] = av_ref[...]
    ki_o[...] = aki_ref[...]


def _kv_prep(p, k_norm, layer):
    m = p.shape[0]
    tr = min(512, m)
    return pl.pallas_call(
        _kv_prep_kernel,
        grid=(m // tr,),
        in_specs=[pl.BlockSpec((tr, KV_W), lambda i: (i, OFF_AK // KV_W)),
                  pl.BlockSpec((tr, KV_W), lambda i: (i, OFF_AV // KV_W)),
                  pl.BlockSpec((tr, IDX_DIM), lambda i: (i, OFF_AKI // IDX_DIM)),
                  pl.BlockSpec((None, 1, HEAD_DIM), lambda i: (layer, 0, 0))],
        out_specs=[pl.BlockSpec((tr, KV_W), lambda i: (i, 0)),
                   pl.BlockSpec((tr, KV_W), lambda i: (i, 0)),
                   pl.BlockSpec((tr, IDX_DIM), lambda i: (i, 0))],
        out_shape=[jax.ShapeDtypeStruct((m, KV_W), f32), jax.ShapeDtypeStruct((m, KV_W), f32),
                   jax.ShapeDtypeStruct((m, IDX_DIM), f32)],
        compiler_params=_cparams(("parallel",)),
    )(p, p, p, k_norm)


def _index_scores(qi, wi_col, keys):
    rws = qi.shape[0]
    qs = jnp.concatenate([qi[:, h * IDX_DIM:(h + 1) * IDX_DIM] for h in range(IDX_HEADS)], axis=0)
    d = jnp.maximum(_dot_nt(qs, keys, precision=HI), 0.0) * wi_col
    s = d[0:rws]
    for h in range(1, IDX_HEADS):
        s = s + d[h * rws:(h + 1) * rws]
    return s


def _wi_column(awi):
    scale = IDX_HEADS ** -0.5 * IDX_DIM ** -0.5
    return jnp.concatenate([awi[:, h:h + 1] for h in range(IDX_HEADS)], axis=0) * scale


def _sortable(score):
    bits = lax.bitcast_convert_type(score, jnp.int32)
    key = jnp.where(bits < 0, bits ^ jnp.int32(0x7FFFFFFF), bits)
    return jnp.where(score == 0.0, 0, key)


def _select_topk(skey, n_sel, n_keys, axis=1):
    one = tuple(1 if a == axis else s for a, s in enumerate(skey.shape))
    nsel = jnp.float32(n_sel)
    int_min = jnp.int32(-2 ** 31)

    def _count(mask):
        ones = mask.astype(f32)
        if axis == 0 and ones.shape[0] % COUNT_ROWS == 0 and ones.shape[0] > COUNT_ROWS:
            ones = jnp.sum(ones.reshape(-1, COUNT_ROWS, ones.shape[1]), axis=0)
        return jnp.sum(ones, axis=axis, keepdims=True)

    zero = jnp.zeros(one, jnp.int32)
    cand = jnp.where(_count(skey >= zero) >= nsel, zero, zero + int_min)

    def bit_step(it, cand):
        trial = cand + jnp.left_shift(jnp.int32(1), 30 - it)
        return jnp.where(_count(skey >= trial) >= nsel, trial, cand)

    tau = lax.fori_loop(0, 31, bit_step, cand)
    gt = skey > tau
    eq = skey == tau
    need = nsel - _count(gt)
    idx = lax.broadcasted_iota(jnp.int32, skey.shape, axis)
    nbits = int(n_keys).bit_length()

    def idx_step(it, x):
        trial = x + jnp.left_shift(jnp.int32(1), nbits - 1 - it)
        ok = (trial <= n_keys) & (_count(eq & (idx < trial)) < need)
        return jnp.where(ok, trial, x)

    surplus = jnp.max(_count(eq) - need) > 0.0
    x = lax.cond(surplus, lambda: lax.fori_loop(0, nbits, idx_step, zero), lambda: zero + n_keys)
    return gt | (eq & (idx <= x))


def _q_heads(aq, gain, n):
    outs = []
    for g in range(AT_GROUP):
        h = n * AT_GROUP + g
        x = aq[:, h * HEAD_DIM:(h + 1) * HEAD_DIM]
        ms = jnp.mean(x * x, axis=-1, keepdims=True)
        outs.append(x * lax.rsqrt(ms + NORM_EPS) * gain)
    return jnp.concatenate(outs, axis=0)


def _dsa_prompt_kernel(aq_ref, aqi_ref, awi_ref, k_ref, v_ref, ki_ref, qn_ref, o_ref,
                       sel_ref, keep_ref, kb_ref, vt_ref, q_ref, m_ref, l_ref, acc_ref, *, t_len, n_sel):
    i = pl.program_id(1)
    qb = Q_BLOCK
    nkb = t_len // qb
    kc = DSA_KEY_CHUNK if t_len % DSA_KEY_CHUNK == 0 else qb

    @pl.when(i == 0)
    def _():
        kb_ref[...] = k_ref[...].astype(bf16)
        for n in range(AT_KV_HEADS):
            for j in range(nkb):
                tile = v_ref[j * qb:(j + 1) * qb, n * HEAD_DIM:(n + 1) * HEAD_DIM].T.astype(bf16)
                lo = (j * qb) % kc
                vt_ref[n, (j * qb) // kc, :, lo:lo + qb] = tile

    qi = aqi_ref[...]
    qs = jnp.concatenate([qi[:, h * IDX_DIM:(h + 1) * IDX_DIM] for h in range(IDX_HEADS)], axis=0)
    qs_hi, qs_lo = _split_bf16(qs)
    wi_t = awi_ref[...].T
    wi_row = jnp.concatenate([wi_t[h:h + 1, :] for h in range(IDX_HEADS)], axis=1)
    wi_row = wi_row * (IDX_HEADS ** -0.5 * IDX_DIM ** -0.5)
    q_pos = i * qb + lax.broadcasted_iota(jnp.int32, (1, qb), 1)

    for j in range(nkb):
        rows = slice(j * qb, (j + 1) * qb)

        @pl.when(j <= i)
        def _():
            k_hi, k_lo = _split_bf16(ki_ref[rows, :])
            d = _dot_nt(k_hi, qs_hi) + (_dot_nt(k_hi, qs_lo) + _dot_nt(k_lo, qs_hi))
            d = jnp.maximum(d, 0.0) * wi_row
            s = d[:, 0:qb]
            for h in range(1, IDX_HEADS):
                s = s + d[:, h * qb:(h + 1) * qb]
            key_pos = j * qb + lax.broadcasted_iota(jnp.int32, (qb, qb), 0)
            s = jnp.where(key_pos <= q_pos, s, NEG_BIG)
            sel_ref[rows, :] = _sortable(s)

        @pl.when(j > i)
        def _():
            sel_ref[rows, :] = _sortable(jnp.full((qb, qb), NEG_BIG, f32))

    def select(width):
        chosen = _select_topk(sel_ref[:width, :], n_sel, width, axis=0)
        key_pos = lax.broadcasted_iota(jnp.int32, (width, qb), 0)
        keep_ref[:width, :] = jnp.where(chosen & (key_pos <= q_pos), 1.0, 0.0)

    n_tier = DSA_TIERS if nkb % DSA_TIERS == 0 and (nkb // DSA_TIERS * qb) % kc == 0 else 1
    per_tier = nkb // n_tier
    for tier in range(n_tier):
        pl.when(i // per_tier == tier)(functools.partial(select, (tier + 1) * per_tier * qb))

    gain = qn_ref[...]
    for h in range(AT_HEADS):
        x = aq_ref[:, h * HEAD_DIM:(h + 1) * HEAD_DIM]
        ms = jnp.mean(x * x, axis=-1, keepdims=True)
        q_ref[h] = (x * lax.rsqrt(ms + NORM_EPS) * gain).astype(bf16)
    m_ref[...] = jnp.full(m_ref.shape, NEG_BIG, f32)
    l_ref[...] = jnp.zeros(l_ref.shape, f32)
    acc_ref[...] = jnp.zeros(acc_ref.shape, f32)

    def key_chunk(c, carry):
        rows = pl.ds(pl.multiple_of(c * kc, kc), kc)
        keep = keep_ref[rows, :] > 0.5
        for n in range(AT_KV_HEADS):
            kn = kb_ref[rows, n * HEAD_DIM:(n + 1) * HEAD_DIM]
            vnt = vt_ref[n, c]
            for g in range(AT_GROUP):
                h = n * AT_GROUP + g
                s = _dot_nt(kn, q_ref[h]) * (HEAD_DIM ** -0.5 * LOG2_E)
                s = jnp.where(keep, s, NEG_BIG)
                m_old = m_ref[h]
                m_new = jnp.maximum(m_old, jnp.max(s, axis=0, keepdims=True))
                alpha = jnp.exp2(m_old - m_new)
                e = jnp.exp2(s - m_new)
                l_ref[h] = alpha * l_ref[h] + jnp.sum(e, axis=0, keepdims=True)
                acc_ref[h] = alpha * acc_ref[h] + _dot(vnt, e.astype(bf16))
                m_ref[h] = m_new
        return carry

    lax.fori_loop(0, ((i + 1) * qb + kc - 1) // kc, key_chunk, 0)
    for h in range(AT_HEADS):
        o_ref[:, h * HEAD_DIM:(h + 1) * HEAD_DIM] = (acc_ref[h] / l_ref[h]).T.astype(o_ref.dtype)


def _dsa_prompt(p, k, v, ki, q_norm, layer, *, nb, t_len):
    n_sel = min(TOPK_MAX, t_len // 4)
    nq = t_len // Q_BLOCK
    kc = DSA_KEY_CHUNK if t_len % DSA_KEY_CHUNK == 0 else Q_BLOCK
    return pl.pallas_call(
        functools.partial(_dsa_prompt_kernel, t_len=t_len, n_sel=n_sel),
        grid=(nb, nq),
        in_specs=[pl.BlockSpec((Q_BLOCK, AT_W), lambda b, i: (b * nq + i, OFF_AQ // AT_W)),
                  pl.BlockSpec((Q_BLOCK, IDX_HEADS * IDX_DIM),
                               lambda b, i: (b * nq + i, OFF_AQI // (IDX_HEADS * IDX_DIM))),
                  pl.BlockSpec((Q_BLOCK, LANES), lambda b, i: (b * nq + i, OFF_AWI // LANES)),
                  pl.BlockSpec((t_len, KV_W), lambda b, i: (b, 0)),
                  pl.BlockSpec((t_len, KV_W), lambda b, i: (b, 0)),
                  pl.BlockSpec((t_len, IDX_DIM), lambda b, i: (b, 0)),
                  pl.BlockSpec((None, 1, HEAD_DIM), lambda b, i: (layer, 0, 0))],
        out_specs=pl.BlockSpec((Q_BLOCK, AT_W), lambda b, i: (b * nq + i, 0)),
        out_shape=jax.ShapeDtypeStruct((nb * t_len, AT_W), bf16),
        scratch_shapes=[pltpu.VMEM((t_len, Q_BLOCK), jnp.int32),
                        pltpu.VMEM((t_len, Q_BLOCK), f32),
                        pltpu.VMEM((t_len, KV_W), bf16),
                        pltpu.VMEM((AT_KV_HEADS, t_len // kc, HEAD_DIM, kc), bf16),
                        pltpu.VMEM((AT_HEADS, Q_BLOCK, HEAD_DIM), bf16),
                        pltpu.VMEM((AT_HEADS, 1, Q_BLOCK), f32),
                        pltpu.VMEM((AT_HEADS, 1, Q_BLOCK), f32),
                        pltpu.VMEM((AT_HEADS, HEAD_DIM, Q_BLOCK), f32)],
        compiler_params=_cparams(("parallel", "arbitrary")),
    )(p, p, p, k, v, ki, q_norm)


def _dsa_sample_score_kernel(pt_ref, aqi_ref, awi_ref, *refs, n_steps, npg, valid):
    page_refs, knew_ref, o_ref = refs[:npg], refs[npg], refs[npg + 1]
    g = pl.program_id(1)
    rws = SAMPLE_ROWS
    ps = page_refs[0].shape[0]
    qi = aqi_ref[...]
    wi_col = _wi_column(awi_ref[...])

    @pl.when(g < n_steps)
    def _():
        keys = jnp.concatenate([r[...] for r in page_refs], axis=0)
        o_ref[...] = _index_scores(qi, wi_col, keys)

    @pl.when(g == n_steps)
    def _():
        keys = jnp.concatenate([knew_ref[...], jnp.zeros((ps - rws, IDX_DIM), f32)], axis=0)
        s = _index_scores(qi, wi_col, keys)
        key_i = lax.broadcasted_iota(jnp.int32, (rws, ps), 1)
        q_i = lax.broadcasted_iota(jnp.int32, (rws, ps), 0)
        o_ref[:, :ps] = jnp.where((key_i <= q_i) & (key_i < valid), s, NEG_BIG)
        if npg > 1:
            o_ref[:, ps:] = jnp.full((rws, (npg - 1) * ps), NEG_BIG, f32)


def _page_specs(n_pages, npg, page_shape, layer):
    zeros = (0,) * len(page_shape)

    def spec(j):
        return pl.BlockSpec((None, None) + tuple(page_shape),
                            lambda b, g, pt: (layer, pt[b, jnp.minimum(g * npg + j, n_pages - 1)]) + zeros)
    return [spec(j) for j in range(npg)]


def _dsa_sample_scores(p, ki_new, cache_kidx, page_table, layer, *, nb, valid):
    n_pages = page_table.shape[1]
    ps = cache_kidx.shape[2]
    rws = SAMPLE_ROWS
    npg = min(SAMPLE_PAGES, n_pages)
    assert n_pages % npg == 0
    n_steps = n_pages // npg
    grid_spec = pltpu.PrefetchScalarGridSpec(
        num_scalar_prefetch=1,
        grid=(nb, n_steps + 1),
        in_specs=[pl.BlockSpec((rws, IDX_HEADS * IDX_DIM),
                               lambda b, g, pt: (b, OFF_AQI // (IDX_HEADS * IDX_DIM))),
                  pl.BlockSpec((rws, LANES), lambda b, g, pt: (b, OFF_AWI // LANES))]
        + _page_specs(n_pages, npg, (ps, IDX_DIM), layer)
        + [pl.BlockSpec((rws, IDX_DIM), lambda b, g, pt: (b, 0))],
        out_specs=pl.BlockSpec((None, rws, npg * ps), lambda b, g, pt: (b, 0, g)),
    )
    return pl.pallas_call(
        functools.partial(_dsa_sample_score_kernel, n_steps=n_steps, npg=npg, valid=valid),
        grid_spec=grid_spec,
        out_shape=jax.ShapeDtypeStruct((nb, rws, (n_steps + 1) * npg * ps), f32),
        compiler_params=_cparams(("parallel", "arbitrary")),
    )(page_table, p, p, *([cache_kidx] * npg), ki_new)


def _dsa_sample_select_kernel(s_ref, o_ref, *, n_sel, n_keys):
    chosen = _select_topk(_sortable(s_ref[...]), n_sel, n_keys)
    o_ref[...] = jnp.where(chosen & (s_ref[...] > 0.5 * NEG_BIG), 1.0, 0.0)


def _dsa_sample_select(scores, n_sel):
    nb, rws, n_keys = scores.shape
    spec = pl.BlockSpec((None, rws, n_keys), lambda b: (b, 0, 0))
    return pl.pallas_call(
        functools.partial(_dsa_sample_select_kernel, n_sel=n_sel, n_keys=n_keys),
        grid=(nb,),
        in_specs=[spec],
        out_specs=spec,
        out_shape=jax.ShapeDtypeStruct(scores.shape, f32),
        compiler_params=_cparams(("parallel",)),
    )(scores)


def _dsa_sample_attn_kernel(pt_ref, aq_ref, keep_ref, *refs, n_steps, npg):
    kpage_refs, vpage_refs = refs[:npg], refs[npg:2 * npg]
    knew_ref, vnew_ref, qn_ref, o_ref, q_ref, m_ref, l_ref, acc_ref = refs[2 * npg:]
    g = pl.program_id(1)
    rws = SAMPLE_ROWS
    ps = kpage_refs[0].shape[0] // AT_KV_HEADS

    @pl.when(g == 0)
    def _():
        m_ref[...] = jnp.full(m_ref.shape, NEG_BIG, f32)
        l_ref[...] = jnp.zeros(l_ref.shape, f32)
        acc_ref[...] = jnp.zeros(acc_ref.shape, f32)
        for n in range(AT_KV_HEADS):
            q_ref[n] = _q_heads(aq_ref[...], qn_ref[...], n).astype(bf16)

    def step(k_of, v_of, keep):
        keep4 = jnp.concatenate([keep] * AT_GROUP, axis=0) > 0.5
        for n in range(AT_KV_HEADS):
            s = _dot_nt(q_ref[n], k_of(n)) * (HEAD_DIM ** -0.5)
            s = jnp.where(keep4, s, NEG_BIG)
            m_old = m_ref[n]
            m_new = jnp.maximum(m_old, jnp.max(s, axis=1, keepdims=True))
            alpha = jnp.exp(m_old - m_new)
            e = jnp.where(keep4, jnp.exp(s - m_new), 0.0)
            l_ref[n] = alpha * l_ref[n] + jnp.sum(e, axis=1, keepdims=True)
            acc_ref[n] = alpha * acc_ref[n] + _dot(e.astype(bf16), v_of(n))
            m_ref[n] = m_new

    def page_head(pages, n):
        rows = pl.ds(n, ps, stride=AT_KV_HEADS)
        return jnp.concatenate([pg[rows, :].astype(bf16) for pg in pages], axis=0)

    def new_head(block, n):
        return block[:, n * HEAD_DIM:(n + 1) * HEAD_DIM].astype(bf16)

    @pl.when(g < n_steps)
    def _():
        step(functools.partial(page_head, kpage_refs), functools.partial(page_head, vpage_refs), keep_ref[...])

    @pl.when(g == n_steps)
    def _():
        pad = jnp.zeros((ps - rws, KV_W), f32)
        knew = jnp.concatenate([knew_ref[...], pad], axis=0)
        vnew = jnp.concatenate([vnew_ref[...], pad], axis=0)
        step(functools.partial(new_head, knew), functools.partial(new_head, vnew), keep_ref[:, :ps])
        for n in range(AT_KV_HEADS):
            o = acc_ref[n] / l_ref[n]
            for gq in range(AT_GROUP):
                h = n * AT_GROUP + gq
                o_ref[:, h * HEAD_DIM:(h + 1) * HEAD_DIM] = o[gq * rws:(gq + 1) * rws].astype(o_ref.dtype)


def _dsa_sample_attn(p, keep, cache_k, cache_v, k_new, v_new, q_norm, page_table, layer, *, nb):
    n_pages = page_table.shape[1]
    ps = cache_k.shape[2] // AT_KV_HEADS
    rws = SAMPLE_ROWS
    npg = min(SAMPLE_PAGES, n_pages)
    n_steps = n_pages // npg
    new =pl.BlockSpec((rws, KV_W), lambda b, g, pt: (b, 0))
    grid_spec = pltpu.PrefetchScalarGridSpec(
        num_scalar_prefetch=1,
        grid=(nb, n_steps + 1),
        in_specs=[pl.BlockSpec((rws, AT_W), lambda b, g, pt: (b, OFF_AQ // AT_W)),
                  pl.BlockSpec((None, rws, npg * ps), lambda b, g, pt: (b, 0, g))]
        + 2 * _page_specs(n_pages, npg, (ps * AT_KV_HEADS, HEAD_DIM), layer)
        + [new, new, pl.BlockSpec((None, 1, HEAD_DIM), lambda b, g, pt: (layer, 0, 0))],
        out_specs=pl.BlockSpec((rws, AT_W), lambda b, g, pt: (b, 0)),
        scratch_shapes=[pltpu.VMEM((AT_KV_HEADS, AT_GROUP * rws, HEAD_DIM), bf16),
                        pltpu.VMEM((AT_KV_HEADS, AT_GROUP * rws, 1), f32),
                        pltpu.VMEM((AT_KV_HEADS, AT_GROUP * rws, 1), f32),
                        pltpu.VMEM((AT_KV_HEADS, AT_GROUP * rws, HEAD_DIM), f32)],
    )
    return pl.pallas_call(
        functools.partial(_dsa_sample_attn_kernel, n_steps=n_steps, npg=npg),
        grid_spec=grid_spec,
        out_shape=jax.ShapeDtypeStruct((nb * rws, AT_W), bf16),
        compiler_params=_cparams(("parallel", "arbitrary")),
    )(page_table, p, keep, *([cache_k] * npg), *([cache_v] * npg), k_new, v_new, q_norm)


def _prep_weights(w_in, rwkv_mu, rwkv_w2, rwkv_a2, rwkv_g2, w_out, w_gate, w_up, w_down):
    depth, d, _ = w_in.shape
    hg, rw, at = jnp.split(w_in.astype(bf16), [HG_COLS, HG_COLS + RW_COLS], axis=-1)
    aq, ak, av, aqi, aki, awi = jnp.split(at, np.cumsum([AT_W, KV_W, KV_W, IDX_HEADS * IDX_DIM, IDX_DIM])
                                          .tolist(), axis=-1)
    r, k, v, lora = jnp.split(rw, [RW_W, 2 * RW_W, 3 * RW_W], axis=-1)
    z = lambda n: jnp.zeros((depth, d, n), bf16)
    w_in_p = jnp.concatenate(
        [hg, aq, aqi, ak, av, aki, awi, z(OFF_LORA - OFF_AWI - IDX_HEADS), lora, z(LORA_PAD - RW_LORA), r, k, v],
        axis=-1)
    assert w_in_p.shape[-1] == IN_PAD
    mu_r, mu_l = rwkv_mu[:, :3 * RW_W], rwkv_mu[:, 3 * RW_W:]
    mu_pad = jnp.concatenate([mu_r, mu_l, jnp.zeros((depth, LORA_PAD - RW_LORA), f32)], axis=-1)[:, None, :]
    zl = lambda n: jnp.zeros((depth, n, RW_W), bf16)
    w2p = jnp.concatenate([rwkv_w2.astype(bf16), zl(LORA_PAD - RW_DECAY_LORA)], axis=1)
    a2p = jnp.concatenate([zl(RW_DECAY_LORA), rwkv_a2.astype(bf16),
                           zl(LORA_PAD - RW_DECAY_LORA - RW_AAA_LORA)], axis=1)
    g2p = jnp.concatenate([zl(RW_DECAY_LORA + RW_AAA_LORA), rwkv_g2.astype(bf16), zl(LORA_PAD - RW_LORA)], axis=1)
    return dict(w_in=w_in_p, mu_pad=mu_pad, w2p=w2p, a2p=a2p, g2p=g2p,
                w_out=w_out, w_gate=w_gate, w_up=w_up, w_down=w_down.astype(bf16))


def _shift_row(p_row):
    return jnp.concatenate([p_row[..., OFF_R:OFF_R + 3 * RW_W], p_row[..., OFF_LORA:OFF_LORA + RW_LORA]], axis=-1)


def _k_tile(f):
    half = f // 2
    return half if f % 2 == 0 and half % LANES == 0 else f


def _layer(x, layer, wts, *, nb, tp, t_valid, hg_s0, rw_s0, shift0, attend):
    h = _rmsnorm(x, wts["ln1"], layer)
    p = _matmul(h, wts["w_in"], layer, tm=1024, tn=512, tk=h.shape[1])
    valid = None if t_valid == tp else t_valid
    rows = min(256, tp)
    o_hg, hg_s = _gla(p, wts["lbs"], wts["hgrn_norm"], hg_s0, layer, nb=nb, tp=tp, rows=rows, valid=valid)
    o_rw, rw_s = _rwkv(p, shift0, rw_s0, wts, layer, nb=nb, tp=tp, t_valid=t_valid, rows=rows)
    k, v, ki = _kv_prep(p, wts["k_norm"], layer)
    o_at = attend(p, k, v, ki)
    mix = jnp.concatenate([o_hg, o_rw, o_at], axis=-1)
    x = _matmul(mix, wts["w_out"], layer, tm=1024, tn=512, tk=mix.shape[1], res=x)
    h2 = _rmsnorm(x, wts["ln2"], layer)
    act = _swiglu(h2, wts["w_gate"], wts["w_up"], layer, tm=2048, tn=256)
    x = _matmul(act, wts["w_down"], layer, tm=1024, tn=512, tk=_k_tile(act.shape[1]), res=x)
    shift = _shift_row(p.reshape(nb, tp, IN_PAD)[:, t_valid - 1])
    return x, (k, v, ki, hg_s, rw_s, shift)


def kernel(x_prompt, x_sample, cache_k, cache_v, cache_kidx, state_hgrn, state_rwkv, state_shift, page_table,
           ln1, w_in, hgrn_lb, hgrn_norm, rwkv_mu, rwkv_w0, rwkv_w2, rwkv_a0, rwkv_a2, rwkv_g2, rwkv_kk,
           rwkv_ka, rwkv_rk, rwkv_lnx_w, rwkv_lnx_b, q_norm, k_norm, w_out, ln2, w_gate, w_up, w_down):
    depth = w_in.shape[0]
    B, T, D = x_prompt.shape
    DB, DS, _ = x_sample.shape
    n_pool, page_size = cache_k.shape[1], cache_k.shape[2]
    ck = cache_k.reshape(depth, n_pool, page_size * AT_KV_HEADS, HEAD_DIM)
    cv = cache_v.reshape(depth, n_pool, page_size * AT_KV_HEADS, HEAD_DIM)
    past = page_table.shape[1] * page_size

    wts = _prep_weights(w_in, rwkv_mu, rwkv_w2, rwkv_a2, rwkv_g2, w_out, w_gate, w_up, w_down)
    lb_p = jax.nn.softmax(hgrn_lb.astype(f32), axis=0)
    row = lambda a: a.astype(f32).reshape(depth, 1, -1)
    wts.update(lbs=(jnp.cumsum(lb_p, axis=0) - lb_p[0:1])[:, None, :], hgrn_norm=row(hgrn_norm),
               ln1=row(ln1), ln2=row(ln2), w0=row(rwkv_w0), a0=row(rwkv_a0), kkw=row(rwkv_kk),
               kaw=row(rwkv_ka), rkw=row(rwkv_rk), lnw=row(rwkv_lnx_w), lnb=row(rwkv_lnx_b),
               q_norm=row(q_norm), k_norm=row(k_norm))

    xp = x_prompt.reshape(B * T, D)
    outs_p = []
    zeros_hg = jnp.zeros((B, HG_HEADS, HG_DK, HG_DV), f32)
    zeros_rw = jnp.zeros((B, RW_HEADS, RW_HD, RW_HD), f32)
    zeros_sh = jnp.zeros((B, RW_COLS), f32)
    for l in range(depth):
        attend = lambda p, k, v, ki, l=l: _dsa_prompt(p, k, v, ki, wts["q_norm"], l, nb=B, t_len=T)
        xp, st = _layer(xp, l, wts, nb=B, tp=T, t_valid=T, hg_s0=zeros_hg, rw_s0=zeros_rw, shift0=zeros_sh,
                        attend=attend)
        outs_p.append(st)

    TP = SAMPLE_ROWS
    xs = jnp.pad(x_sample, ((0, 0), (0, TP - DS), (0, 0))).reshape(DB * TP, D)
    n_sel_s = min(TOPK_MAX, (past + DS) // 4)
    outs_s = []
    for l in range(depth):
        def attend(p, k, v, ki, l=l):
            scores = _dsa_sample_scores(p, ki, cache_kidx, page_table, l, nb=DB, valid=DS)
            keep = _dsa_sample_select(scores, n_sel_s)
            return _dsa_sample_attn(p, keep, ck, cv, k, v, wts["q_norm"], page_table, l, nb=DB)

        xs, st = _layer(xs, l, wts, nb=DB, tp=TP, t_valid=DS, hg_s0=state_hgrn[l], rw_s0=state_rwkv[l],
                        shift0=state_shift[l], attend=attend)
        outs_s.append(st)

    def stack(outs, i):
        return jnp.stack([o[i] for o in outs])

    k_p = stack(outs_p, 0).reshape(depth, B, T, AT_KV_HEADS, HEAD_DIM)
    v_p = stack(outs_p, 1).reshape(depth, B, T, AT_KV_HEADS, HEAD_DIM)
    ki_p = stack(outs_p, 2).reshape(depth, B, T, IDX_DIM)
    cut = lambda a, w: a.reshape(depth, DB, TP, *w)[:, :, :DS]
    k_s = cut(stack(outs_s, 0), (AT_KV_HEADS, HEAD_DIM))
    v_s = cut(stack(outs_s, 1), (AT_KV_HEADS, HEAD_DIM))
    ki_s = cut(stack(outs_s, 2), (IDX_DIM,))
    y_p = xp.reshape(B, T, D)
    y_s = xs.reshape(DB, TP, D)[:, :DS]
    return (y_p, y_s, k_p, v_p, ki_p, stack(outs_p, 3), stack(outs_p, 4), stack(outs_p, 5),
            k_s, v_s, ki_s, stack(outs_s, 3), stack(outs_s, 4), stack(outs_s, 5))
```

```python
import functools

import jax
import jax.numpy as jnp
import numpy as np
from jax import lax
from jax.experimental import pallas as pl
from jax.experimental.pallas import tpu as pltpu

f32 = jnp.float32
bf16 = jnp.bfloat16

HG_HEADS, HG_DK, HG_DV = 8, 128, 128
HG_W = HG_HEADS * HG_DV
EXP_CLIP = 60.0
RW_HEADS, RW_HD = 16, 64
RW_W = RW_HEADS * RW_HD
RW_DECAY_LORA, RW_AAA_LORA, RW_GATE_LORA = 64, 64, 160
RW_LORA = RW_DECAY_LORA + RW_AAA_LORA + RW_GATE_LORA
RW_GN_EPS = 64e-5
RW_COLS = 3 * RW_W + RW_LORA
AT_HEADS, AT_KV_HEADS, HEAD_DIM = 16, 4, 128
AT_GROUP = AT_HEADS // AT_KV_HEADS
AT_W = AT_HEADS * HEAD_DIM
KV_W = AT_KV_HEADS * HEAD_DIM
IDX_HEADS, IDX_DIM = 8, 128
TOPK_MAX = 256
Q_BLOCK = 128
NEG_BIG = -1e30
LOG2_E = 1.4426950408889634
NORM_EPS = 1e-6
HG_COLS = 2 * HG_HEADS * HG_DK + 2 * HG_W
AT_COLS = AT_W + 2 * KV_W + IDX_HEADS * IDX_DIM + IDX_DIM + IDX_HEADS

LANES = 128
VMEM_LIMIT = 56 * 1024 * 1024
SAMPLE_ROWS = 16
GLA_CHUNK = 128
GLA_HEADS = 8
GLA_SUB = 16
GLA_SPAN = 60.0
RW_UNROLL = 32
DSA_TIERS = 4
COUNT_ROWS = 64
DSA_KEY_CHUNK = 256
SAMPLE_PAGES = 16

LORA_PAD = 512
OFF_HG = 0
OFF_AQ = HG_COLS
OFF_AQI = OFF_AQ + AT_W
OFF_AK = OFF_AQI + IDX_HEADS * IDX_DIM
OFF_AV = OFF_AK + KV_W
OFF_AKI = OFF_AV + KV_W
OFF_AWI = OFF_AKI + IDX_DIM
OFF_LORA = 8704
OFF_R = OFF_LORA + LORA_PAD
OFF_K = OFF_R + RW_W
OFF_V = OFF_K + RW_W
IN_PAD = OFF_V + RW_W


def _cparams(sem):
    return pltpu.CompilerParams(dimension_semantics=sem, vmem_limit_bytes=VMEM_LIMIT)


def _sigmoid(x):
    return jax.nn.sigmoid(x)


def _dot(a, b, precision=None):
    return jnp.dot(a, b, preferred_element_type=f32, precision=precision)


def _dot_nt(a, b, precision=None):
    return lax.dot_general(a, b, (((1,), (1,)), ((), ())), preferred_element_type=f32, precision=precision)


def _dot_tn(a, b, precision=None):
    return lax.dot_general(a, b, (((0,), (0,)), ((), ())), preferred_element_type=f32, precision=precision)


HI = lax.Precision.HIGHEST


def _split_bf16(x):
    hi = x.astype(bf16)
    return hi, (x - hi.astype(f32)).astype(bf16)


def _rmsnorm_kernel(x_ref, g_ref, o_ref):
    x = x_ref[...]
    ms = jnp.mean(x * x, axis=-1, keepdims=True)
    o_ref[...] = (x * lax.rsqrt(ms + NORM_EPS) * g_ref[...]).astype(o_ref.dtype)


def _rmsnorm(x, g, layer):
    m, d = x.shape
    tr = min(256, m)
    return pl.pallas_call(
        _rmsnorm_kernel,
        grid=(m // tr,),
        in_specs=[pl.BlockSpec((tr, d), lambda i: (i, 0)),
                  pl.BlockSpec((None, 1, d), lambda i: (layer, 0, 0))],
        out_specs=pl.BlockSpec((tr, d), lambda i: (i, 0)),
        out_shape=jax.ShapeDtypeStruct((m, d), bf16),
        compiler_params=_cparams(("parallel",)),
    )(x, g)


def _mm_kernel(*refs, nk, has_res):
    if has_res:
        a_ref, b_ref, r_ref, o_ref = refs[:4]
    else:
        a_ref, b_ref, o_ref = refs[:3]
        r_ref = None
    part = _dot(a_ref[...], b_ref[...].astype(bf16))
    if nk == 1:
        o_ref[...] = (part + r_ref[...]) if has_res else part
        return
    acc_ref = refs[-1]
    k = pl.program_id(2)

    @pl.when(k == 0)
    def _():
        acc_ref[...] = part

    @pl.when(k > 0)
    def _():
        acc_ref[...] += part

    @pl.when(k == nk - 1)
    def _():
        o_ref[...] = (acc_ref[...] + r_ref[...]) if has_res else acc_ref[...]


def _matmul(a, w, layer, *, tm, tn, tk, res=None):
    m, kdim = a.shape
    n = w.shape[-1]
    tm = min(tm, m)
    tn = min(tn, n)
    nk = kdim // tk
    assert m % tm == 0 and n % tn == 0 and kdim % tk == 0
    in_specs = [pl.BlockSpec((tm, tk), lambda i, j, k: (i, k)),
                pl.BlockSpec((None, tk, tn), lambda i, j, k: (layer, k, j))]
    args = [a, w]
    if res is not None:
        in_specs.append(pl.BlockSpec((tm, tn), lambda i, j, k: (i, j)))
        args.append(res)
    scratch = [pltpu.VMEM((tm, tn), f32)] if nk > 1 else []
    return pl.pallas_call(
        functools.partial(_mm_kernel, nk=nk, has_res=res is not None),
        grid=(m // tm, n // tn, nk),
        in_specs=in_specs,
        out_specs=pl.BlockSpec((tm, tn), lambda i, j, k: (i, j)),
        out_shape=jax.ShapeDtypeStruct((m, n), f32),
        scratch_shapes=scratch,
        compiler_params=_cparams(("parallel", "parallel", "arbitrary")),
    )(*args)


def _swiglu_kernel(a_ref, wg_ref, wu_ref, o_ref):
    a = a_ref[...]
    g = _dot(a, wg_ref[...].astype(bf16))
    u = _dot(a, wu_ref[...].astype(bf16))
    o_ref[...] = (g * _sigmoid(g) * u).astype(o_ref.dtype)


def _resident_rows_spec(tm, width, index_map):
    return pl.BlockSpec((tm, width), index_map, pipeline_mode=pl.Buffered(1))


def _swiglu(a, wg, wu, layer, *, tm, tn):
    m, d = a.shape
    n = wg.shape[-1]
    tm = min(tm, m)
    assert m % tm == 0 and n % tn == 0
    wspec = pl.BlockSpec((None, d, tn), lambda i, j: (layer, 0, j))
    return pl.pallas_call(
        _swiglu_kernel,
        grid=(m // tm, n // tn),
        in_specs=[_resident_rows_spec(tm, d, lambda i, j: (i, 0)), wspec, wspec],
        out_specs=pl.BlockSpec((tm, tn), lambda i, j: (i, j)),
        out_shape=jax.ShapeDtypeStruct((m, n), bf16),
        compiler_params=_cparams(("parallel", "parallel")),
    )(a, wg, wu)


def _gla_kernel(pq_ref, pf_ref, pi_ref, pg_ref, lb_ref, g_ref, s0_ref, o_ref, sout_ref, st_ref,
                *, rows, valid):
    i = pl.program_id(2)
    chunk, sub = GLA_CHUNK, GLA_SUB
    live_rows = min(rows, chunk)
    heads = range(GLA_HEADS)
    lanes = [slice(hh * LANES, (hh + 1) * LANES) for hh in heads]

    @pl.when(i == 0)
    def _():
        for hh in heads:
            st_ref[hh] = s0_ref[hh].T

    n_live = live_rows if valid is None else min(valid, live_rows)
    nsb = -(-n_live // sub)
    crow = lax.broadcasted_iota(jnp.int32, (chunk, 1), 0)
    tri = (lax.broadcasted_iota(jnp.int32, (chunk, chunk), 0)
           >= lax.broadcasted_iota(jnp.int32, (chunk, chunk), 1)).astype(f32)
    sub_row = lax.broadcasted_iota(jnp.int32, (sub, 1), 0)

    def padded(x):
        if live_rows == chunk:
            return x
        return jnp.concatenate([x, jnp.zeros((chunk - live_rows, x.shape[1]), x.dtype)], axis=0)

    tail = chunk - nsb * sub
    zero_row = jnp.zeros((1, HG_DK), f32)

    def one_chunk(c, carry):
        rs = pl.ds(pl.multiple_of(c * live_rows, live_rows), live_rows)

        def prelude(hh):
            pq = padded(pq_ref[rs, lanes[hh]])
            fr = padded(pf_ref[rs, lanes[hh]])
            v = padded(pi_ref[rs, lanes[hh]])
            lb = lb_ref[:, lanes[hh]]
            q = pq * _sigmoid(pq) * (HG_DK ** -0.5)
            log_sig = jnp.minimum(fr, 0.0) - jnp.log1p(jnp.exp(-jnp.abs(fr)))
            log_f = log_sig + jnp.log1p(lb * jnp.exp(jnp.minimum(-fr, EXP_CLIP)))
            k = (1.0 - lb) * _sigmoid(-fr)
            if valid is not None or live_rows != chunk:
                live = crow < live_rows
                if valid is not None:
                    live = live & (i * rows + c * live_rows + crow < valid)
                log_f = jnp.where(live, log_f, 0.0)
                k = jnp.where(live, k, 0.0)
            b = _dot(tri, log_f, precision=HI)
            ref_rows = [zero_row if ib == 0 else b[ib * sub - 1:ib * sub] for ib in range(nsb)]
            return q, k, v, b, ref_rows

        pre = [prelude(hh) for hh in heads]

        def scores(hh, with_diagonal):
            q, k, v, b, ref_rows = pre[hh]
            a_rows = []
            for ib in range(nsb):
                lo = ib * sub
                hi = lo + sub if with_diagonal else lo
                if hi == 0:
                    a_rows.append(jnp.zeros((sub, chunk), f32))
                    continue
                qt = (q[lo:lo + sub] * jnp.exp(b[lo:lo + sub] - ref_rows[ib])).astype(bf16)
                grow = jnp.minimum(ref_rows[ib] - b, GLA_SPAN if with_diagonal else 0.0)
                kt = jnp.where(crow < hi, k * jnp.exp(grow), 0.0).astype(bf16)
                a = _dot_nt(qt, kt)
                if with_diagonal:
                    key = lax.broadcasted_iota(jnp.int32, (sub, chunk), 1)
                    a = jnp.where(key <= lo + sub_row, a, 0.0)
                a_rows.append(a)
            if tail:
                a_rows.append(jnp.zeros((tail, chunk), f32))
            return jnp.concatenate(a_rows, axis=0)

        def diagonal_terms(hh):
            q, k, v, b, _ = pre[hh]
            d_rows = []
            for ib in range(nsb):
                lo = ib * sub
                q_i, b_i = q[lo:lo + sub], b[lo:lo + sub]
                d_i = jnp.zeros((sub, HG_DV), f32)
                for s in range(sub):
                    gs = lo + s
                    term = q_i * jnp.exp(jnp.minimum(b_i - b[gs:gs + 1], 0.0)) * k[gs:gs + 1]
                    a_col = jnp.sum(term, axis=1, keepdims=True)
                    d_i = d_i + jnp.where(sub_row >= s, a_col, 0.0) * v[gs:gs + 1]
                d_rows.append(d_i)
            if tail:
                d_rows.append(jnp.zeros((tail, HG_DV), f32))
            return jnp.concatenate(d_rows, axis=0)

        def factored():
            return tuple(x for hh in heads for x in (scores(hh, True), jnp.zeros((chunk, HG_DV), f32)))

        def term_by_term():
            return tuple(x for hh in heads for x in (scores(hh, False), diagonal_terms(hh)))

        span = zero_row
        for q, k, v, b, ref_rows in pre:
            for ib in range(nsb):
                span = jnp.maximum(span, ref_rows[ib] - b[(ib + 1) * sub - 1:(ib + 1) * sub])
        mats = lax.cond(jnp.max(span) <= GLA_SPAN, factored, term_by_term)

        for hh in heads:
            q, k, v, b, _ = pre[hh]
            st = st_ref[hh]
            v16 = v.astype(bf16)
            o = _dot_nt((q * jnp.exp(b)).astype(bf16), st.astype(bf16))
            o = o + _dot(mats[2 * hh].astype(bf16), v16) + mats[2 * hh + 1]
            b_last = b[chunk - 1:chunk]
            kd = (k * jnp.exp(b_last - b)).astype(bf16)
            st_ref[hh] = st * jnp.exp(b_last) + _dot_tn(v16, kd)
            o = o[:live_rows]
            ms = jnp.mean(o * o, axis=-1, keepdims=True)
            y = o * lax.rsqrt(ms + NORM_EPS) * g_ref[:, lanes[hh]] * _sigmoid(pg_ref[rs, lanes[hh]])
            o_ref[rs, lanes[hh]] = y.astype(o_ref.dtype)
        return carry

    lax.fori_loop(0, rows // live_rows, one_chunk, 0)

    @pl.when(i == pl.num_programs(2) - 1)
    def _():
        for hh in heads:
            sout_ref[hh] = st_ref[hh].T


def _gla(p, lb, gain, s0, layer, *, nb, tp, rows, valid):
    nblk = tp // rows
    H = HG_HEADS
    G = GLA_HEADS
    assert H % G == 0
    ng = H // G
    wide = G * LANES

    def pspec(part):
        return pl.BlockSpec((rows, wide), lambda b, h, i: (b * nblk + i, part * ng + h))

    return pl.pallas_call(
        functools.partial(_gla_kernel, rows=rows, valid=valid),
        grid=(nb, ng, nblk),
        in_specs=[pspec(0), pspec(1), pspec(2), pspec(3),
                  pl.BlockSpec((None, 1, wide), lambda b, h, i: (layer, 0, h)),
                  pl.BlockSpec((None, 1, wide), lambda b, h, i: (layer, 0, h)),
                  pl.BlockSpec((None, G, HG_DK, HG_DV), lambda b, h, i: (b, h, 0, 0))],
        out_specs=[pl.BlockSpec((rows, wide), lambda b, h, i: (b * nblk + i, h)),
                   pl.BlockSpec((None, G, HG_DK, HG_DV), lambda b, h, i: (b, h, 0, 0))],
        out_shape=[jax.ShapeDtypeStruct((nb * tp, HG_W), bf16),
                   jax.ShapeDtypeStruct((nb, H, HG_DK, HG_DV), f32)],
        scratch_shapes=[pltpu.VMEM((G, HG_DV, HG_DK), f32)],
        compiler_params=_cparams(("parallel", "parallel", "arbitrary")),
    )(p, p, p, p, lb, gain, s0)


def _head_sum(x, hm):
    cols = [_dot(x[:, c * LANES:(c + 1) * LANES], hm, precision=HI) for c in range(RW_W // LANES)]
    return jnp.concatenate(cols, axis=1)


def _head_matrix():
    r = lax.broadcasted_iota(jnp.int32, (LANES, LANES), 0) // RW_HD
    c = lax.broadcasted_iota(jnp.int32, (LANES, LANES), 1) // RW_HD
    return (r == c).astype(f32)


def _rw_pre_kernel(pr_ref, pk_ref, pv_ref, pl_ref, sr_ref, sk_ref, sv_ref, sl_ref,
                   mr_ref, mk_ref, mv_ref, ml_ref, w0_ref, w2_ref, a0_ref, a2_ref, g2_ref,
                   kkw_ref, kaw_ref, rkw_ref,
                   r_o, w_o, k_o, v_o, nkk_o, b_o, bonus_o, g_o,
                   cr_ref, ck_ref, cv_ref, cl_ref, *, rows, valid):
    i = pl.program_id(1)

    @pl.when(i == 0)
    def _():
        cr_ref[...] = sr_ref[...]
        ck_ref[...] = sk_ref[...]
        cv_ref[...] = sv_ref[...]
        cl_ref[...] = sl_ref[...]

    def mixed(p_ref, c_ref, m_ref):
        cur = p_ref[...]
        rolled = pltpu.roll(cur, 1, axis=0)
        rowid = lax.broadcasted_iota(jnp.int32, cur.shape, 0)
        prev = jnp.where(rowid == 0, c_ref[...], rolled)
        c_ref[...] = cur[rows - 1:rows, :]
        return cur + (prev - cur) * m_ref[...]

    r = mixed(pr_ref, cr_ref, mr_ref)
    k = mixed(pk_ref, ck_ref, mk_ref)
    v = mixed(pv_ref, cv_ref, mv_ref)
    xl = mixed(pl_ref, cl_ref, ml_ref)

    zw = w0_ref[...] + _dot(jnp.tanh(xl).astype(bf16), w2_ref[...])
    w = jnp.minimum(zw, 0.0) - jnp.log1p(jnp.exp(-jnp.abs(zw))) - 0.5
    decay = jnp.exp(-jnp.exp(w))
    a = _sigmoid(a0_ref[...] + _dot(xl.astype(bf16), a2_ref[...]))
    g = _dot(_sigmoid(xl).astype(bf16), g2_ref[...])

    hm = _head_matrix()
    kk = k * kkw_ref[...]
    kk = kk * lax.rsqrt(jnp.maximum(_head_sum(kk * kk, hm), 1e-24))
    k2 = k * (1.0 + (a - 1.0) * kaw_ref[...])
    bonus = _head_sum(r * k2 * rkw_ref[...], hm) * v
    nkk = -kk
    bb = kk * a
    if valid is not None:
        tok = i * rows + lax.broadcasted_iota(jnp.int32, (rows, 1), 0)
        live = tok < valid
        decay = jnp.where(live, decay, 1.0)
        k2 = jnp.where(live, k2, 0.0)
        v = jnp.where(live, v, 0.0)
        nkk = jnp.where(live, nkk, 0.0)
        bb = jnp.where(live, bb, 0.0)
    r_o[...] = r
    w_o[...] = decay
    k_o[...] = k2
    v_o[...] = v
    nkk_o[...] = nkk
    b_o[...] = bb
    bonus_o[...] = bonus
    g_o[...] = g


def _rw_pre(p, shift_pad, mu_pad, w0, w2p, a0, a2p, g2p, kkw, kaw, rkw, layer, *, nb, tp, rows, valid):
    nblk = tp // rows
    W = RW_W

    def pspec(off, width):
        return pl.BlockSpec((rows, width), lambda b, i: (b * nblk + i, off // width))

    def sspec(off, width):
        return pl.BlockSpec((None, 1, width), lambda b, i: (b, 0, off // width))

    def mspec(off, width):
        return pl.BlockSpec((None, 1, width), lambda b, i: (layer, 0, off // width))

    def vec():
        return pl.BlockSpec((None, 1, W), lambda b, i: (layer, 0, 0))

    def lora():
        return pl.BlockSpec((None, LORA_PAD, W), lambda b, i: (layer, 0, 0))

    out_spec = pl.BlockSpec((rows, W), lambda b, i: (b * nblk + i, 0))
    out_shape = jax.ShapeDtypeStruct((nb * tp, W), f32)
    return pl.pallas_call(
        functools.partial(_rw_pre_kernel, rows=rows, valid=valid),
        grid=(nb, nblk),
        in_specs=[pspec(OFF_R, W), pspec(OFF_K, W), pspec(OFF_V, W), pspec(OFF_LORA, LORA_PAD),
                  sspec(0, W), sspec(W, W), sspec(2 * W, W), sspec(3 * W, LORA_PAD),
                  mspec(0, W), mspec(W, W), mspec(2 * W, W), mspec(3 * W, LORA_PAD),
                  vec(), lora(), vec(), lora(), lora(), vec(), vec(), vec()],
        out_specs=[out_spec] * 8,
        out_shape=[out_shape] * 8,
        scratch_shapes=[pltpu.VMEM((1, W), f32), pltpu.VMEM((1, W), f32), pltpu.VMEM((1, W), f32),
                        pltpu.VMEM((1, LORA_PAD), f32)],
        compiler_params=_cparams(("parallel", "arbitrary")),
    )(p, p, p, p, shift_pad, shift_pad, shift_pad, shift_pad, mu_pad, mu_pad, mu_pad, mu_pad,
      w0, w2p, a0, a2p, g2p, kkw, kaw, rkw)


def _rw_scan_kernel(r_ref, w_ref, k_ref, b_ref, nkk_ref, v_ref, s0_ref, y_ref, sout_ref, s_ref,
                    *, tt, ki_n, fold):
    i = pl.program_id(0)

    @pl.when(i == 0)
    def _():
        s_ref[...] = s0_ref[...]

    def lane_total(x):
        return x + pltpu.roll(x, LANES // 2, axis=1) if fold else x

    half = RW_HD // 2
    halves = (slice(0, half), slice(half, RW_HD))
    zero = jnp.zeros((half, LANES), f32)

    def row(ref, t, ki):
        return ref[t, pl.ds(ki, 1), :]

    def first_sa(vr):
        acc = [zero, zero]
        for ki in range(ki_n):
            acc[ki % 2] = acc[ki % 2] + s_ref[ki, vr, :] * row(nkk_ref, 0, ki)
        return lane_total(acc[0] + acc[1])

    def token(t, sa_pair):
        nxt = jnp.minimum(t + 1, tt - 1)
        sa_next = []
        for vr, sa in zip(halves, sa_pair):
            vt = v_ref[t, vr, :]

            def k_block(kb, carry):
                y0, y1, a0, a1 = carry
                for u in range(RW_UNROLL):
                    ki = kb * RW_UNROLL + u
                    s_new = (s_ref[ki, vr, :] * row(w_ref, t, ki) + sa * row(b_ref, t, ki)
                             + vt * row(k_ref, t, ki))
                    s_ref[ki, vr, :] = s_new
                    if u % 2 == 0:
                        y0 = y0 + s_new * row(r_ref, t, ki)
                        a0 = a0 + s_new * row(nkk_ref, nxt, ki)
                    else:
                        y1 = y1 + s_new * row(r_ref, t, ki)
                        a1 = a1 + s_new * row(nkk_ref, nxt, ki)
                return y0, y1, a0, a1

            n_blocks = ki_n // RW_UNROLL
            if n_blocks == 1:
                y0, y1, a0, a1 = k_block(0, (zero, zero, zero, zero))
            else:
                y0, y1, a0, a1 = lax.fori_loop(0, n_blocks, k_block, (zero, zero, zero, zero))
            y_ref[t, vr, :] = lane_total(y0 + y1)
            sa_next.append(lane_total(a0 + a1))
        return tuple(sa_next)

    lax.fori_loop(0, tt, token, tuple(first_sa(vr) for vr in halves))

    @pl.when(i == pl.num_programs(0) - 1)
    def _():
        sout_ref[...] = s_ref[...]


def _rw_scan(rT, wT, kT, bT, nkkT, vT, s0T, *, tt):
    t_len, ki_n, _ = rT.shape
    fold = ki_n * 2 == RW_HD
    assert fold or ki_n == RW_HD
    tt = min(tt, t_len)
    assert t_len % tt == 0 and ki_n % RW_UNROLL == 0
    op = pl.BlockSpec((tt, ki_n, LANES), lambda i: (i, 0, 0))
    vs = pl.BlockSpec((tt, RW_HD, LANES), lambda i: (i, 0, 0))
    ss = pl.BlockSpec((ki_n, RW_HD, LANES), lambda i: (0, 0, 0))
    return pl.pallas_call(
        functools.partial(_rw_scan_kernel, tt=tt, ki_n=ki_n, fold=fold),
        grid=(t_len // tt,),
        in_specs=[op, op, op, op, op, vs, ss],
        out_specs=[vs, ss],
        out_shape=[jax.ShapeDtypeStruct((t_len, RW_HD, LANES), f32),
                   jax.ShapeDtypeStruct((ki_n, RW_HD, LANES), f32)],
        scratch_shapes=[pltpu.VMEM((ki_n, RW_HD, LANES), f32)],
        compiler_params=_cparams(("arbitrary",)),
    )(rT, wT, kT, bT, nkkT, vT, s0T)


def _rw_post_kernel(y_ref, bonus_ref, g_ref, lw_ref, lb_ref, o_ref):
    hm = _head_matrix()
    y = y_ref[...]
    mean = _head_sum(y, hm) * (1.0 / RW_HD)
    d = y - mean
    var = _head_sum(d * d, hm) * (1.0 / RW_HD)
    yn = d * lax.rsqrt(var + RW_GN_EPS) * lw_ref[...] + lb_ref[...]
    o_ref[...] = ((yn + bonus_ref[...]) * g_ref[...]).astype(o_ref.dtype)


def _rw_post(y, bonus, g, lnw, lnb, layer):
    m = y.shape[0]
    tr = min(512, m)
    spec = pl.BlockSpec((tr, RW_W), lambda i: (i, 0))
    vec = pl.BlockSpec((None, 1, RW_W), lambda i: (layer, 0, 0))
    return pl.pallas_call(
        _rw_post_kernel,
        grid=(m // tr,),
        in_specs=[spec, spec, spec, vec, vec],
        out_specs=spec,
        out_shape=jax.ShapeDtypeStruct((m, RW_W), bf16),
        compiler_params=_cparams(("parallel",)),
    )(y, bonus, g, lnw, lnb)


def _rwkv(p, shift0, s0, wts, layer, *, nb, tp, t_valid, rows):
    H, N = RW_HEADS, RW_HD
    kh = LANES // (nb * H)
    assert kh in (1, 2) and kh * nb * H == LANES
    ki_n = N // kh
    shift_pad = jnp.pad(shift0, ((0, 0), (0, 3 * RW_W + LORA_PAD - RW_COLS)))[:, None, :]
    valid = None if t_valid == tp else t_valid
    r, w, k, v, nkk, bb, bonus, g = _rw_pre(
        p, shift_pad, wts["mu_pad"], wts["w0"], wts["w2p"], wts["a0"], wts["a2p"], wts["g2p"],
        wts["kkw"], wts["kaw"], wts["rkw"], layer, nb=nb, tp=tp, rows=rows, valid=valid)

    def key_lanes(x):
        x = x.reshape(nb, tp, H, kh, ki_n)[:, :t_valid]
        return x.transpose(1, 4, 3, 0, 2).reshape(t_valid, ki_n, LANES)

    vT = jnp.broadcast_to(v.reshape(nb, tp, H, 1, N)[:, :t_valid], (nb, t_valid, H, kh, N))
    vT = vT.transpose(1, 4, 3, 0, 2).reshape(t_valid, N, LANES)
    s0T = s0.reshape(nb, H, N, kh, ki_n).transpose(4, 2, 3, 0, 1).reshape(ki_n, N, LANES)
    yT, sT = _rw_scan(key_lanes(r), key_lanes(w), key_lanes(k), key_lanes(bb), key_lanes(nkk), vT, s0T,
                      tt=64)
    y = yT[:, :, :nb * H].reshape(t_valid, N, nb, H).transpose(2, 0, 3, 1).reshape(nb, t_valid, RW_W)
    if t_valid != tp:
        y = jnp.pad(y, ((0, 0), (0, tp - t_valid), (0, 0)))
    y = y.reshape(nb * tp, RW_W)
    s_out = sT.reshape(ki_n, N, kh, nb, H).transpose(3, 4, 1, 2, 0).reshape(nb, H, N, N)
    o = _rw_post(y, bonus, g, wts["lnw"], wts["lnb"], layer)
    return o, s_out


def _kv_prep_kernel(ak_ref, av_ref, aki_ref, kn_ref, k_o, v_o, ki_o):
    gain = kn_ref[...]
    for n in range(AT_KV_HEADS):
        x = ak_ref[:, n * HEAD_DIM:(n + 1) * HEAD_DIM]
        ms = jnp.mean(x * x, axis=-1, keepdims=True)
        k_o[:, n * HEAD_DIM:(n + 1) * HEAD_DIM] = x * lax.rsqrt(ms + NORM_EPS) * gain
    v_o[...] = av_ref[...]
    ki_o[...] = aki_ref[...]


def _kv_prep(p, k_norm, layer):
    m = p.shape[0]
    tr = min(512, m)
    return pl.pallas_call(
        _kv_prep_kernel,
        grid=(m // tr,),
        in_specs=[pl.BlockSpec((tr, KV_W), lambda i: (i, OFF_AK // KV_W)),
                  pl.BlockSpec((tr, KV_W), lambda i: (i, OFF_AV // KV_W)),
                  pl.BlockSpec((tr, IDX_DIM), lambda i: (i, OFF_AKI // IDX_DIM)),
                  pl.BlockSpec((None, 1, HEAD_DIM), lambda i: (layer, 0, 0))],
        out_specs=[pl.BlockSpec((tr, KV_W), lambda i: (i, 0)),
                   pl.BlockSpec((tr, KV_W), lambda i: (i, 0)),
                   pl.BlockSpec((tr, IDX_DIM), lambda i: (i, 0))],
        out_shape=[jax.ShapeDtypeStruct((m, KV_W), f32), jax.ShapeDtypeStruct((m, KV_W), f32),
                   jax.ShapeDtypeStruct((m, IDX_DIM), f32)],
        compiler_params=_cparams(("parallel",)),
    )(p, p, p, k_norm)


def _index_scores(qi, wi_col, keys):
    rws = qi.shape[0]
    qs = jnp.concatenate([qi[:, h * IDX_DIM:(h + 1) * IDX_DIM] for h in range(IDX_HEADS)], axis=0)
    d = jnp.maximum(_dot_nt(qs, keys, precision=HI), 0.0) * wi_col
    s = d[0:rws]
    for h in range(1, IDX_HEADS):
        s = s + d[h * rws:(h + 1) * rws]
    return s


def _wi_column(awi):
    scale = IDX_HEADS ** -0.5 * IDX_DIM ** -0.5
    return jnp.concatenate([awi[:, h:h + 1] for h in range(IDX_HEADS)], axis=0) * scale


def _sortable(score):
    bits = lax.bitcast_convert_type(score, jnp.int32)
    key = jnp.where(bits < 0, bits ^ jnp.int32(0x7FFFFFFF), bits)
    return jnp.where(score == 0.0, 0, key)


def _select_topk(skey, n_sel, n_keys, axis=1):
    one = tuple(1 if a == axis else s for a, s in enumerate(skey.shape))
    nsel = jnp.float32(n_sel)
    int_min = jnp.int32(-2 ** 31)

    def _count(mask):
        ones = mask.astype(f32)
        if axis == 0 and ones.shape[0] % COUNT_ROWS == 0 and ones.shape[0] > COUNT_ROWS:
            ones = jnp.sum(ones.reshape(-1, COUNT_ROWS, ones.shape[1]), axis=0)
        return jnp.sum(ones, axis=axis, keepdims=True)

    zero = jnp.zeros(one, jnp.int32)
    cand = jnp.where(_count(skey >= zero) >= nsel, zero, zero + int_min)

    def bit_step(it, cand):
        trial = cand + jnp.left_shift(jnp.int32(1), 30 - it)
        return jnp.where(_count(skey >= trial) >= nsel, trial, cand)

    tau = lax.fori_loop(0, 31, bit_step, cand)
    gt = skey > tau
    eq = skey == tau
    need = nsel - _count(gt)
    idx = lax.broadcasted_iota(jnp.int32, skey.shape, axis)
    nbits = int(n_keys).bit_length()

    def idx_step(it, x):
        trial = x + jnp.left_shift(jnp.int32(1), nbits - 1 - it)
        ok = (trial <= n_keys) & (_count(eq & (idx < trial)) < need)
        return jnp.where(ok, trial, x)

    surplus = jnp.max(_count(eq) - need) > 0.0
    x = lax.cond(surplus, lambda: lax.fori_loop(0, nbits, idx_step, zero), lambda: zero + n_keys)
    return gt | (eq & (idx <= x))


def _q_heads(aq, gain, n):
    outs = []
    for g in range(AT_GROUP):
        h = n * AT_GROUP + g
        x = aq[:, h * HEAD_DIM:(h + 1) * HEAD_DIM]
        ms = jnp.mean(x * x, axis=-1, keepdims=True)
        outs.append(x * lax.rsqrt(ms + NORM_EPS) * gain)
    return jnp.concatenate(outs, axis=0)


def _dsa_prompt_kernel(aq_ref, aqi_ref, awi_ref, k_ref, v_ref, ki_ref, qn_ref, o_ref,
                       sel_ref, keep_ref, kb_ref, vt_ref, q_ref, m_ref, l_ref, acc_ref, *, t_len, n_sel):
    i = pl.program_id(1)
    qb = Q_BLOCK
    nkb = t_len // qb
    kc = DSA_KEY_CHUNK if t_len % DSA_KEY_CHUNK == 0 else qb

    @pl.when(i == 0)
    def _():
        kb_ref[...] = k_ref[...].astype(bf16)
        for n in range(AT_KV_HEADS):
            for j in range(nkb):
                tile = v_ref[j * qb:(j + 1) * qb, n * HEAD_DIM:(n + 1) * HEAD_DIM].T.astype(bf16)
                lo = (j * qb) % kc
                vt_ref[n, (j * qb) // kc, :, lo:lo + qb] = tile

    qi = aqi_ref[...]
    qs = jnp.concatenate([qi[:, h * IDX_DIM:(h + 1) * IDX_DIM] for h in range(IDX_HEADS)], axis=0)
    qs_hi, qs_lo = _split_bf16(qs)
    wi_t = awi_ref[...].T
    wi_row = jnp.concatenate([wi_t[h:h + 1, :] for h in range(IDX_HEADS)], axis=1)
    wi_row = wi_row * (IDX_HEADS ** -0.5 * IDX_DIM ** -0.5)
    q_pos = i * qb + lax.broadcasted_iota(jnp.int32, (1, qb), 1)

    for j in range(nkb):
        rows = slice(j * qb, (j + 1) * qb)

        @pl.when(j <= i)
        def _():
            k_hi, k_lo = _split_bf16(ki_ref[rows, :])
            d = _dot_nt(k_hi, qs_hi) + (_dot_nt(k_hi, qs_lo) + _dot_nt(k_lo, qs_hi))
            d = jnp.maximum(d, 0.0) * wi_row
            s = d[:, 0:qb]
            for h in range(1, IDX_HEADS):
                s = s + d[:, h * qb:(h + 1) * qb]
            key_pos = j * qb + lax.broadcasted_iota(jnp.int32, (qb, qb), 0)
            s = jnp.where(key_pos <= q_pos, s, NEG_BIG)
            sel_ref[rows, :] = _sortable(s)

        @pl.when(j > i)
        def _():
            sel_ref[rows, :] = _sortable(jnp.full((qb, qb), NEG_BIG, f32))

    def select(width):
        chosen = _select_topk(sel_ref[:width, :], n_sel, width, axis=0)
        key_pos = lax.broadcasted_iota(jnp.int32, (width, qb), 0)
        keep_ref[:width, :] = jnp.where(chosen & (key_pos <= q_pos), 1.0, 0.0)

    n_tier = DSA_TIERS if nkb % DSA_TIERS == 0 and (nkb // DSA_TIERS * qb) % kc == 0 else 1
    per_tier = nkb // n_tier
    for tier in range(n_tier):
        pl.when(i // per_tier == tier)(functools.partial(select, (tier + 1) * per_tier * qb))

    gain = qn_ref[...]
    for h in range(AT_HEADS):
        x = aq_ref[:, h * HEAD_DIM:(h + 1) * HEAD_DIM]
        ms = jnp.mean(x * x, axis=-1, keepdims=True)
        q_ref[h] = (x * lax.rsqrt(ms + NORM_EPS) * gain).astype(bf16)
    m_ref[...] = jnp.full(m_ref.shape, NEG_BIG, f32)
    l_ref[...] = jnp.zeros(l_ref.shape, f32)
    acc_ref[...] = jnp.zeros(acc_ref.shape, f32)

    def key_chunk(c, carry):
        rows = pl.ds(pl.multiple_of(c * kc, kc), kc)
        keep = keep_ref[rows, :] > 0.5
        for n in range(AT_KV_HEADS):
            kn = kb_ref[rows, n * HEAD_DIM:(n + 1) * HEAD_DIM]
            vnt = vt_ref[n, c]
            for g in range(AT_GROUP):
                h = n * AT_GROUP + g
                s = _dot_nt(kn, q_ref[h]) * (HEAD_DIM ** -0.5 * LOG2_E)
                s = jnp.where(keep, s, NEG_BIG)
                m_old = m_ref[h]
                m_new = jnp.maximum(m_old, jnp.max(s, axis=0, keepdims=True))
                alpha = jnp.exp2(m_old - m_new)
                e = jnp.exp2(s - m_new)
                l_ref[h] = alpha * l_ref[h] + jnp.sum(e, axis=0, keepdims=True)
                acc_ref[h] = alpha * acc_ref[h] + _dot(vnt, e.astype(bf16))
                m_ref[h] = m_new
        return carry

    lax.fori_loop(0, ((i + 1) * qb + kc - 1) // kc, key_chunk, 0)
    for h in range(AT_HEADS):
        o_ref[:, h * HEAD_DIM:(h + 1) * HEAD_DIM] = (acc_ref[h] / l_ref[h]).T.astype(o_ref.dtype)


def _dsa_prompt(p, k, v, ki, q_norm, layer, *, nb, t_len):
    n_sel = min(TOPK_MAX, t_len // 4)
    nq = t_len // Q_BLOCK
    kc = DSA_KEY_CHUNK if t_len % DSA_KEY_CHUNK == 0 else Q_BLOCK
    return pl.pallas_call(
        functools.partial(_dsa_prompt_kernel, t_len=t_len, n_sel=n_sel),
        grid=(nb, nq),
        in_specs=[pl.BlockSpec((Q_BLOCK, AT_W), lambda b, i: (b * nq + i, OFF_AQ // AT_W)),
                  pl.BlockSpec((Q_BLOCK, IDX_HEADS * IDX_DIM),
                               lambda b, i: (b * nq + i, OFF_AQI // (IDX_HEADS * IDX_DIM))),
                  pl.BlockSpec((Q_BLOCK, LANES), lambda b, i: (b * nq + i, OFF_AWI // LANES)),
                  pl.BlockSpec((t_len, KV_W), lambda b, i: (b, 0)),
                  pl.BlockSpec((t_len, KV_W), lambda b, i: (b, 0)),
                  pl.BlockSpec((t_len, IDX_DIM), lambda b, i: (b, 0)),
                  pl.BlockSpec((None, 1, HEAD_DIM), lambda b, i: (layer, 0, 0))],
        out_specs=pl.BlockSpec((Q_BLOCK, AT_W), lambda b, i: (b * nq + i, 0)),
        out_shape=jax.ShapeDtypeStruct((nb * t_len, AT_W), bf16),
        scratch_shapes=[pltpu.VMEM((t_len, Q_BLOCK), jnp.int32),
                        pltpu.VMEM((t_len, Q_BLOCK), f32),
                        pltpu.VMEM((t_len, KV_W), bf16),
                        pltpu.VMEM((AT_KV_HEADS, t_len // kc, HEAD_DIM, kc), bf16),
                        pltpu.VMEM((AT_HEADS, Q_BLOCK, HEAD_DIM), bf16),
                        pltpu.VMEM((AT_HEADS, 1, Q_BLOCK), f32),
                        pltpu.VMEM((AT_HEADS, 1, Q_BLOCK), f32),
                        pltpu.VMEM((AT_HEADS, HEAD_DIM, Q_BLOCK), f32)],
        compiler_params=_cparams(("parallel", "arbitrary")),
    )(p, p, p, k, v, ki, q_norm)


def _dsa_sample_score_kernel(pt_ref, aqi_ref, awi_ref, *refs, n_steps, npg, valid):
    page_refs, knew_ref, o_ref = refs[:npg], refs[npg], refs[npg + 1]
    g = pl.program_id(1)
    rws = SAMPLE_ROWS
    ps = page_refs[0].shape[0]
    qi = aqi_ref[...]
    wi_col = _wi_column(awi_ref[...])

    @pl.when(g < n_steps)
    def _():
        keys = jnp.concatenate([r[...] for r in page_refs], axis=0)
        o_ref[...] = _index_scores(qi, wi_col, keys)

    @pl.when(g == n_steps)
    def _():
        keys = jnp.concatenate([knew_ref[...], jnp.zeros((ps - rws, IDX_DIM), f32)], axis=0)
        s = _index_scores(qi, wi_col, keys)
        key_i = lax.broadcasted_iota(jnp.int32, (rws, ps), 1)
        q_i = lax.broadcasted_iota(jnp.int32, (rws, ps), 0)
        o_ref[:, :ps] = jnp.where((key_i <= q_i) & (key_i < valid), s, NEG_BIG)
        if npg > 1:
            o_ref[:, ps:] = jnp.full((rws, (npg - 1) * ps), NEG_BIG, f32)


def _page_specs(n_pages, npg, page_shape, layer):
    zeros = (0,) * len(page_shape)

    def spec(j):
        return pl.BlockSpec((None, None) + tuple(page_shape),
                            lambda b, g, pt: (layer, pt[b, jnp.minimum(g * npg + j, n_pages - 1)]) + zeros)
    return [spec(j) for j in range(npg)]


def _dsa_sample_scores(p, ki_new, cache_kidx, page_table, layer, *, nb, valid):
    n_pages = page_table.shape[1]
    ps = cache_kidx.shape[2]
    rws = SAMPLE_ROWS
    npg = min(SAMPLE_PAGES, n_pages)
    assert n_pages % npg == 0
    n_steps = n_pages // npg
    grid_spec = pltpu.PrefetchScalarGridSpec(
        num_scalar_prefetch=1,
        grid=(nb, n_steps + 1),
        in_specs=[pl.BlockSpec((rws, IDX_HEADS * IDX_DIM),
                               lambda b, g, pt: (b, OFF_AQI // (IDX_HEADS * IDX_DIM))),
                  pl.BlockSpec((rws, LANES), lambda b, g, pt: (b, OFF_AWI // LANES))]
        + _page_specs(n_pages, npg, (ps, IDX_DIM), layer)
        + [pl.BlockSpec((rws, IDX_DIM), lambda b, g, pt: (b, 0))],
        out_specs=pl.BlockSpec((None, rws, npg * ps), lambda b, g, pt: (b, 0, g)),
    )
    return pl.pallas_call(
        functools.partial(_dsa_sample_score_kernel, n_steps=n_steps, npg=npg, valid=valid),
        grid_spec=grid_spec,
        out_shape=jax.ShapeDtypeStruct((nb, rws, (n_steps + 1) * npg * ps), f32),
        compiler_params=_cparams(("parallel", "arbitrary")),
    )(page_table, p, p, *([cache_kidx] * npg), ki_new)


def _dsa_sample_select_kernel(s_ref, o_ref, *, n_sel, n_keys):
    chosen = _select_topk(_sortable(s_ref[...]), n_sel, n_keys)
    o_ref[...] = jnp.where(chosen & (s_ref[...] > 0.5 * NEG_BIG), 1.0, 0.0)


def _dsa_sample_select(scores, n_sel):
    nb, rws, n_keys = scores.shape
    spec = pl.BlockSpec((None, rws, n_keys), lambda b: (b, 0, 0))
    return pl.pallas_call(
        functools.partial(_dsa_sample_select_kernel, n_sel=n_sel, n_keys=n_keys),
        grid=(nb,),
        in_specs=[spec],
        out_specs=spec,
        out_shape=jax.ShapeDtypeStruct(scores.shape, f32),
        compiler_params=_cparams(("parallel",)),
    )(scores)


def _dsa_sample_attn_kernel(pt_ref, aq_ref, keep_ref, *refs, n_steps, npg):
    kpage_refs, vpage_refs = refs[:npg], refs[npg:2 * npg]
    knew_ref, vnew_ref, qn_ref, o_ref, q_ref, m_ref, l_ref, acc_ref = refs[2 * npg:]
    g = pl.program_id(1)
    rws = SAMPLE_ROWS
    ps = kpage_refs[0].shape[0] // AT_KV_HEADS

    @pl.when(g == 0)
    def _():
        m_ref[...] = jnp.full(m_ref.shape, NEG_BIG, f32)
        l_ref[...] = jnp.zeros(l_ref.shape, f32)
        acc_ref[...] = jnp.zeros(acc_ref.shape, f32)
        for n in range(AT_KV_HEADS):
            q_ref[n] = _q_heads(aq_ref[...], qn_ref[...], n).astype(bf16)

    def step(k_of, v_of, keep):
        keep4 = jnp.concatenate([keep] * AT_GROUP, axis=0) > 0.5
        for n in range(AT_KV_HEADS):
            s = _dot_nt(q_ref[n], k_of(n)) * (HEAD_DIM ** -0.5)
            s = jnp.where(keep4, s, NEG_BIG)
            m_old = m_ref[n]
            m_new = jnp.maximum(m_old, jnp.max(s, axis=1, keepdims=True))
            alpha = jnp.exp(m_old - m_new)
            e = jnp.where(keep4, jnp.exp(s - m_new), 0.0)
            l_ref[n] = alpha * l_ref[n] + jnp.sum(e, axis=1, keepdims=True)
            acc_ref[n] = alpha * acc_ref[n] + _dot(e.astype(bf16), v_of(n))
            m_ref[n] = m_new

    def page_head(pages, n):
        rows = pl.ds(n, ps, stride=AT_KV_HEADS)
        return jnp.concatenate([pg[rows, :].astype(bf16) for pg in pages], axis=0)

    def new_head(block, n):
        return block[:, n * HEAD_DIM:(n + 1) * HEAD_DIM].astype(bf16)

    @pl.when(g < n_steps)
    def _():
        step(functools.partial(page_head, kpage_refs), functools.partial(page_head, vpage_refs), keep_ref[...])

    @pl.when(g == n_steps)
    def _():
        pad = jnp.zeros((ps - rws, KV_W), f32)
        knew = jnp.concatenate([knew_ref[...], pad], axis=0)
        vnew = jnp.concatenate([vnew_ref[...], pad], axis=0)
        step(functools.partial(new_head, knew), functools.partial(new_head, vnew), keep_ref[:, :ps])
        for n in range(AT_KV_HEADS):
            o = acc_ref[n] / l_ref[n]
            for gq in range(AT_GROUP):
                h = n * AT_GROUP + gq
                o_ref[:, h * HEAD_DIM:(h + 1) * HEAD_DIM] = o[gq * rws:(gq + 1) * rws].astype(o_ref.dtype)


def _dsa_sample_attn(p, keep, cache_k, cache_v, k_new, v_new, q_norm, page_table, layer, *, nb):
    n_pages = page_table.shape[1]
    ps = cache_k.shape[2] // AT_KV_HEADS
    rws = SAMPLE_ROWS
    npg = min(SAMPLE_PAGES, n_pages)
    n_steps = n_pages // npg
    new =pl.BlockSpec((rws, KV_W), lambda b, g, pt: (b, 0))
    grid_spec = pltpu.PrefetchScalarGridSpec(
        num_scalar_prefetch=1,
        grid=(nb, n_steps + 1),
        in_specs=[pl.BlockSpec((rws, AT_W), lambda b, g, pt: (b, OFF_AQ // AT_W)),
                  pl.BlockSpec((None, rws, npg * ps), lambda b, g, pt: (b, 0, g))]
        + 2 * _page_specs(n_pages, npg, (ps * AT_KV_HEADS, HEAD_DIM), layer)
        + [new, new, pl.BlockSpec((None, 1, HEAD_DIM), lambda b, g, pt: (layer, 0, 0))],
        out_specs=pl.BlockSpec((rws, AT_W), lambda b, g, pt: (b, 0)),
        scratch_shapes=[pltpu.VMEM((AT_KV_HEADS, AT_GROUP * rws, HEAD_DIM), bf16),
                        pltpu.VMEM((AT_KV_HEADS, AT_GROUP * rws, 1), f32),
                        pltpu.VMEM((AT_KV_HEADS, AT_GROUP * rws, 1), f32),
                        pltpu.VMEM((AT_KV_HEADS, AT_GROUP * rws, HEAD_DIM), f32)],
    )
    return pl.pallas_call(
        functools.partial(_dsa_sample_attn_kernel, n_steps=n_steps, npg=npg),
        grid_spec=grid_spec,
        out_shape=jax.ShapeDtypeStruct((nb * rws, AT_W), bf16),
        compiler_params=_cparams(("parallel", "arbitrary")),
    )(page_table, p, keep, *([cache_k] * npg), *([cache_v] * npg), k_new, v_new, q_norm)


def _prep_weights(w_in, rwkv_mu, rwkv_w2, rwkv_a2, rwkv_g2, w_out, w_gate, w_up, w_down):
    depth, d, _ = w_in.shape
    hg, rw, at = jnp.split(w_in.astype(bf16), [HG_COLS, HG_COLS + RW_COLS], axis=-1)
    aq, ak, av, aqi, aki, awi = jnp.split(at, np.cumsum([AT_W, KV_W, KV_W, IDX_HEADS * IDX_DIM, IDX_DIM])
                                          .tolist(), axis=-1)
    r, k, v, lora = jnp.split(rw, [RW_W, 2 * RW_W, 3 * RW_W], axis=-1)
    z = lambda n: jnp.zeros((depth, d, n), bf16)
    w_in_p = jnp.concatenate(
        [hg, aq, aqi, ak, av, aki, awi, z(OFF_LORA - OFF_AWI - IDX_HEADS), lora, z(LORA_PAD - RW_LORA), r, k, v],
        axis=-1)
    assert w_in_p.shape[-1] == IN_PAD
    mu_r, mu_l = rwkv_mu[:, :3 * RW_W], rwkv_mu[:, 3 * RW_W:]
    mu_pad = jnp.concatenate([mu_r, mu_l, jnp.zeros((depth, LORA_PAD - RW_LORA), f32)], axis=-1)[:, None, :]
    zl = lambda n: jnp.zeros((depth, n, RW_W), bf16)
    w2p = jnp.concatenate([rwkv_w2.astype(bf16), zl(LORA_PAD - RW_DECAY_LORA)], axis=1)
    a2p = jnp.concatenate([zl(RW_DECAY_LORA), rwkv_a2.astype(bf16),
                           zl(LORA_PAD - RW_DECAY_LORA - RW_AAA_LORA)], axis=1)
    g2p = jnp.concatenate([zl(RW_DECAY_LORA + RW_AAA_LORA), rwkv_g2.astype(bf16), zl(LORA_PAD - RW_LORA)], axis=1)
    return dict(w_in=w_in_p, mu_pad=mu_pad, w2p=w2p, a2p=a2p, g2p=g2p,
                w_out=w_out, w_gate=w_gate, w_up=w_up, w_down=w_down.astype(bf16))


def _shift_row(p_row):
    return jnp.concatenate([p_row[..., OFF_R:OFF_R + 3 * RW_W], p_row[..., OFF_LORA:OFF_LORA + RW_LORA]], axis=-1)


def _k_tile(f):
    half = f // 2
    return half if f % 2 == 0 and half % LANES == 0 else f


def _layer(x, layer, wts, *, nb, tp, t_valid, hg_s0, rw_s0, shift0, attend):
    h = _rmsnorm(x, wts["ln1"], layer)
    p = _matmul(h, wts["w_in"], layer, tm=1024, tn=512, tk=h.shape[1])
    valid = None if t_valid == tp else t_valid
    rows = min(256, tp)
    o_hg, hg_s = _gla(p, wts["lbs"], wts["hgrn_norm"], hg_s0, layer, nb=nb, tp=tp, rows=rows, valid=valid)
    o_rw, rw_s = _rwkv(p, shift0, rw_s0, wts, layer, nb=nb, tp=tp, t_valid=t_valid, rows=rows)
    k, v, ki = _kv_prep(p, wts["k_norm"], layer)
    o_at = attend(p, k, v, ki)
    mix = jnp.concatenate([o_hg, o_rw, o_at], axis=-1)
    x = _matmul(mix, wts["w_out"], layer, tm=1024, tn=512, tk=mix.shape[1], res=x)
    h2 = _rmsnorm(x, wts["ln2"], layer)
    act = _swiglu(h2, wts["w_gate"], wts["w_up"], layer, tm=2048, tn=256)
    x = _matmul(act, wts["w_down"], layer, tm=512, tn=512, tk=act.shape[1], res=x)
    shift = _shift_row(p.reshape(nb, tp, IN_PAD)[:, t_valid - 1])
    return x, (k, v, ki, hg_s, rw_s, shift)


def kernel(x_prompt, x_sample, cache_k, cache_v, cache_kidx, state_hgrn, state_rwkv, state_shift, page_table,
           ln1, w_in, hgrn_lb, hgrn_norm, rwkv_mu, rwkv_w0, rwkv_w2, rwkv_a0, rwkv_a2, rwkv_g2, rwkv_kk,
           rwkv_ka, rwkv_rk, rwkv_lnx_w, rwkv_lnx_b, q_norm, k_norm, w_out, ln2, w_gate, w_up, w_down):
    depth = w_in.shape[0]
    B, T, D = x_prompt.shape
    DB, DS, _ = x_sample.shape
    n_pool, page_size = cache_k.shape[1], cache_k.shape[2]
    ck = cache_k.reshape(depth, n_pool, page_size * AT_KV_HEADS, HEAD_DIM)
    cv = cache_v.reshape(depth, n_pool, page_size * AT_KV_HEADS, HEAD_DIM)
    past = page_table.shape[1] * page_size

    wts = _prep_weights(w_in, rwkv_mu, rwkv_w2, rwkv_a2, rwkv_g2, w_out, w_gate, w_up, w_down)
    lb_p = jax.nn.softmax(hgrn_lb.astype(f32), axis=0)
    row = lambda a: a.astype(f32).reshape(depth, 1, -1)
    wts.update(lbs=(jnp.cumsum(lb_p, axis=0) - lb_p[0:1])[:, None, :], hgrn_norm=row(hgrn_norm),
               ln1=row(ln1), ln2=row(ln2), w0=row(rwkv_w0), a0=row(rwkv_a0), kkw=row(rwkv_kk),
               kaw=row(rwkv_ka), rkw=row(rwkv_rk), lnw=row(rwkv_lnx_w), lnb=row(rwkv_lnx_b),
               q_norm=row(q_norm), k_norm=row(k_norm))

    xp = x_prompt.reshape(B * T, D)
    outs_p = []
    zeros_hg = jnp.zeros((B, HG_HEADS, HG_DK, HG_DV), f32)
    zeros_rw = jnp.zeros((B, RW_HEADS, RW_HD, RW_HD), f32)
    zeros_sh = jnp.zeros((B, RW_COLS), f32)
    for l in range(depth):
        attend = lambda p, k, v, ki, l=l: _dsa_prompt(p, k, v, ki, wts["q_norm"], l, nb=B, t_len=T)
        xp, st = _layer(xp, l, wts, nb=B, tp=T, t_valid=T, hg_s0=zeros_hg, rw_s0=zeros_rw, shift0=zeros_sh,
                        attend=attend)
        outs_p.append(st)

    TP = SAMPLE_ROWS
    xs = jnp.pad(x_sample, ((0, 0), (0, TP - DS), (0, 0))).reshape(DB * TP, D)
    n_sel_s = min(TOPK_MAX, (past + DS) // 4)
    outs_s = []
    for l in range(depth):
        def attend(p, k, v, ki, l=l):
            scores = _dsa_sample_scores(p, ki, cache_kidx, page_table, l, nb=DB, valid=DS)
            keep = _dsa_sample_select(scores, n_sel_s)
            return _dsa_sample_attn(p, keep, ck, cv, k, v, wts["q_norm"], page_table, l, nb=DB)

        xs, st = _layer(xs, l, wts, nb=DB, tp=TP, t_valid=DS, hg_s0=state_hgrn[l], rw_s0=state_rwkv[l],
                        shift0=state_shift[l], attend=attend)
        outs_s.append(st)

    def stack(outs, i):
        return jnp.stack([o[i] for o in outs])

    k_p = stack(outs_p, 0).reshape(depth, B, T, AT_KV_HEADS, HEAD_DIM)
    v_p = stack(outs_p, 1).reshape(depth, B, T, AT_KV_HEADS, HEAD_DIM)
    ki_p = stack(outs_p, 2).reshape(depth, B, T, IDX_DIM)
    cut = lambda a, w: a.reshape(depth, DB, TP, *w)[:, :, :DS]
    k_s = cut(stack(outs_s, 0), (AT_KV_HEADS, HEAD_DIM))
    v_s = cut(stack(outs_s, 1), (AT_KV_HEADS, HEAD_DIM))
    ki_s = cut(stack(outs_s, 2), (IDX_DIM,))
    y_p = xp.reshape(B, T, D)
    y_s = xs.reshape(DB, TP, D)[:, :DS]
    return (y_p, y_s, k_p, v_p, ki_p, stack(outs_p, 3), stack(outs_p, 4), stack(outs_p, 5),
            k_s, v_s, ki_s, stack(outs_s, 3), stack(outs_s, 4), stack(outs_s, 5))
```
